```python
import functools
import jax
import jax.numpy as jnp
from jax import lax
import numpy as np

D_MODEL = 2048
BATCH = 8
SEQ = 2048
DEPTH = 2
DEC_BATCH = 32
DEC_SEQ = 4
PAST_LEN = 8192
PAGE_SIZE = 128

MIX_WIDTH = D_MODEL
A_HEAD_DIM = 128
A_WIDTH = MIX_WIDTH // 2
A_HEADS = A_WIDTH // A_HEAD_DIM
A_BLOCK = 256
A_TOPK = 3
A_QCHUNK = 128
R_WIDTH = MIX_WIDTH // 4
R_HEAD_DIM = 64
R_HEADS = R_WIDTH // R_HEAD_DIM
R_DECAY_LORA = 64
R_A_LORA = 64
R_G_LORA = 128
R_COLS = 3 * R_WIDTH + R_DECAY_LORA + R_A_LORA + R_G_LORA
R_SPLITS = (R_WIDTH, 2 * R_WIDTH, 3 * R_WIDTH, 3 * R_WIDTH + R_DECAY_LORA, 3 * R_WIDTH + R_DECAY_LORA + R_A_LORA)
R_GN_EPS = 64e-5
C_WIDTH = MIX_WIDTH - A_WIDTH - R_WIDTH
POOL_WINDOWS = (2, 4, 8, 16)
C_GROUPS = len(POOL_WINDOWS)
C_GROUP = C_WIDTH // C_GROUPS
POOL_MAX = max(POOL_WINDOWS)
IN_COLS = 3 * A_WIDTH + R_COLS + C_WIDTH
IN_SPLITS = (A_WIDTH, 2 * A_WIDTH, 3 * A_WIDTH, 3 * A_WIDTH + R_COLS)
N_MEM = 256
M_HEADS = 4
M_HEAD_DIM = 128
M_WIDTH = M_HEADS * M_HEAD_DIM
D_FF = ((8 * D_MODEL // 3 + 255) // 256) * 256
RMS_EPS = 1e-6

kernel_name = 'hybrid_moba_rwkv7_pool_decoder_step'


def rmsnorm(x, g):
    x32 = x.astype(jnp.float32)
    y = x32 * lax.rsqrt(jnp.mean(x32 * x32, axis=-1, keepdims=True) + RMS_EPS)
    return (y * g.astype(jnp.float32)).astype(x.dtype)


def swiglu(h, wg, wu, wd):
    return (jax.nn.silu(h @ wg) * (h @ wu)) @ wd


def alibi_slopes(n):
    return jnp.exp2(-8.0 * jnp.arange(1, n + 1, dtype=jnp.float32) / n)


def moba_select(q, kmean, pos_q):
    g = jnp.einsum('hqd,hnd->hqn', q.astype(jnp.float32), kmean)
    blk = jnp.arange(kmean.shape[1], dtype=jnp.int32)
    cand = blk[None, None, :] < (pos_q // A_BLOCK)[None, :, None]
    vals, sel = lax.top_k(jnp.where(cand, g, -jnp.inf), A_TOPK)
    return sel, jnp.isfinite(vals)


def moba_attend(q, pos_q, k_sel, v_sel, sel_start, sel_valid, k_own, v_own, own_start, slopes):
    H, Q, hd = q.shape
    f32 = jnp.float32
    scale = hd ** -0.5
    offs = jnp.arange(A_BLOCK, dtype=jnp.int32)
    d_sel = (pos_q[None, :, None, None] - (sel_start[..., None] + offs)).astype(f32)
    s_sel = jnp.einsum('hqd,hqnkd->hqnk', q, k_sel).astype(f32) * scale - slopes[:, None, None, None] * d_sel
    s_sel = jnp.where(sel_valid[..., None], s_sel, -jnp.inf)
    pos_own = own_start + offs
    d_own = (pos_q[:, None] - pos_own[None, :]).astype(f32)
    s_own = jnp.einsum('hqd,hkd->hqk', q, k_own).astype(f32) * scale - slopes[:, None, None] * d_own
    s_own = jnp.where((pos_own[None, :] <= pos_q[:, None])[None], s_own, -jnp.inf)
    n_sel = A_TOPK * A_BLOCK
    p = jax.nn.softmax(jnp.concatenate([s_sel.reshape(H, Q, n_sel), s_own], axis=-1), axis=-1).astype(v_own.dtype)
    o = jnp.einsum('hqnk,hqnkd->hqd', p[..., :n_sel].reshape(H, Q, A_TOPK, A_BLOCK), v_sel)
    return o + jnp.einsum('hqk,hkd->hqd', p[..., n_sel:], v_own)


def moba_prompt(q, k, v, slopes):
    B, T, H, hd = q.shape
    nb = max(-(-T // A_BLOCK), A_TOPK)
    pad = nb * A_BLOCK - T

    def blocks(z):
        z = jnp.pad(z, ((0, 0), (0, pad), (0, 0), (0, 0)))
        return z.reshape(B, nb, A_BLOCK, H, hd).transpose(0, 3, 1, 2, 4)

    kb, vb = blocks(k), blocks(v)
    kmean = kb.astype(jnp.float32).mean(axis=3)
    nq = T // A_QCHUNK
    qc = q.reshape(B, nq, A_QCHUNK, H, hd).transpose(0, 1, 3, 2, 4)
    h_idx = jnp.arange(H)[:, None, None]
    offs = jnp.arange(A_QCHUNK, dtype=jnp.int32)

    def per_seq(args):
        q_s, kb_s, vb_s, km_s = args

        def per_chunk(cargs):
            q_c, c = cargs
            pos_q = c * A_QCHUNK + offs
            own = (c * A_QCHUNK) // A_BLOCK
            sel, valid = moba_select(q_c, km_s, pos_q)
            k_own = lax.dynamic_index_in_dim(kb_s, own, axis=1, keepdims=False)
            v_own = lax.dynamic_index_in_dim(vb_s, own, axis=1, keepdims=False)
            return moba_attend(q_c, pos_q, kb_s[h_idx, sel], vb_s[h_idx, sel], sel * A_BLOCK, valid,
                               k_own, v_own, own * A_BLOCK, slopes)

        return lax.map(per_chunk, (q_s, jnp.arange(nq, dtype=jnp.int32)))

    o = lax.map(per_seq, (qc, kb, vb, kmean))
    return o.transpose(0, 1, 3, 2, 4).reshape(B, T, H * hd)


def moba_sample(q, k_new, v_new, cache_k, cache_v, layer, page_table, slopes):
    DB, Tn, H, hd = q.shape
    n_pages = page_table.shape[1]
    past = n_pages * PAGE_SIZE
    ppb = A_BLOCK // PAGE_SIZE
    nbf = past // A_BLOCK
    nb = max(nbf, A_TOPK)
    k_full = cache_k[layer, page_table[:, :nbf * ppb]].reshape(DB, nbf, A_BLOCK, H, hd)
    kmean = k_full.astype(jnp.float32).mean(axis=2).transpose(0, 2, 1, 3)
    kmean = jnp.pad(kmean, ((0, 0), (0, 0), (0, nb - nbf), (0, 0)))
    own_start = nbf * A_BLOCK
    n_tail = past - own_start
    tail = page_table[:, nbf * ppb:]

    def own_rows(cache, new):
        rows = cache[layer, tail].reshape(DB, n_tail, H, hd).astype(new.dtype)
        rows = jnp.concatenate([rows, new], axis=1)
        rows = jnp.pad(rows, ((0, 0), (0, A_BLOCK - n_tail - Tn), (0, 0), (0, 0)))
        return rows.transpose(0, 2, 1, 3)

    k_own, v_own = own_rows(cache_k, k_new), own_rows(cache_v, v_new)
    pos_q = past + jnp.arange(Tn, dtype=jnp.int32)
    qh = q.transpose(0, 2, 1, 3)
    sel, valid = jax.vmap(moba_select, in_axes=(0, 0, None))(qh, kmean, pos_q)
    blk = jnp.where(valid, sel, 0)
    page_idx = jnp.clip(blk[..., None] * ppb + jnp.arange(ppb, dtype=jnp.int32), 0, n_pages - 1)
    phys = page_table[jnp.arange(DB)[:, None, None, None, None], page_idx]
    rows = jnp.arange(PAGE_SIZE)
    head_b = jnp.arange(H)[None, :, None, None, None, None]

    def sel_rows(cache):
        g = cache[layer, phys[..., None], rows, head_b]
        return g.reshape(DB, H, Tn, A_TOPK, A_BLOCK, hd).astype(q.dtype)

    o = jax.vmap(moba_attend, in_axes=(0, None, 0, 0, 0, 0, 0, 0, None, None))(
        qh, pos_q, sel_rows(cache_k), sel_rows(cache_v), blk * A_BLOCK, valid, k_own, v_own,
        jnp.int32(own_start), slopes)
    return o.transpose(0, 2, 1, 3).reshape(DB, Tn, H * hd)


def rwkv_mix(u, shift0, wkv0, mu, w0, w2, a0, a2, g2, k_k, k_a, r_k, lnx_w, lnx_b):
    B, T, _ = u.shape
    f32 = jnp.float32
    prev = jnp.concatenate([shift0[:, None].astype(u.dtype), u[:, :-1]], axis=1)
    xs = (u + (prev - u) * mu).astype(f32)
    r, k, v, wd, ad, gd = jnp.split(xs, R_SPLITS, axis=-1)
    w_log = -jax.nn.softplus(-(w0 + jnp.tanh(wd) @ w2)) - 0.5
    decay = jnp.exp(-jnp.exp(w_log))
    a = jax.nn.sigmoid(a0 + ad @ a2)
    g = jax.nn.sigmoid(gd) @ g2
    hs = lambda z: z.reshape(B, T, R_HEADS, R_HEAD_DIM)
    kk = hs(k * k_k)
    kk = kk / jnp.maximum(jnp.sqrt(jnp.sum(kk * kk, axis=-1, keepdims=True)), 1e-12)
    k = k * (1.0 + (a - 1.0) * k_a)
    r4, k4, v4, w4, a4 = hs(r), hs(k), hs(v), hs(decay), hs(a)

    def step(S, inp):
        rt, wt, kt, vt, kkt, at = inp
        sa = jnp.einsum('bhvk,bhk->bhv', S, kkt)
        S = S * wt[:, :, None, :] - sa[..., None] * (kkt * at)[:, :, None, :] + vt[..., None] * kt[:, :, None, :]
        return S, jnp.einsum('bhvk,bhk->bhv', S, rt)

    tm = lambda z: jnp.swapaxes(z, 0, 1)
    S, y = lax.scan(step, wkv0.astype(f32), (tm(r4), tm(w4), tm(k4), tm(v4), tm(kk), tm(a4)))
    y = tm(y)
    mean = jnp.mean(y, axis=-1, keepdims=True)
    var = jnp.mean(jnp.square(y - mean), axis=-1, keepdims=True)
    y = ((y - mean) * lax.rsqrt(var + R_GN_EPS)).reshape(B, T, R_WIDTH) * lnx_w + lnx_b
    bonus = (jnp.sum(r4 * k4 * r_k, axis=-1, keepdims=True) * v4).reshape(B, T, R_WIDTH)
    return ((y + bonus) * g).astype(u.dtype), S, u[:, -1]


def pool_mix(u, prev, pos, pool_w, pool_scale):
    B, T, _ = u.shape
    P = prev.shape[1]
    ext = jnp.concatenate([prev.astype(u.dtype), u], axis=1)
    e32 = ext.astype(jnp.float32)
    cs = jnp.concatenate([jnp.zeros((B, 1, C_WIDTH), jnp.float32), jnp.cumsum(e32, axis=1)], axis=1)
    cur = e32[:, P:]
    means = []
    for gi, w in enumerate(POOL_WINDOWS):
        sl = slice(gi * C_GROUP, (gi + 1) * C_GROUP)
        win = cs[:, P + 1:, sl] - cs[:, P + 1 - w:P + 1 - w + T, sl]
        cnt = jnp.minimum(pos + 1, w).astype(jnp.float32)[None, :, None]
        means.append(win / cnt - cur[..., sl])
    m = jnp.stack(means, axis=2).astype(u.dtype)
    z = jnp.einsum('btgc,gcd->btgd', m, pool_w).reshape(B, T, C_WIDTH)
    return z * pool_scale, ext[:, -P:]


def memory_kv(mem, g, wk, wv):
    b, m, _ = mem.shape
    hm = rmsnorm(mem, g)
    return (hm @ wk).reshape(b, m, M_HEADS, M_HEAD_DIM), (hm @ wv).reshape(b, m, M_HEADS, M_HEAD_DIM)


def cross_attend(h, mk, mv, wq, wo):
    b, t, _ = h.shape
    q = (h @ wq).reshape(b, t, M_HEADS, M_HEAD_DIM)
    s = jnp.einsum('bthd,bmhd->bhtm', q, mk.astype(q.dtype)).astype(jnp.float32) * (M_HEAD_DIM ** -0.5)
    p = jax.nn.softmax(s, axis=-1).astype(q.dtype)
    o = jnp.einsum('bhtm,bmhd->bthd', p, mv.astype(q.dtype)).reshape(b, t, M_WIDTH)
    return o @ wo


def layer_forward(x, pos, prm, moba_fn, shift0, wkv0, pool0, mem_k, mem_v):
    b, t, _ = x.shape
    x = x + 0.5 * swiglu(rmsnorm(x, prm['norm_ffn1']), prm['ffn1_gate'], prm['ffn1_up'], prm['ffn1_down'])
    u = rmsnorm(x, prm['norm_mix']) @ prm['w_in']
    qa, ka, va, ur, uc = jnp.split(u, IN_SPLITS, axis=-1)
    heads = lambda z: z.reshape(b, t, A_HEADS, A_HEAD_DIM)
    qa, ka, va = heads(qa), heads(ka), heads(va)
    ya = moba_fn(qa, ka, va)
    yr, wkv, shift = rwkv_mix(ur, shift0, wkv0, prm['rw_mu'], prm['rw_w0'], prm['rw_w2'], prm['rw_a0'],
                              prm['rw_a2'], prm['rw_g2'], prm['rw_kk'], prm['rw_ka'], prm['rw_rk'],
                              prm['rw_lnx_w'], prm['rw_lnx_b'])
    yc, pool_buf = pool_mix(uc, pool0, pos, prm['pool_w'], prm['pool_scale'])
    y_mix = jnp.concatenate([ya.astype(x.dtype), yr.astype(x.dtype), yc.astype(x.dtype)], axis=-1)
    x = x + y_mix @ prm['w_out']
    x = x + cross_attend(rmsnorm(x, prm['norm_cross']), mem_k, mem_v, prm['mem_wq'], prm['mem_wo'])
    x = x + 0.5 * swiglu(rmsnorm(x, prm['norm_ffn2']), prm['ffn2_gate'], prm['ffn2_up'], prm['ffn2_down'])
    return x, ka, va, wkv, shift, pool_buf


def setup_inputs(seed: int = 0) -> dict:
    key = jax.random.key(seed)
    ks = iter(jax.random.split(key, 48))
    f32 = jnp.float32
    L = DEPTH

    def nrm(shape, scale=1.0):
        return jax.random.normal(next(ks), shape, f32) * scale

    def gain(shape):
        return 1.0 + 0.05 * nrm(shape)

    n_pages = PAST_LEN // PAGE_SIZE
    n_used = DEC_BATCH * n_pages
    n_pool = n_used + max(1, n_used // 4)
    page_table = jax.random.permutation(next(ks), n_pool)[:n_used].reshape(DEC_BATCH, n_pages).astype(jnp.int32)
    return {
        'x_prompt': nrm((BATCH, SEQ, D_MODEL)),
        'x_sample': nrm((DEC_BATCH, DEC_SEQ, D_MODEL)),
        'cache_k': nrm((L, n_pool, PAGE_SIZE, A_HEADS, A_HEAD_DIM)),
        'cache_v': nrm((L, n_pool, PAGE_SIZE, A_HEADS, A_HEAD_DIM)),
        'cache_mem_k': nrm((L, DEC_BATCH, N_MEM, M_HEADS, M_HEAD_DIM)),
        'cache_mem_v': nrm((L, DEC_BATCH, N_MEM, M_HEADS, M_HEAD_DIM)),
        'state_wkv': nrm((L, DEC_BATCH, R_HEADS, R_HEAD_DIM, R_HEAD_DIM), 0.1),
        'state_shift': nrm((L, DEC_BATCH, R_COLS)),
        'state_pool': nrm((L, DEC_BATCH, POOL_MAX - 1, C_WIDTH)),
        'page_table': page_table,
        'mem_prompt': nrm((BATCH, N_MEM, D_MODEL)),
        'norm_ffn1': gain((L, D_MODEL)),
        'ffn1_gate': nrm((L, D_MODEL, D_FF), D_MODEL ** -0.5),
        'ffn1_up': nrm((L, D_MODEL, D_FF), D_MODEL ** -0.5),
        'ffn1_down': nrm((L, D_FF, D_MODEL), D_FF ** -0.5),
        'norm_mix': gain((L, D_MODEL)),
        'w_in': nrm((L, D_MODEL, IN_COLS), D_MODEL ** -0.5),
        'w_out': nrm((L, MIX_WIDTH, D_MODEL), MIX_WIDTH ** -0.5),
        'rw_mu': jax.random.uniform(next(ks), (L, R_COLS), f32),
        'rw_w0': jax.random.uniform(next(ks), (L, R_WIDTH), f32, -6.0, -1.0),
        'rw_w2': nrm((L, R_DECAY_LORA, R_WIDTH), 0.1 * R_DECAY_LORA ** -0.5),
        'rw_a0': nrm((L, R_WIDTH), 0.1),
        'rw_a2': nrm((L, R_A_LORA, R_WIDTH), R_A_LORA ** -0.5),
        'rw_g2': nrm((L, R_G_LORA, R_WIDTH), R_G_LORA ** -0.5),
        'rw_kk': 0.85 + 0.05 * nrm((L, R_WIDTH)),
        'rw_ka': gain((L, R_WIDTH)),
        'rw_rk': nrm((L, R_HEADS, R_HEAD_DIM), 0.1),
        'rw_lnx_w': gain((L, R_WIDTH)),
        'rw_lnx_b': nrm((L, R_WIDTH), 0.02),
        'pool_w': nrm((L, C_GROUPS, C_GROUP, C_GROUP), C_GROUP ** -0.5),
        'pool_scale': 0.5 + 0.05 * nrm((L, C_WIDTH)),
        'norm_cross': gain((L, D_MODEL)),
        'norm_mem': gain((L, D_MODEL)),
        'mem_wq': nrm((L, D_MODEL, M_WIDTH), D_MODEL ** -0.5),
        'mem_wk': nrm((L, D_MODEL, M_WIDTH), D_MODEL ** -0.5),
        'mem_wv': nrm((L, D_MODEL, M_WIDTH), D_MODEL ** -0.5),
        'mem_wo': nrm((L, M_WIDTH, D_MODEL), M_WIDTH ** -0.5),
        'norm_ffn2': gain((L, D_MODEL)),
        'ffn2_gate': nrm((L, D_MODEL, D_FF), D_MODEL ** -0.5),
        'ffn2_up': nrm((L, D_MODEL, D_FF), D_MODEL ** -0.5),
        'ffn2_down': nrm((L, D_FF, D_MODEL), D_FF ** -0.5),
        'norm_final': gain((D_MODEL,)),
    }


def reference(x_prompt, x_sample, cache_k, cache_v, cache_mem_k, cache_mem_v, state_wkv, state_shift,
              state_pool, page_table, mem_prompt, norm_ffn1, ffn1_gate, ffn1_up, ffn1_down, norm_mix,
              w_in, w_out, rw_mu, rw_w0, rw_w2, rw_a0, rw_a2, rw_g2, rw_kk, rw_ka, rw_rk, rw_lnx_w,
              rw_lnx_b, pool_w, pool_scale, norm_cross, norm_mem, mem_wq, mem_wk, mem_wv, mem_wo,
              norm_ffn2, ffn2_gate, ffn2_up, ffn2_down, norm_final):
    slopes = alibi_slopes(A_HEADS)
    bp, tp = x_prompt.shape[0], x_prompt.shape[1]
    ts = x_sample.shape[1]
    past = page_table.shape[1] * PAGE_SIZE
    pos_p = jnp.arange(tp, dtype=jnp.int32)
    pos_s = past + jnp.arange(ts, dtype=jnp.int32)
    shift0 = jnp.zeros((bp, R_COLS), x_prompt.dtype)
    wkv0 = jnp.zeros((bp, R_HEADS, R_HEAD_DIM, R_HEAD_DIM), jnp.float32)
    pool0 = jnp.zeros((bp, POOL_MAX - 1, C_WIDTH), x_prompt.dtype)
    moba_p = functools.partial(moba_prompt, slopes=slopes)
    xp, xs = x_prompt, x_sample
    kp, vp, wkp, shp, plp, mkp, mvp = [], [], [], [], [], [], []
    ksm, vsm, wks, shs, pls = [], [], [], [], []
    for l in range(DEPTH):
        prm = dict(norm_ffn1=norm_ffn1[l], ffn1_gate=ffn1_gate[l], ffn1_up=ffn1_up[l], ffn1_down=ffn1_down[l],
                   norm_mix=norm_mix[l], w_in=w_in[l], w_out=w_out[l], rw_mu=rw_mu[l], rw_w0=rw_w0[l],
                   rw_w2=rw_w2[l], rw_a0=rw_a0[l], rw_a2=rw_a2[l], rw_g2=rw_g2[l], rw_kk=rw_kk[l],
                   rw_ka=rw_ka[l], rw_rk=rw_rk[l], rw_lnx_w=rw_lnx_w[l], rw_lnx_b=rw_lnx_b[l],
                   pool_w=pool_w[l], pool_scale=pool_scale[l], norm_cross=norm_cross[l],
                   mem_wq=mem_wq[l], mem_wo=mem_wo[l], norm_ffn2=norm_ffn2[l], ffn2_gate=ffn2_gate[l],
                   ffn2_up=ffn2_up[l], ffn2_down=ffn2_down[l])
        mk, mv = memory_kv(mem_prompt, norm_mem[l], mem_wk[l], mem_wv[l])
        xp, k_, v_, w_, sh_, pl_ = layer_forward(xp, pos_p, prm, moba_p, shift0, wkv0, pool0, mk, mv)
        kp.append(k_); vp.append(v_); wkp.append(w_); shp.append(sh_); plp.append(pl_)
        mkp.append(mk); mvp.append(mv)
        moba_s = functools.partial(moba_sample, cache_k=cache_k, cache_v=cache_v, layer=l,
                                   page_table=page_table, slopes=slopes)
        xs, k_, v_, w_, sh_, pl_ = layer_forward(xs, pos_s, prm, moba_s, state_shift[l], state_wkv[l],
                                                 state_pool[l], cache_mem_k[l], cache_mem_v[l])
        ksm.append(k_); vsm.append(v_); wks.append(w_); shs.append(sh_); pls.append(pl_)
    y_prompt = rmsnorm(xp, norm_final)
    y_sample = rmsnorm(xs, norm_final)
    return (y_prompt, y_sample,
            jnp.stack(kp), jnp.stack(vp), jnp.stack(wkp), jnp.stack(shp), jnp.stack(plp),
            jnp.stack(mkp), jnp.stack(mvp),
            jnp.stack(ksm), jnp.stack(vsm), jnp.stack(wks), jnp.stack(shs), jnp.stack(pls))
```

```python
import functools

import jax
import jax.numpy as jnp
from jax import lax
from jax.experimental import pallas as pl
from jax.experimental.pallas import tpu as pltpu

F32 = jnp.float32
BF16 = jnp.bfloat16
HI = lax.Precision.HIGHEST

RMS_EPS = 1e-6
LANES = 128
SUBLANES = 8
VMEM_LIMIT = 48 * 1024 * 1024

PAGE_SIZE = 128
A_HEAD_DIM = 128
A_BLOCK = 256
A_TOPK = 3
A_QCHUNK = 128
R_HEAD_DIM = 64
R_CHUNK = 64
R_GN_EPS = 64e-5
POOL_WINDOWS = (2, 4, 8, 16)
POOL_MAX = 16
M_HEAD_DIM = 128
NEG = -1e30

NT_DIMS = (((1,), (1,)), ((), ()))


def _cparams(sem):
    return pltpu.CompilerParams(dimension_semantics=sem, vmem_limit_bytes=VMEM_LIMIT)


def _row_tile(n, pref):
    return pref if n % pref == 0 else n


def _rms(x, g):
    ms = jnp.mean(x * x, axis=-1, keepdims=True)
    return x * lax.rsqrt(ms + RMS_EPS) * g


def _rms_kernel(x_ref, g_ref, o_ref):
    o_ref[...] = _rms(x_ref[...], g_ref[...]).astype(o_ref.dtype)


def rmsnorm(x, g, out_dtype):
    n, d = x.shape
    tm = _row_tile(n, 512)
    return pl.pallas_call(
        _rms_kernel,
        out_shape=jax.ShapeDtypeStruct((n, d), out_dtype),
        grid=(n // tm,),
        in_specs=[pl.BlockSpec((tm, d), lambda i: (i, 0)), pl.BlockSpec((1, d), lambda i: (0, 0))],
        out_specs=pl.BlockSpec((tm, d), lambda i: (i, 0)),
        compiler_params=_cparams(("parallel",)),
        name="rmsnorm",
    )(x, g.reshape(1, d))


def _matmul_kernel(a_ref, w_ref, o_ref):
    o_ref[...] = jnp.dot(a_ref[...], w_ref[...], preferred_element_type=F32)


def matmul(a, w, tn):
    n, k = a.shape
    m = w.shape[1]
    tm = _row_tile(n, 512)
    return pl.pallas_call(
        _matmul_kernel,
        out_shape=jax.ShapeDtypeStruct((n, m), F32),
        grid=(n // tm, m // tn),
        in_specs=[pl.BlockSpec((tm, k), lambda i, j: (i, 0)), pl.BlockSpec((k, tn), lambda i, j: (0, j))],
        out_specs=pl.BlockSpec((tm, tn), lambda i, j: (i, j)),
        compiler_params=_cparams(("parallel", "arbitrary")),
        name="matmul",
    )(a, w)


def _ffn_kernel(x_ref, g_ref, wg_ref, wu_ref, wd_ref, o_ref, h_scr, acc_scr):
    j = pl.program_id(1)

    @pl.when(j == 0)
    def _():
        h_scr[...] = _rms(x_ref[...], g_ref[...]).astype(BF16)
        acc_scr[...] = jnp.zeros_like(acc_scr)

    h = h_scr[...]
    gate = jnp.dot(h, wg_ref[...], preferred_element_type=F32)
    up = jnp.dot(h, wu_ref[...], preferred_element_type=F32)
    act = (gate * jax.nn.sigmoid(gate) * up).astype(BF16)
    acc_scr[...] += jnp.dot(act, wd_ref[...], preferred_element_type=F32)

    @pl.when(j == pl.num_programs(1) - 1)
    def _():
        o_ref[...] = x_ref[...] + 0.5 * acc_scr[...]


def ffn_half_step(x, g, wg, wu, wd, tf=512):
    n, d = x.shape
    f = wg.shape[1]
    tm = _row_tile(n, 512)
    return pl.pallas_call(
        _ffn_kernel,
        out_shape=jax.ShapeDtypeStruct((n, d), F32),
        grid=(n // tm, f // tf),
        in_specs=[
            pl.BlockSpec((tm, d), lambda i, j: (i, 0)),
            pl.BlockSpec((1, d), lambda i, j: (0, 0)),
            pl.BlockSpec((d, tf), lambda i, j: (0, j)),
            pl.BlockSpec((d, tf), lambda i, j: (0, j)),
            pl.BlockSpec((tf, d), lambda i, j: (j, 0)),
        ],
        out_specs=pl.BlockSpec((tm, d), lambda i, j: (i, 0)),
        scratch_shapes=[pltpu.VMEM((tm, d), BF16), pltpu.VMEM((tm, d), F32)],
        compiler_params=_cparams(("parallel", "arbitrary")),
        name="ffn_half_step",
    )(x, g.reshape(1, d), wg, wu, wd)


def _moba_prompt_kernel(slope_ref, q_ref, k_ref, v_ref, o_ref, kb_scr, vb_scr, km_scr, *, nb):
    h = pl.program_id(1)
    c = pl.program_id(2)
    qc = A_QCHUNK

    @pl.when(c == 0)
    def _():
        k = k_ref[0]
        kb_scr[...] = k.astype(BF16)
        vb_scr[...] = v_ref[0].astype(BF16)
        km_scr[...] = jnp.zeros_like(km_scr)
        for n in range(nb):
            km_scr[n:n + 1, :] = jnp.sum(k[n * A_BLOCK:(n + 1) * A_BLOCK], axis=0, keepdims=True) * (1.0 / A_BLOCK)

    q = q_ref[0]
    own = (c * qc) // A_BLOCK
    slope = slope_ref[h]
    scale = A_HEAD_DIM ** -0.5

    g = lax.dot_general(q, km_scr[...], NT_DIMS, precision=HI, preferred_element_type=F32)
    lane = lax.broadcasted_iota(jnp.int32, (qc, LANES), 1)
    cand = lane < own
    gm = jnp.where(cand, g, -jnp.inf)
    sel = []
    for n in range(nb):
        gn = gm[:, n:n + 1]
        beats = (gm > gn) | ((gm == gn) & (lane < n))
        cnt = jnp.sum(jnp.where(beats & cand, 1.0, 0.0), axis=-1, keepdims=True)
        sel.append(jnp.where(cnt < A_TOPK, 1.0, 0.0) * jnp.where(n < own, 1.0, 0.0))

    s = lax.dot_general(q.astype(BF16), kb_scr[...], NT_DIMS, preferred_element_type=F32)
    pos_q = c * qc + lax.broadcasted_iota(jnp.int32, (qc, A_BLOCK), 0)
    off_k = lax.broadcasted_iota(jnp.int32, (qc, A_BLOCK), 1)
    blocks = []
    for n in range(nb):
        pos_k = n * A_BLOCK + off_k
        sn = s[:, n * A_BLOCK:(n + 1) * A_BLOCK] * scale - slope * (pos_q - pos_k).astype(F32)
        causal = jnp.where(pos_k <= pos_q, jnp.where(own == n, 1.0, 0.0), 0.0)
        allow = (jnp.broadcast_to(sel[n], (qc, A_BLOCK)) + causal) > 0.5
        blocks.append(jnp.where(allow, sn, NEG))
    m = blocks[0].max(axis=-1, keepdims=True)
    for n in range(1, nb):
        m = jnp.maximum(m, blocks[n].max(axis=-1, keepdims=True))
    l = jnp.zeros((qc, 1), F32)
    o = jnp.zeros((qc, A_HEAD_DIM), F32)
    for n in range(nb):
        p = jnp.exp(blocks[n] - m)
        l = l + jnp.sum(p, axis=-1, keepdims=True)
        o = o + jnp.dot(p.astype(BF16), vb_scr[n * A_BLOCK:(n + 1) * A_BLOCK, :], preferred_element_type=F32)
    o_ref[0] = o / l


def moba_prompt(q, k, v, slopes):
    b, t, w = q.shape
    nh = w // A_HEAD_DIM
    nb = t // A_BLOCK
    assert t % A_BLOCK == 0 and nb >= A_TOPK and nb <= LANES
    nq = t // A_QCHUNK
    return pl.pallas_call(
        functools.partial(_moba_prompt_kernel, nb=nb),
        out_shape=jax.ShapeDtypeStruct((b, t, w), F32),
        grid=(b, nh, nq),
        in_specs=[
            pl.BlockSpec(memory_space=pltpu.SMEM),
            pl.BlockSpec((1, A_QCHUNK, A_HEAD_DIM), lambda i, h, c: (i, c, h)),
            pl.BlockSpec((1, t, A_HEAD_DIM), lambda i, h, c: (i, 0, h)),
            pl.BlockSpec((1, t, A_HEAD_DIM), lambda i, h, c: (i, 0, h)),
        ],
        out_specs=pl.BlockSpec((1, A_QCHUNK, A_HEAD_DIM), lambda i, h, c: (i, c, h)),
        scratch_shapes=[
            pltpu.VMEM((t, A_HEAD_DIM), BF16),
            pltpu.VMEM((t, A_HEAD_DIM), BF16),
            pltpu.VMEM((LANES, A_HEAD_DIM), F32),
        ],
        compiler_params=_cparams(("parallel", "parallel", "arbitrary")),
        name="moba_prompt",
    )(slopes, q, k, v)


def _moba_sample_kernel(pt_ref, slope_ref, q_ref, kn_ref, vn_ref, ka_ref, kb_ref, va_ref, vb_ref, o_ref,
                        g_scr, m_scr, l_scr, o_scr, *, nh, nblk, tn, past):
    n = pl.program_id(1)
    scale = A_HEAD_DIM ** -0.5
    rows = q_ref.shape[1]
    t_q = lax.broadcasted_iota(jnp.int32, (rows, A_BLOCK), 0)
    off_k = lax.broadcasted_iota(jnp.int32, (rows, A_BLOCK), 1)
    dist = (past + t_q - (n * A_BLOCK + off_k)).astype(F32)

    for h in range(nh):
        hs = slice(h * A_HEAD_DIM, (h + 1) * A_HEAD_DIM)
        q = q_ref[0, :, hs]
        kblk = jnp.concatenate([ka_ref[:, hs], kb_ref[:, hs]], axis=0)
        vblk = jnp.concatenate([va_ref[:, hs], vb_ref[:, hs]], axis=0)
        kmean = jnp.sum(kblk, axis=0, keepdims=True) * (1.0 / A_BLOCK)
        g = jnp.sum(q * kmean, axis=-1, keepdims=True)
        s = lax.dot_general(q.astype(BF16), kblk.astype(BF16), NT_DIMS, preferred_element_type=F32)
        s = s * scale - slope_ref[h] * dist
        m = s.max(axis=-1, keepdims=True)
        p = jnp.exp(s - m)
        l = jnp.sum(p, axis=-1, keepdims=True)
        o = jnp.dot(p.astype(BF16), vblk.astype(BF16), preferred_element_type=F32)
        g_scr[n, h] = jnp.broadcast_to(g, (rows, LANES))
        m_scr[n, h] = jnp.broadcast_to(m, (rows, LANES))
        l_scr[n, h] = jnp.broadcast_to(l, (rows, LANES))
        o_scr[n, h] = o

    @pl.when(n == nblk - 1)
    def _():
        t_o = lax.broadcasted_iota(jnp.int32, (rows, LANES), 0)
        j_o = lax.broadcasted_iota(jnp.int32, (rows, LANES), 1)
        own_allow = (j_o <= t_o) & (j_o < tn)
        for h in range(nh):
            hs = slice(h * A_HEAD_DIM, (h + 1) * A_HEAD_DIM)
            q = q_ref[0, :, hs]
            s_own = lax.dot_general(q.astype(BF16), kn_ref[0, :, hs].astype(BF16), NT_DIMS,
                                    preferred_element_type=F32)
            s_own = s_own * scale - slope_ref[h] * (t_o - j_o).astype(F32)
            s_own = jnp.where(own_allow, s_own, NEG)
            gs = [g_scr[i, h] for i in range(nblk)]
            sels = []
            for i in range(nblk):
                cnt = jnp.zeros((rows, LANES), F32)
                for jj in range(nblk):
                    if jj == i:
                        continue
                    beats = (gs[jj] >= gs[i]) if jj < i else (gs[jj] > gs[i])
                    cnt = cnt + jnp.where(beats, 1.0, 0.0)
                sels.append(cnt < A_TOPK)
            mx = s_own.max(axis=-1, keepdims=True)
            mx = jnp.broadcast_to(mx, (rows, LANES))
            for i in range(nblk):
                mx = jnp.maximum(mx, jnp.where(sels[i], m_scr[i, h], NEG))
            p_own = jnp.exp(s_own - mx)
            den = jnp.broadcast_to(jnp.sum(p_own, axis=-1, keepdims=True), (rows, LANES))
            num = jnp.dot(p_own.astype(BF16), vn_ref[0, :, hs].astype(BF16), preferred_element_type=F32)
            for i in range(nblk):
                wgt = jnp.where(sels[i], jnp.exp(m_scr[i, h] - mx), 0.0)
                den = den + wgt * l_scr[i, h]
                num = num + wgt * o_scr[i, h]
            o_ref[0, :, hs] = num / den


def moba_sample(q, k_new, v_new, cache_k, cache_v, layer, page_table, slopes):
    db, tn, w = q.shape
    nh = w // A_HEAD_DIM
    n_pages = page_table.shape[1]
    past = n_pages * PAGE_SIZE
    ppb = A_BLOCK // PAGE_SIZE
    assert past % A_BLOCK == 0 and ppb == 2 and tn <= SUBLANES
    nblk = past // A_BLOCK
    assert nblk >= A_TOPK
    rows = SUBLANES
    qp = jnp.pad(q, ((0, 0), (0, rows - tn), (0, 0)))
    knp = jnp.pad(k_new, ((0, 0), (0, LANES - tn), (0, 0)))
    vnp = jnp.pad(v_new, ((0, 0), (0, LANES - tn), (0, 0)))

    def page_spec(j):
        return pl.BlockSpec((None, None, PAGE_SIZE, w), lambda i, n, pt: (layer, pt[i, ppb * n + j], 0, 0))

    out = pl.pallas_call(
        functools.partial(_moba_sample_kernel, nh=nh, nblk=nblk, tn=tn, past=past),
        out_shape=jax.ShapeDtypeStruct((db, rows, w), F32),
        grid_spec=pltpu.PrefetchScalarGridSpec(
            num_scalar_prefetch=1,
            grid=(db, nblk),
            in_specs=[
                pl.BlockSpec(memory_space=pltpu.SMEM),
                pl.BlockSpec((1, rows, w), lambda i, n, pt: (i, 0, 0)),
                pl.BlockSpec((1, LANES, w), lambda i, n, pt: (i, 0, 0)),
                pl.BlockSpec((1, LANES, w), lambda i, n, pt: (i, 0, 0)),
                page_spec(0), page_spec(1), page_spec(0), page_spec(1),
            ],
            out_specs=pl.BlockSpec((1, rows, w), lambda i, n, pt: (i, 0, 0)),
            scratch_shapes=[
                pltpu.VMEM((nblk, nh, rows, LANES), F32),
                pltpu.VMEM((nblk, nh, rows, LANES), F32),
                pltpu.VMEM((nblk, nh, rows, LANES), F32),
                pltpu.VMEM((nblk, nh, rows, A_HEAD_DIM), F32),
            ],
        ),
        compiler_params=_cparams(("parallel", "arbitrary")),
        name="moba_sample",
    )(page_table, slopes, qp, knp, vnp, cache_k, cache_k, cache_v, cache_v)
    return out[:, :tn]


def _head_sum_matrix():
    i = lax.broadcasted_iota(jnp.int32, (LANES, LANES), 0)
    j = lax.broadcasted_iota(jnp.int32, (LANES, LANES), 1)
    return jnp.where(lax.shift_right_logical(i, 6) == lax.shift_right_logical(j, 6), 1.0, 0.0).astype(F32)


def _rwkv_prep_kernel(u_ref, pv_ref, s0_ref, mu_ref, w0_ref, w2_ref, a0_ref, a2_ref, g2_ref, kk_ref, ka_ref,
                      r_o, lw_o, k_o, v_o, kk_o, a_o, g_o, *, rw):
    t = pl.program_id(1)
    u = u_ref[0]
    tt = u.shape[0]
    prev_row = jnp.where(t == 0, s0_ref[0], pv_ref[0, SUBLANES - 1:SUBLANES, :])
    row = lax.broadcasted_iota(jnp.int32, u.shape, 0)
    prev = jnp.where(row == 0, prev_row, pltpu.roll(u, 1, axis=0))
    xs = u + (prev - u) * mu_ref[...]
    r = xs[:, 0:rw]
    k = xs[:, rw:2 * rw]
    v = xs[:, 2 * rw:3 * rw]
    lora_wa = xs[:, 3 * rw:3 * rw + LANES]
    gd = xs[:, 3 * rw + LANES:3 * rw + 2 * LANES]
    z = w0_ref[...] + jnp.dot(jnp.tanh(lora_wa), w2_ref[...], precision=HI, preferred_element_type=F32)
    nz = -z
    softplus = jnp.maximum(nz, 0.0) + jnp.log1p(jnp.exp(-jnp.abs(nz)))
    w_log = -softplus - 0.5
    lw_o[0] = -jnp.exp(w_log)
    a = jax.nn.sigmoid(a0_ref[...] + jnp.dot(lora_wa, a2_ref[...], precision=HI, preferred_element_type=F32))
    g_o[0] = jnp.dot(jax.nn.sigmoid(gd), g2_ref[...], precision=HI, preferred_element_type=F32)
    kk = k * kk_ref[...]
    seg = _head_sum_matrix()
    for p in range(rw // LANES):
        ps = slice(p * LANES, (p + 1) * LANES)
        kkp = kk[:, ps]
        ss = jnp.dot(kkp * kkp, seg, precision=HI, preferred_element_type=F32)
        kk_o[0, :, ps] = kkp / jnp.maximum(jnp.sqrt(ss), 1e-12)
    r_o[0] = r
    k_o[0] = k * (1.0 + (a - 1.0) * ka_ref[...])
    v_o[0] = v
    a_o[0] = a


def rwkv_prep(ur, shift0, prm):
    b, t, cols = ur.shape
    rw = prm['rw_w0'].shape[-1]
    tt = _row_tile(t, 256)
    nlora = prm['rw_w2'].shape[0]
    assert 2 * nlora == LANES and prm['rw_g2'].shape[0] == LANES and cols == 3 * rw + 2 * LANES
    w2p = jnp.concatenate([prm['rw_w2'], jnp.zeros_like(prm['rw_w2'])], axis=0)
    a2p = jnp.concatenate([jnp.zeros_like(prm['rw_a2']), prm['rw_a2']], axis=0)
    row = lambda z: z.reshape(1, -1)
    full = lambda shape: pl.BlockSpec(shape, lambda i, j: (0,) * len(shape))
    outs = pl.pallas_call(
        functools.partial(_rwkv_prep_kernel, rw=rw),
        out_shape=[jax.ShapeDtypeStruct((b, t, rw), F32)] * 7,
        grid=(b, t // tt),
        in_specs=[
            pl.BlockSpec((1, tt, cols), lambda i, j: (i, j, 0)),
            pl.BlockSpec((1, SUBLANES, cols), lambda i, j: (i, jnp.maximum(j * (tt // SUBLANES) - 1, 0), 0)),
            pl.BlockSpec((1, 1, cols), lambda i, j: (i, 0, 0)),
            full((1, cols)), full((1, rw)), full((LANES, rw)), full((1, rw)), full((LANES, rw)),
            full((LANES, rw)), full((1, rw)), full((1, rw)),
        ],
        out_specs=[pl.BlockSpec((1, tt, rw), lambda i, j: (i, j, 0))] * 7,
        compiler_params=_cparams(("parallel", "parallel")),
        name="rwkv_prep",
    )(ur, ur, shift0.reshape(b, 1, cols), row(prm['rw_mu']), row(prm['rw_w0']), w2p, row(prm['rw_a0']), a2p,
      prm['rw_g2'], row(prm['rw_kk']), row(prm['rw_ka']))
    return outs


def _pair_masks():
    i = lax.broadcasted_iota(jnp.int32, (LANES, LANES), 0)
    j = lax.broadcasted_iota(jnp.int32, (LANES, LANES), 1)
    same = lax.shift_right_logical(i, 6) == lax.shift_right_logical(j, 6)
    return i, j, same


def _rwkv_chunk_kernel(r_ref, lw_ref, k_ref, v_ref, kk_ref, a_ref,
                       ks_o, rs_o, nt_o, arb_o, aakv_o, arkv_o, bh_o, vtk_o, bc_o):
    c = R_CHUNK
    r, lw, k, v, kk, a = r_ref[0], lw_ref[0], k_ref[0], v_ref[0], kk_ref[0], a_ref[0]
    ti = lax.broadcasted_iota(jnp.int32, (c, c), 0)
    tj = lax.broadcasted_iota(jnp.int32, (c, c), 1)
    cum = jnp.dot(jnp.where(tj <= ti, 1.0, 0.0).astype(F32), lw, precision=HI, preferred_element_type=F32)
    cl = cum[c - 1:c, :]
    beta = kk * a
    kap_t = kk * jnp.exp(cum - lw)
    r_t = r * jnp.exp(cum)
    e_inv = jnp.exp(-cum)
    b_t = beta * e_inv
    k_t = k * e_inv
    e_end = jnp.exp(cl - cum)
    b_h = beta * e_end
    k_h = k * e_end

    lo = lax.broadcasted_iota(jnp.int32, (c, LANES), 1) < R_HEAD_DIM
    stack = lambda x: jnp.concatenate([jnp.where(lo, x, 0.0), jnp.where(lo, 0.0, x)], axis=0)
    dup = lambda x: jnp.concatenate([x, x], axis=0)
    mm = lambda x, y: jnp.dot(x, y, precision=HI, preferred_element_type=F32)
    mm_nt = lambda x, y: lax.dot_general(x, y, NT_DIMS, precision=HI, preferred_element_type=F32)

    i, j, same = _pair_masks()
    strict = same & (j < i)
    incl = same & (j <= i)
    kap_s, r_s, v_s = stack(kap_t), stack(r_t), stack(v)
    b_d, k_d = dup(b_t), dup(k_t)
    a_ab = jnp.where(strict, mm_nt(kap_s, b_d), 0.0)
    a_ak = jnp.where(strict, mm_nt(kap_s, k_d), 0.0)
    a_rb = jnp.where(incl, mm_nt(r_s, b_d), 0.0)
    a_rk = jnp.where(incl, mm_nt(r_s, k_d), 0.0)

    x = jnp.where(i == j, 1.0, 0.0).astype(F32)
    s = 1
    while s < c:
        low = (lax.shift_right_logical(i, s.bit_length()) == lax.shift_right_logical(j, s.bit_length())) \
            & ((i & (2 * s - 1)) >= s) & ((j & (2 * s - 1)) < s)
        x = x - mm(mm(x, jnp.where(low, a_ab, 0.0)), x)
        s *= 2

    ks_o[0, 0, 0] = kap_s
    rs_o[0, 0, 0] = r_s
    nt_o[0, 0, 0] = -x
    arb_o[0, 0, 0] = a_rb
    aakv_o[0, 0, 0] = mm(a_ak, v_s)
    arkv_o[0, 0, 0] = mm(a_rk, v_s)
    bh_o[0, 0, 0] = stack(b_h)
    vtk_o[0, 0, 0] = mm(v_s.T, stack(k_h))
    bc_o[0, 0, 0] = jnp.broadcast_to(jnp.exp(cl), (SUBLANES, LANES))


def _rwkv_scan_kernel(ks, rs, nt, arb, aakv, arkv, bh, vtk, bc, s0_ref, y_o, st_o, s_scr):
    cidx = pl.program_id(2)

    @pl.when(cidx == 0)
    def _():
        s_scr[...] = s0_ref[0, 0]

    mm = lambda x, y: jnp.dot(x, y, precision=HI, preferred_element_type=F32)
    mm_nt = lambda x, y: lax.dot_general(x, y, NT_DIMS, precision=HI, preferred_element_type=F32)
    s = s_scr[...]
    u = mm(nt[0, 0, 0], mm_nt(ks[0, 0, 0], s) + aakv[0, 0, 0])
    ys = mm_nt(rs[0, 0, 0], s) + mm(arb[0, 0, 0], u) + arkv[0, 0, 0]
    s_new = s * bc[0, 0, 0, 0:1, :] + mm(u.T, bh[0, 0, 0]) + vtk[0, 0, 0]
    s_scr[...] = s_new
    y_o[0] = ys[:R_CHUNK] + ys[R_CHUNK:]
    st_o[0, 0] = s_new


def rwkv_scan(r, lw, k, v, kk, a, wkv0):
    b, t, rw = r.shape
    npair = rw // LANES
    nc = t // R_CHUNK
    assert t % R_CHUNK == 0 and 2 * R_HEAD_DIM == LANES
    in_spec = pl.BlockSpec((1, R_CHUNK, LANES), lambda i, p, c: (i, c, p))
    mat = lambda rows: pl.BlockSpec((1, 1, 1, rows, LANES), lambda i, p, c: (i, p, c, 0, 0))
    shp = lambda rows: jax.ShapeDtypeStruct((b, npair, nc, rows, LANES), F32)
    factors = pl.pallas_call(
        _rwkv_chunk_kernel,
        out_shape=[shp(LANES)] * 8 + [shp(SUBLANES)],
        grid=(b, npair, nc),
        in_specs=[in_spec] * 6,
        out_specs=[mat(LANES)] * 8 + [mat(SUBLANES)],
        compiler_params=_cparams(("parallel", "parallel", "parallel")),
        name="rwkv_chunk",
    )(r, lw, k, v, kk, a)

    w4 = wkv0.reshape(b, npair, 2, R_HEAD_DIM, R_HEAD_DIM)
    zero = jnp.zeros_like(w4[:, :, 0])
    s0 = jnp.concatenate([jnp.concatenate([w4[:, :, 0], zero], axis=-1),
                          jnp.concatenate([zero, w4[:, :, 1]], axis=-1)], axis=-2)
    st_spec = pl.BlockSpec((1, 1, LANES, LANES), lambda i, p, c: (i, p, 0, 0))
    y, st = pl.pallas_call(
        _rwkv_scan_kernel,
        out_shape=[jax.ShapeDtypeStruct((b, t, rw), F32), jax.ShapeDtypeStruct((b, npair, LANES, LANES), F32)],
        grid=(b, npair, nc),
        in_specs=[mat(LANES)] * 8 + [mat(SUBLANES), st_spec],
        out_specs=[pl.BlockSpec((1, R_CHUNK, LANES), lambda i, p, c: (i, c, p)), st_spec],
        scratch_shapes=[pltpu.VMEM((LANES, LANES), F32)],
        compiler_params=_cparams(("parallel", "parallel", "arbitrary")),
        name="rwkv_scan",
    )(*factors, s0)
    wkv = jnp.stack([st[:, :, :R_HEAD_DIM, :R_HEAD_DIM], st[:, :, R_HEAD_DIM:, R_HEAD_DIM:]], axis=2)
    return y, wkv.reshape(wkv0.shape)


def _rwkv_post_kernel(y_ref, r_ref, k_ref, v_ref, g_ref, rk_ref, lw_ref, lb_ref, o_ref):
    seg = _head_sum_matrix()
    inv_n = 1.0 / R_HEAD_DIM
    for p in range(y_ref.shape[2] // LANES):
        ps = slice(p * LANES, (p + 1) * LANES)
        y = y_ref[0, :, ps]
        hsum = lambda z: jnp.dot(z, seg, precision=HI, preferred_element_type=F32)
        mean = hsum(y) * inv_n
        d = y - mean
        var = hsum(d * d) * inv_n
        yn = d * lax.rsqrt(var + R_GN_EPS) * lw_ref[:, ps] + lb_ref[:, ps]
        bonus = hsum(r_ref[0, :, ps] * k_ref[0, :, ps] * rk_ref[:, ps]) * v_ref[0, :, ps]
        o_ref[0, :, ps] = (yn + bonus) * g_ref[0, :, ps]


def rwkv_post(y, r, k, v, g, prm):
    b, t, rw = y.shape
    tt = _row_tile(t, 256)
    act = pl.BlockSpec((1, tt, rw), lambda i, j: (i, j, 0))
    par = pl.BlockSpec((1, rw), lambda i, j: (0, 0))
    return pl.pallas_call(
        _rwkv_post_kernel,
        out_shape=jax.ShapeDtypeStruct((b, t, rw), F32),
        grid=(b, t // tt),
        in_specs=[act] * 5 + [par] * 3,
        out_specs=act,
        compiler_params=_cparams(("parallel", "parallel")),
        name="rwkv_post",
    )(y, r, k, v, g, prm['rw_rk'].reshape(1, rw), prm['rw_lnx_w'].reshape(1, rw), prm['rw_lnx_b'].reshape(1, rw))


def rwkv_mix(ur, shift0, wkv0, prm):
    b, t, _ = ur.shape
    tp = -(-t // SUBLANES) * SUBLANES
    urp = jnp.pad(ur, ((0, 0), (0, tp - t), (0, 0)))
    r, lw, k, v, kk, a, g = rwkv_prep(urp, shift0, prm)
    tc = -(-t // R_CHUNK) * R_CHUNK
    fit = lambda z: jnp.pad(z[:, :t], ((0, 0), (0, tc - t), (0, 0)))
    r, lw, k, v, kk, a, g = (fit(z) for z in (r, lw, k, v, kk, a, g))
    y, wkv = rwkv_scan(r, lw, k, v, kk, a, wkv0)
    out = rwkv_post(y, r, k, v, g, prm)
    return out[:, :t], wkv, ur[:, -1]


def _pool_kernel(u_ref, pv_ref, p0_ref, w_ref, sc_ref, o_ref, *, pos0):
    t = pl.program_id(1)
    cur = u_ref[0]
    tt = cur.shape[0]
    ext = jnp.concatenate([jnp.where(t == 0, p0_ref[0], pv_ref[0]), cur], axis=0)
    sums = []
    s = ext
    w = 1
    while w < POOL_MAX:
        s = s + pltpu.roll(s, w, axis=0)
        w *= 2
        sums.append(s)
    pos = pos0 + t * tt + lax.broadcasted_iota(jnp.int32, (tt, 1), 0)
    for gi, w in enumerate(POOL_WINDOWS):
        gs = slice(gi * LANES, (gi + 1) * LANES)
        win = sums[w.bit_length() - 2][POOL_MAX:, gs]
        cnt = jnp.minimum(pos + 1, w).astype(F32)
        m = win / cnt - cur[:, gs]
        z = jnp.dot(m.astype(BF16), w_ref[gi], preferred_element_type=F32)
        o_ref[0, :, gs] = z * sc_ref[:, gs]


def pool_mix(uc, pool0, pos0, pool_w, pool_scale):
    b, t, width = uc.shape
    assert width == len(POOL_WINDOWS) * LANES and pool_w.shape[1] == LANES
    tp = -(-t // SUBLANES) * SUBLANES
    ucp = jnp.pad(uc, ((0, 0), (0, tp - t), (0, 0)))
    tt = _row_tile(tp, 256)
    assert tt % POOL_MAX == 0 or tp == tt
    p0 = jnp.pad(pool0, ((0, 0), (1, 0), (0, 0)))
    nprev = tt // POOL_MAX if tt % POOL_MAX == 0 else 0
    ext = jnp.concatenate([pool0, uc], axis=1)
    if nprev:
        pv, pv_spec = ucp, pl.BlockSpec((1, POOL_MAX, width), lambda i, j: (i, jnp.maximum(j * nprev - 1, 0), 0))
    else:
        pv, pv_spec = p0, pl.BlockSpec((1, POOL_MAX, width), lambda i, j: (i, 0, 0))
    z = pl.pallas_call(
        functools.partial(_pool_kernel, pos0=pos0),
        out_shape=jax.ShapeDtypeStruct((b, tp, width), F32),
        grid=(b, tp // tt),
        in_specs=[
            pl.BlockSpec((1, tt, width), lambda i, j: (i, j, 0)),
            pv_spec,
            pl.BlockSpec((1, POOL_MAX, width), lambda i, j: (i, 0, 0)),
            pl.BlockSpec(pool_w.shape, lambda i, j: (0, 0, 0)),
            pl.BlockSpec((1, width), lambda i, j: (0, 0)),
        ],
        out_specs=pl.BlockSpec((1, tt, width), lambda i, j: (i, j, 0)),
        compiler_params=_cparams(("parallel", "parallel")),
        name="pool_mix",
    )(ucp, pv, p0, pool_w.astype(BF16), pool_scale.reshape(1, width))
    return z[:, :t], ext[:, -(POOL_MAX - 1):]


def _mix_out_kernel(x_ref, ya_ref, yr_ref, yc_ref, wa_ref, wr_ref, wc_ref, o_ref):
    acc = jnp.dot(ya_ref[...].astype(BF16), wa_ref[...], preferred_element_type=F32)
    acc += jnp.dot(yr_ref[...].astype(BF16), wr_ref[...], preferred_element_type=F32)
    acc += jnp.dot(yc_ref[...].astype(BF16), wc_ref[...], preferred_element_type=F32)
    o_ref[...] = x_ref[...] + acc


def mix_out(x, ya, yr, yc, w_out):
    n, d = x.shape
    wa_, wr_, wc_ = ya.shape[1], yr.shape[1], yc.shape[1]
    tm = _row_tile(n, 512)
    tn = 1024
    act = lambda wd: pl.BlockSpec((tm, wd), lambda i, j: (i, 0))
    wsp = lambda wd: pl.BlockSpec((wd, tn), lambda i, j: (0, j))
    return pl.pallas_call(
        _mix_out_kernel,
        out_shape=jax.ShapeDtypeStruct((n, d), F32),
        grid=(n // tm, d // tn),
        in_specs=[pl.BlockSpec((tm, tn), lambda i, j: (i, j)), act(wa_), act(wr_), act(wc_),
                  wsp(wa_), wsp(wr_), wsp(wc_)],
        out_specs=pl.BlockSpec((tm, tn), lambda i, j: (i, j)),
        compiler_params=_cparams(("parallel", "arbitrary")),
        name="mix_out",
    )(x, ya, yr, yc, w_out[:wa_], w_out[wa_:wa_ + wr_], w_out[wa_ + wr_:])


def _cross_kernel(x_ref, g_ref, wq_ref, mk_ref, mv_ref, wo_ref, o_ref, *, nh):
    x = x_ref[0]
    h = _rms(x, g_ref[...]).astype(BF16)
    q = jnp.dot(h, wq_ref[...], preferred_element_type=F32)
    scale = M_HEAD_DIM ** -0.5
    outs = []
    for hh in range(nh):
        hs = slice(hh * M_HEAD_DIM, (hh + 1) * M_HEAD_DIM)
        s = lax.dot_general(q[:, hs].astype(BF16), mk_ref[0, :, hs].astype(BF16), NT_DIMS,
                            preferred_element_type=F32) * scale
        p = jnp.exp(s - s.max(axis=-1, keepdims=True))
        l = jnp.sum(p, axis=-1, keepdims=True)
        outs.append(jnp.dot(p.astype(BF16), mv_ref[0, :, hs].astype(BF16), preferred_element_type=F32) / l)
    o = jnp.concatenate(outs, axis=-1).astype(BF16)
    o_ref[0] = x + jnp.dot(o, wo_ref[...], preferred_element_type=F32)


def cross_attend(x, g, wq, mk, mv, wo):
    b, t0, d = x.shape
    t = -(-t0 // SUBLANES) * SUBLANES
    x = jnp.pad(x, ((0, 0), (0, t - t0), (0, 0)))
    mw = wq.shape[1]
    nm = mk.shape[1]
    tm = _row_tile(t, 512)
    out = pl.pallas_call(
        functools.partial(_cross_kernel, nh=mw // M_HEAD_DIM),
        out_shape=jax.ShapeDtypeStruct((b, t, d), F32),
        grid=(b, t // tm),
        in_specs=[
            pl.BlockSpec((1, tm, d), lambda i, j: (i, j, 0)),
            pl.BlockSpec((1, d), lambda i, j: (0, 0)),
            pl.BlockSpec((d, mw), lambda i, j: (0, 0)),
            pl.BlockSpec((1, nm, mw), lambda i, j: (i, 0, 0)),
            pl.BlockSpec((1, nm, mw), lambda i, j: (i, 0, 0)),
            pl.BlockSpec((mw, d), lambda i, j: (0, 0)),
        ],
        out_specs=pl.BlockSpec((1, tm, d), lambda i, j: (i, j, 0)),
        compiler_params=_cparams(("parallel", "parallel")),
        name="cross_attend",
    )(x, g.reshape(1, d), wq, mk, mv, wo)
    return out[:, :t0]


def _layer(x, prm, moba_fn, pos0, shift0, wkv0, pool0, mem_k, mem_v):
    b, t, d = x.shape
    n = b * t
    x2 = ffn_half_step(x.reshape(n, d), prm['norm_ffn1'], prm['ffn1_gate'], prm['ffn1_up'], prm['ffn1_down'])
    h = rmsnorm(x2, prm['norm_mix'], BF16)
    aw, rc, cw = prm['a_width'], prm['r_cols'], prm['c_width']
    w_in = prm['w_in']
    qa = matmul(h, w_in[:, :aw], 512).reshape(b, t, aw)
    ka = matmul(h, w_in[:, aw:2 * aw], 512).reshape(b, t, aw)
    va = matmul(h, w_in[:, 2 * aw:3 * aw], 512).reshape(b, t, aw)
    ur = matmul(h, w_in[:, 3 * aw:3 * aw + rc], rc // 2).reshape(b, t, rc)
    uc = matmul(h, w_in[:, 3 * aw + rc:], cw).reshape(b, t, cw)
    ya = moba_fn(qa, ka, va)
    yr, wkv, shift = rwkv_mix(ur, shift0, wkv0, prm)
    yc, pool_buf = pool_mix(uc, pool0, pos0, prm['pool_w'], prm['pool_scale'])
    x3 = mix_out(x2, ya.reshape(n, aw), yr.reshape(n, -1), yc.reshape(n, cw), prm['w_out'])
    x4 = cross_attend(x3.reshape(b, t, d), prm['norm_cross'], prm['mem_wq'], mem_k, mem_v, prm['mem_wo'])
    x5 = ffn_half_step(x4.reshape(n, d), prm['norm_ffn2'], prm['ffn2_gate'], prm['ffn2_up'], prm['ffn2_down'])
    return x5.reshape(b, t, d), ka, va, wkv, shift, pool_buf


def kernel(x_prompt, x_sample, cache_k, cache_v, cache_mem_k, cache_mem_v, state_wkv, state_shift, state_pool, page_table, mem_prompt, norm_ffn1, ffn1_gate, ffn1_up, ffn1_down, norm_mix, w_in, w_out, rw_mu, rw_w0, rw_w2, rw_a0, rw_a2, rw_g2, rw_kk, rw_ka, rw_rk, rw_lnx_w, rw_lnx_b, pool_w, pool_scale, norm_cross, norm_mem, mem_wq, mem_wk, mem_wv, mem_wo, norm_ffn2, ffn2_gate, ffn2_up, ffn2_down, norm_final):
    depth = w_in.shape[0]
    bp, tp, d = x_prompt.shape
    db, ts, _ = x_sample.shape
    n_heads, hd = cache_k.shape[3], cache_k.shape[4]
    aw = n_heads * hd
    r_heads, rn = rw_rk.shape[1], rw_rk.shape[2]
    r_cols = rw_mu.shape[1]
    c_width = pool_scale.shape[1]
    nm, m_heads, mhd = cache_mem_k.shape[2:]
    mw = m_heads * mhd
    assert hd == A_HEAD_DIM and rn == R_HEAD_DIM and mhd == M_HEAD_DIM
    past = page_table.shape[1] * PAGE_SIZE
    slopes = jnp.exp2(-8.0 * jnp.arange(1, n_heads + 1, dtype=F32) / n_heads)
    ck = cache_k.reshape(depth, cache_k.shape[1], PAGE_SIZE, aw)
    cv = cache_v.reshape(depth, cache_v.shape[1], PAGE_SIZE, aw)
    bf = lambda z: z.astype(BF16)
    big = dict(ffn1_gate=bf(ffn1_gate), ffn1_up=bf(ffn1_up), ffn1_down=bf(ffn1_down), w_in=bf(w_in),
               w_out=bf(w_out), mem_wq=bf(mem_wq), mem_wk=bf(mem_wk), mem_wv=bf(mem_wv), mem_wo=bf(mem_wo),
               ffn2_gate=bf(ffn2_gate), ffn2_up=bf(ffn2_up), ffn2_down=bf(ffn2_down))
    small = dict(norm_ffn1=norm_ffn1, norm_mix=norm_mix, rw_mu=rw_mu, rw_w0=rw_w0, rw_w2=rw_w2, rw_a0=rw_a0,
                 rw_a2=rw_a2, rw_g2=rw_g2, rw_kk=rw_kk, rw_ka=rw_ka, rw_rk=rw_rk.reshape(depth, -1),
                 rw_lnx_w=rw_lnx_w, rw_lnx_b=rw_lnx_b, pool_w=pool_w, pool_scale=pool_scale,
                 norm_cross=norm_cross, norm_ffn2=norm_ffn2)

    shift0 = jnp.zeros((bp, r_cols), F32)
    wkv0 = jnp.zeros((bp, r_heads, rn, rn), F32)
    pool0 = jnp.zeros((bp, POOL_MAX - 1, c_width), F32)
    xp, xs = x_prompt, x_sample
    outs = [[] for _ in range(12)]
    for l in range(depth):
        prm = {k: v[l] for k, v in big.items()}
        prm.update({k: v[l] for k, v in small.items()})
        prm.update(a_width=aw, r_cols=r_cols, c_width=c_width)
        hm = rmsnorm(mem_prompt.reshape(bp * nm, d), norm_mem[l], BF16)
        mk = matmul(hm, prm['mem_wk'], mw).reshape(bp, nm, mw)
        mv = matmul(hm, prm['mem_wv'], mw).reshape(bp, nm, mw)
        moba_p = functools.partial(moba_prompt, slopes=slopes)
        xp, k_, v_, w_, sh_, pl_ = _layer(xp, prm, moba_p, 0, shift0, wkv0, pool0, mk, mv)
        res_p = (k_.reshape(bp, tp, n_heads, hd), v_.reshape(bp, tp, n_heads, hd), w_, sh_, pl_,
                 mk.reshape(bp, nm, m_heads, mhd), mv.reshape(bp, nm, m_heads, mhd))
        moba_s = functools.partial(moba_sample, cache_k=ck, cache_v=cv, layer=l, page_table=page_table,
                                   slopes=slopes)
        xs, k_, v_, w_, sh_, pl_ = _layer(xs, prm, moba_s, past, state_shift[l], state_wkv[l], state_pool[l],
                                           cache_mem_k[l].reshape(db, nm, mw), cache_mem_v[l].reshape(db, nm, mw))
        res_s = (k_.reshape(db, ts, n_heads, hd), v_.reshape(db, ts, n_heads, hd), w_, sh_, pl_)
        for i, z in enumerate(res_p + res_s):
            outs[i].append(z)
    y_prompt = rmsnorm(xp.reshape(bp * tp, d), norm_final, F32).reshape(bp, tp, d)
    y_sample = rmsnorm(xs.reshape(db * ts, d), norm_final, F32).reshape(db, ts, d)
    return (y_prompt, y_sample) + tuple(jnp.stack(o) for o in outs)
```

```python
import functools

import jax
import jax.numpy as jnp
from jax import lax
from jax.experimental import pallas as pl
from jax.experimental.pallas import tpu as pltpu

F32 = jnp.float32
BF16 = jnp.bfloat16
HI = lax.Precision.HIGHEST

RMS_EPS = 1e-6
LANES = 128
SUBLANES = 8
VMEM_LIMIT = 48 * 1024 * 1024

PAGE_SIZE = 128
A_HEAD_DIM = 128
A_BLOCK = 256
A_TOPK = 3
A_QCHUNK = 128
R_HEAD_DIM = 64
R_CHUNK = 64
R_GN_EPS = 64e-5
POOL_WINDOWS = (2, 4, 8, 16)
POOL_MAX = 16
M_HEAD_DIM = 128
NEG = -1e30

NT_DIMS = (((1,), (1,)), ((), ()))


def _cparams(sem):
    return pltpu.CompilerParams(dimension_semantics=sem, vmem_limit_bytes=VMEM_LIMIT)


def _row_tile(n, pref):
    return pref if n % pref == 0 else n


def _rms(x, g):
    ms = jnp.mean(x * x, axis=-1, keepdims=True)
    return x * lax.rsqrt(ms + RMS_EPS) * g


def _rms_kernel(x_ref, g_ref, o_ref):
    o_ref[...] = _rms(x_ref[...], g_ref[...]).astype(o_ref.dtype)


def rmsnorm(x, g, out_dtype):
    n, d = x.shape
    tm = _row_tile(n, 512)
    return pl.pallas_call(
        _rms_kernel,
        out_shape=jax.ShapeDtypeStruct((n, d), out_dtype),
        grid=(n // tm,),
        in_specs=[pl.BlockSpec((tm, d), lambda i: (i, 0)), pl.BlockSpec((1, d), lambda i: (0, 0))],
        out_specs=pl.BlockSpec((tm, d), lambda i: (i, 0)),
        compiler_params=_cparams(("parallel",)),
        name="rmsnorm",
    )(x, g.reshape(1, d))


def _matmul_kernel(a_ref, w_ref, o_ref):
    o_ref[...] = jnp.dot(a_ref[...], w_ref[...], preferred_element_type=F32)


def matmul(a, w, tn):
    n, k = a.shape
    m = w.shape[1]
    tm = _row_tile(n, 512)
    return pl.pallas_call(
        _matmul_kernel,
        out_shape=jax.ShapeDtypeStruct((n, m), F32),
        grid=(n // tm, m // tn),
        in_specs=[pl.BlockSpec((tm, k), lambda i, j: (i, 0)), pl.BlockSpec((k, tn), lambda i, j: (0, j))],
        out_specs=pl.BlockSpec((tm, tn), lambda i, j: (i, j)),
        compiler_params=_cparams(("parallel", "arbitrary")),
        name="matmul",
    )(a, w)


def _ffn_kernel(x_ref, g_ref, wg_ref, wu_ref, wd_ref, o_ref, h_scr, acc_scr):
    j = pl.program_id(1)

    @pl.when(j == 0)
    def _():
        h_scr[...] = _rms(x_ref[...], g_ref[...]).astype(BF16)
        acc_scr[...] = jnp.zeros_like(acc_scr)

    h = h_scr[...]
    gate = jnp.dot(h, wg_ref[...], preferred_element_type=F32)
    up = jnp.dot(h, wu_ref[...], preferred_element_type=F32)
    act = (gate * jax.nn.sigmoid(gate) * up).astype(BF16)
    acc_scr[...] += jnp.dot(act, wd_ref[...], preferred_element_type=F32)

    @pl.when(j == pl.num_programs(1) - 1)
    def _():
        o_ref[...] = x_ref[...] + 0.5 * acc_scr[...]


def ffn_half_step(x, g, wg, wu, wd, tf=512):
    n, d = x.shape
    f = wg.shape[1]
    tm = _row_tile(n, 512)
    return pl.pallas_call(
        _ffn_kernel,
        out_shape=jax.ShapeDtypeStruct((n, d), F32),
        grid=(n // tm, f // tf),
        in_specs=[
            pl.BlockSpec((tm, d), lambda i, j: (i, 0)),
            pl.BlockSpec((1, d), lambda i, j: (0, 0)),
            pl.BlockSpec((d, tf), lambda i, j: (0, j)),
            pl.BlockSpec((d, tf), lambda i, j: (0, j)),
            pl.BlockSpec((tf, d), lambda i, j: (j, 0)),
        ],
        out_specs=pl.BlockSpec((tm, d), lambda i, j: (i, 0)),
        scratch_shapes=[pltpu.VMEM((tm, d), BF16), pltpu.VMEM((tm, d), F32)],
        compiler_params=_cparams(("parallel", "arbitrary")),
        name="ffn_half_step",
    )(x, g.reshape(1, d), wg, wu, wd)


def _moba_prompt_kernel(slope_ref, q_ref, k_ref, v_ref, o_ref, kb_scr, vb_scr, km_scr, *, nb):
    h = pl.program_id(1)
    own = pl.program_id(2)
    qc = A_BLOCK
    nbp = km_scr.shape[0]

    @pl.when(own == 0)
    def _():
        k = k_ref[0]
        kb_scr[...] = k.astype(BF16)
        vb_scr[...] = v_ref[0].astype(BF16)
        km_scr[...] = jnp.zeros_like(km_scr)
        for n in range(nb):
            km_scr[n:n + 1, :] = jnp.sum(k[n * A_BLOCK:(n + 1) * A_BLOCK], axis=0, keepdims=True) * (1.0 / A_BLOCK)

    q = q_ref[0]
    slope = slope_ref[h]
    scale = A_HEAD_DIM ** -0.5

    g = lax.dot_general(km_scr[...], q, NT_DIMS, precision=HI, preferred_element_type=F32)
    blk = lax.broadcasted_iota(jnp.int32, (nbp, qc), 0)
    gm = jnp.where(blk < own, g, -jnp.inf)
    sel_t = jnp.zeros((nbp, qc), F32)
    for n in range(nb):
        gn = gm[n:n + 1, :]
        beats = (gm > gn) | ((gm == gn) & (blk < n))
        cnt = jnp.sum(jnp.where(beats, 1.0, 0.0), axis=0, keepdims=True)
        sel_n = jnp.where(cnt < A_TOPK, 1.0, 0.0) * jnp.where(n < own, 1.0, 0.0)
        sel_t = jnp.where(blk == n, sel_n, sel_t)
    sel = jnp.concatenate([sel_t, jnp.zeros((LANES - nbp, qc), F32)], axis=0).T

    qb = (q * scale).astype(BF16)
    row = lax.broadcasted_iota(jnp.int32, (qc, A_BLOCK), 0)
    off_k = lax.broadcasted_iota(jnp.int32, (qc, A_BLOCK), 1)

    def attend(own_k):
        nk = own_k + 1
        s = lax.dot_general(qb, kb_scr[0:nk * A_BLOCK, :], NT_DIMS, preferred_element_type=F32)
        blocks = []
        for n in range(nk):
            sn = s[:, n * A_BLOCK:(n + 1) * A_BLOCK] + slope * (n * A_BLOCK + off_k[0:1, :]).astype(F32)
            if n < own_k:
                allow = jnp.broadcast_to(sel[:, n:n + 1], (qc, A_BLOCK)) > 0.5
            else:
                allow = off_k <= row
            blocks.append(jnp.where(allow, sn, NEG))
        m = blocks[0].max(axis=-1, keepdims=True)
        for n in range(1, nk):
            m = jnp.maximum(m, blocks[n].max(axis=-1, keepdims=True))
        l = jnp.zeros((qc, 1), F32)
        o = jnp.zeros((qc, A_HEAD_DIM), F32)
        for n in range(nk):
            p = jnp.exp(blocks[n] - m)
            l = l + jnp.sum(p, axis=-1, keepdims=True)
            o = o + jnp.dot(p.astype(BF16), vb_scr[n * A_BLOCK:(n + 1) * A_BLOCK, :], preferred_element_type=F32)
        o_ref[0] = o / l

    for own_k in range(nb):
        pl.when(own == own_k)(functools.partial(attend, own_k))


def moba_prompt(q, k, v, slopes):
    b, t, w = q.shape
    nh = w // A_HEAD_DIM
    nb = t // A_BLOCK
    assert t % A_BLOCK == 0 and nb >= A_TOPK and nb <= LANES and A_BLOCK % A_QCHUNK == 0
    nbp = -(-nb // SUBLANES) * SUBLANES
    return pl.pallas_call(
        functools.partial(_moba_prompt_kernel, nb=nb),
        out_shape=jax.ShapeDtypeStruct((b, t, w), F32),
        grid=(b, nh, nb),
        in_specs=[
            pl.BlockSpec(memory_space=pltpu.SMEM),
            pl.BlockSpec((1, A_BLOCK, A_HEAD_DIM), lambda i, h, c: (i, c, h)),
            pl.BlockSpec((1, t, A_HEAD_DIM), lambda i, h, c: (i, 0, h)),
            pl.BlockSpec((1, t, A_HEAD_DIM), lambda i, h, c: (i, 0, h)),
        ],
        out_specs=pl.BlockSpec((1, A_BLOCK, A_HEAD_DIM), lambda i, h, c: (i, c, h)),
        scratch_shapes=[
            pltpu.VMEM((t, A_HEAD_DIM), BF16),
            pltpu.VMEM((t, A_HEAD_DIM), BF16),
            pltpu.VMEM((nbp, A_HEAD_DIM), F32),
        ],
        compiler_params=_cparams(("parallel", "parallel", "arbitrary")),
        name="moba_prompt",
    )(slopes, q, k, v)


def _moba_sample_kernel(pt_ref, q_ref, sl_ref, mb_ref, ob_ref, kn_ref, vn_ref, ka_ref, kb_ref, va_ref, vb_ref,
                        o_ref, g_acc, m_acc, l_acc, o_scr, *, nh, nblk, past):
    n = pl.program_id(1)
    scale = A_HEAD_DIM ** -0.5
    rq = SUBLANES
    q = q_ref[0]
    lane = lax.broadcasted_iota(jnp.int32, (nh * rq, LANES), 1)

    @pl.when(n == 0)
    def _():
        g_acc[...] = jnp.zeros_like(g_acc)
        m_acc[...] = jnp.zeros_like(m_acc)
        l_acc[...] = jnp.zeros_like(l_acc)

    ka, kb = ka_ref[...], kb_ref[...]
    head_sum = lambda pg: jnp.sum(pg.reshape(PAGE_SIZE, nh, A_HEAD_DIM), axis=0)
    kmean = (head_sum(ka) + head_sum(kb)) * (1.0 / A_BLOCK)
    kmean_rows = jnp.concatenate([jnp.broadcast_to(kmean[h:h + 1], (rq, A_HEAD_DIM)) for h in range(nh)], axis=0)
    g = jnp.sum(q * kmean_rows, axis=-1, keepdims=True)
    kblk = jnp.concatenate([ka, kb], axis=0).astype(BF16)
    vblk = jnp.concatenate([va_ref[...], vb_ref[...]], axis=0).astype(BF16)
    qb = q.astype(BF16)
    s = lax.dot_general(qb, kblk, NT_DIMS, preferred_element_type=F32) * scale + mb_ref[...]
    m = s.max(axis=-1, keepdims=True)
    p = jnp.exp(s - m)
    l = jnp.sum(p, axis=-1, keepdims=True)
    m = m - sl_ref[:, 0:1] * (past - n * A_BLOCK).astype(F32)
    here = lane == n
    g_acc[...] = jnp.where(here, g, g_acc[...])
    m_acc[...] = jnp.where(here, m, m_acc[...])
    l_acc[...] = jnp.where(here, l, l_acc[...])
    o_scr[n] = jnp.dot(p.astype(BF16), vblk, preferred_element_type=F32)

    @pl.when(n == nblk - 1)
    def _():
        gm = jnp.where(lane < nblk, g_acc[...], -jnp.inf)
        sel = jnp.zeros(gm.shape, F32)
        for i in range(nblk):
            gi = gm[:, i:i + 1]
            beats = (gm > gi) | ((gm == gi) & (lane < i))
            cnt = jnp.sum(jnp.where(beats, 1.0, 0.0), axis=-1, keepdims=True)
            sel = jnp.where(lane == i, jnp.where(cnt < A_TOPK, 1.0, 0.0), sel)
        selb = sel > 0.5
        mblk = m_acc[...]
        s_own = lax.dot_general(qb, kn_ref[0].astype(BF16), NT_DIMS, preferred_element_type=F32) * scale + ob_ref[...]
        mx = jnp.maximum(s_own.max(axis=-1, keepdims=True),
                         jnp.where(selb, mblk, NEG).max(axis=-1, keepdims=True))
        p_own = jnp.exp(s_own - mx)
        wgt = jnp.where(selb, jnp.exp(mblk - mx), 0.0)
        den = jnp.sum(p_own, axis=-1, keepdims=True) + jnp.sum(wgt * l_acc[...], axis=-1, keepdims=True)
        num = jnp.dot(p_own.astype(BF16), vn_ref[0].astype(BF16), preferred_element_type=F32)
        for i in range(nblk):
            num = num + wgt[:, i:i + 1] * o_scr[i]
        o_ref[0] = num / den


def moba_sample(q, k_new, v_new, cache_k, cache_v, layer, page_table, slopes):
    db, tn, w = q.shape
    nh = w // A_HEAD_DIM
    n_pages = page_table.shape[1]
    past = n_pages * PAGE_SIZE
    ppb = A_BLOCK // PAGE_SIZE
    rq = SUBLANES
    assert past % A_BLOCK == 0 and ppb == 2 and tn <= rq and tn * nh <= LANES and nh == SUBLANES
    nblk = past // A_BLOCK
    assert A_TOPK <= nblk <= LANES
    rows = nh * rq
    qr = jnp.pad(q.reshape(db, tn, nh, A_HEAD_DIM).transpose(0, 2, 1, 3), ((0, 0), (0, 0), (0, rq - tn), (0, 0)))
    qr = qr.reshape(db, rows, A_HEAD_DIM)
    new_rows = lambda z: jnp.pad(z.reshape(db, tn * nh, A_HEAD_DIM), ((0, 0), (0, LANES - tn * nh), (0, 0)))
    r_h = jnp.arange(rows, dtype=jnp.int32)[:, None] // rq
    r_t = jnp.arange(rows, dtype=jnp.int32)[:, None] % rq
    slope_r = slopes[r_h[:, 0]][:, None]
    col = jnp.arange(A_BLOCK * nh, dtype=jnp.int32)[None, :]
    mb = jnp.where(col % nh == r_h, -slope_r * (r_t - col // nh).astype(F32), NEG)
    colo = jnp.arange(LANES, dtype=jnp.int32)[None, :]
    jo = colo // nh
    ob = jnp.where((colo % nh == r_h) & (jo <= r_t) & (jo < tn), -slope_r * (r_t - jo).astype(F32), NEG)
    sl = jnp.broadcast_to(slope_r, (rows, LANES))

    def page_spec(j):
        return pl.BlockSpec((None, None, PAGE_SIZE * nh, A_HEAD_DIM),
                            lambda i, n, pt: (layer, pt[i, ppb * n + j], 0, 0))

    const = lambda shape: pl.BlockSpec(shape, lambda i, n, pt: (0,) * len(shape))
    per_seq = lambda r: pl.BlockSpec((1, r, A_HEAD_DIM), lambda i, n, pt: (i, 0, 0))
    out = pl.pallas_call(
        functools.partial(_moba_sample_kernel, nh=nh, nblk=nblk, past=past),
        out_shape=jax.ShapeDtypeStruct((db, rows, A_HEAD_DIM), F32),
        grid_spec=pltpu.PrefetchScalarGridSpec(
            num_scalar_prefetch=1,
            grid=(db, nblk),
            in_specs=[per_seq(rows), const((rows, LANES)), const((rows, A_BLOCK * nh)), const((rows, LANES)),
                      per_seq(LANES), per_seq(LANES), page_spec(0), page_spec(1), page_spec(0), page_spec(1)],
            out_specs=per_seq(rows),
            scratch_shapes=[pltpu.VMEM((rows, LANES), F32)] * 3 + [pltpu.VMEM((nblk, rows, A_HEAD_DIM), F32)],
        ),
        compiler_params=_cparams(("parallel", "arbitrary")),
        name="moba_sample",
    )(page_table, qr, sl, mb, ob, new_rows(k_new), new_rows(v_new), cache_k, cache_k, cache_v, cache_v)
    out = out.reshape(db, nh, rq, A_HEAD_DIM)[:, :, :tn].transpose(0, 2, 1, 3)
    return out.reshape(db, tn, w)


def _head_sum_matrix():
    i = lax.broadcasted_iota(jnp.int32, (LANES, LANES), 0)
    j = lax.broadcasted_iota(jnp.int32, (LANES, LANES), 1)
    return jnp.where(lax.shift_right_logical(i, 6) == lax.shift_right_logical(j, 6), 1.0, 0.0).astype(F32)


def _rwkv_prep_kernel(u_ref, pv_ref, s0_ref, mu_ref, w0_ref, w2_ref, a0_ref, a2_ref, g2_ref, kk_ref, ka_ref,
                      r_o, lw_o, k_o, v_o, kk_o, a_o, g_o, *, rw):
    t = pl.program_id(1)
    u = u_ref[0]
    tt = u.shape[0]
    prev_row = jnp.where(t == 0, s0_ref[0], pv_ref[0, SUBLANES - 1:SUBLANES, :])
    row = lax.broadcasted_iota(jnp.int32, u.shape, 0)
    prev = jnp.where(row == 0, prev_row, pltpu.roll(u, 1, axis=0))
    xs = u + (prev - u) * mu_ref[...]
    r = xs[:, 0:rw]
    k = xs[:, rw:2 * rw]
    v = xs[:, 2 * rw:3 * rw]
    lora_wa = xs[:, 3 * rw:3 * rw + LANES]
    gd = xs[:, 3 * rw + LANES:3 * rw + 2 * LANES]
    z = w0_ref[...] + jnp.dot(jnp.tanh(lora_wa), w2_ref[...], precision=HI, preferred_element_type=F32)
    nz = -z
    softplus = jnp.maximum(nz, 0.0) + jnp.log1p(jnp.exp(-jnp.abs(nz)))
    w_log = -softplus - 0.5
    lw_o[0] = -jnp.exp(w_log)
    a = jax.nn.sigmoid(a0_ref[...] + jnp.dot(lora_wa, a2_ref[...], precision=HI, preferred_element_type=F32))
    g_o[0] = jnp.dot(jax.nn.sigmoid(gd), g2_ref[...], precision=HI, preferred_element_type=F32)
    kk = k * kk_ref[...]
    seg = _head_sum_matrix()
    for p in range(rw // LANES):
        ps = slice(p * LANES, (p + 1) * LANES)
        kkp = kk[:, ps]
        ss = jnp.dot(kkp * kkp, seg, precision=HI, preferred_element_type=F32)
        kk_o[0, :, ps] = kkp / jnp.maximum(jnp.sqrt(ss), 1e-12)
    r_o[0] = r
    k_o[0] = k * (1.0 + (a - 1.0) * ka_ref[...])
    v_o[0] = v
    a_o[0] = a


def rwkv_prep(ur, shift0, prm):
    b, t, cols = ur.shape
    rw = prm['rw_w0'].shape[-1]
    tt = _row_tile(t, 256)
    nlora = prm['rw_w2'].shape[0]
    assert 2 * nlora == LANES and prm['rw_g2'].shape[0] == LANES and cols == 3 * rw + 2 * LANES
    w2p = jnp.concatenate([prm['rw_w2'], jnp.zeros_like(prm['rw_w2'])], axis=0)
    a2p = jnp.concatenate([jnp.zeros_like(prm['rw_a2']), prm['rw_a2']], axis=0)
    row = lambda z: z.reshape(1, -1)
    full = lambda shape: pl.BlockSpec(shape, lambda i, j: (0,) * len(shape))
    outs = pl.pallas_call(
        functools.partial(_rwkv_prep_kernel, rw=rw),
        out_shape=[jax.ShapeDtypeStruct((b, t, rw), F32)] * 7,
        grid=(b, t // tt),
        in_specs=[
            pl.BlockSpec((1, tt, cols), lambda i, j: (i, j, 0)),
            pl.BlockSpec((1, SUBLANES, cols), lambda i, j: (i, jnp.maximum(j * (tt // SUBLANES) - 1, 0), 0)),
            pl.BlockSpec((1, 1, cols), lambda i, j: (i, 0, 0)),
            full((1, cols)), full((1, rw)), full((LANES, rw)), full((1, rw)), full((LANES, rw)),
            full((LANES, rw)), full((1, rw)), full((1, rw)),
        ],
        out_specs=[pl.BlockSpec((1, tt, rw), lambda i, j: (i, j, 0))] * 7,
        compiler_params=_cparams(("parallel", "parallel")),
        name="rwkv_prep",
    )(ur, ur, shift0.reshape(b, 1, cols), row(prm['rw_mu']), row(prm['rw_w0']), w2p, row(prm['rw_a0']), a2p,
      prm['rw_g2'], row(prm['rw_kk']), row(prm['rw_ka']))
    return outs


def _pair_masks():
    i = lax.broadcasted_iota(jnp.int32, (LANES, LANES), 0)
    j = lax.broadcasted_iota(jnp.int32, (LANES, LANES), 1)
    same = lax.shift_right_logical(i, 6) == lax.shift_right_logical(j, 6)
    return i, j, same


def _split(x):
    hi = x.astype(BF16)
    return hi, (x - hi.astype(F32)).astype(BF16)


def _mm3(a, b):
    d = lambda x, y: jnp.dot(x, y, preferred_element_type=F32)
    return d(a[0], b[0]) + d(a[0], b[1]) + d(a[1], b[0])


def _mm3_nt(a, b):
    d = lambda x, y: lax.dot_general(x, y, NT_DIMS, preferred_element_type=F32)
    return d(a[0], b[0]) + d(a[0], b[1]) + d(a[1], b[0])


def _each(f, *lists):
    return [f(*xs) for xs in zip(*lists)]


def _rwkv_chunk_factors(r, lw, k, v, kk, a):
    c = R_CHUNK
    row = lax.broadcasted_iota(jnp.int32, (c, LANES), 0)
    cum = lw
    s = 1
    while s < c:
        cum = _each(lambda z: z + jnp.where(row >= s, pltpu.roll(z, s, axis=0), 0.0), cum)
        s *= 2
    cl = _each(lambda z: z[c - 1:c, :], cum)
    beta = _each(lambda x, y: x * y, kk, a)
    kap_t = _each(lambda x, cu, l: x * jnp.exp(cu - l), kk, cum, lw)
    r_t = _each(lambda x, cu: x * jnp.exp(cu), r, cum)
    e_inv = _each(lambda cu: jnp.exp(-cu), cum)
    b_t = _each(lambda x, e: x * e, beta, e_inv)
    k_t = _each(lambda x, e: x * e, k, e_inv)
    e_end = _each(lambda l, cu: jnp.exp(l - cu), cl, cum)
    b_h = _each(lambda x, e: x * e, beta, e_end)
    k_h = _each(lambda x, e: x * e, k, e_end)

    lo = lax.broadcasted_iota(jnp.int32, (c, LANES), 1) < R_HEAD_DIM
    stack = lambda x: jnp.concatenate([jnp.where(lo, x, 0.0), jnp.where(lo, 0.0, x)], axis=0)
    dup = lambda x: jnp.concatenate([x, x], axis=0)
    split_of = lambda f: (lambda x: _split(f(x)))
    ident = lambda x: x

    i, j, same = _pair_masks()
    strict = same & (j < i)
    incl = same & (j <= i)
    kap_s, r_s, v_s = _each(stack, kap_t), _each(stack, r_t), _each(stack, v)
    kap_p, r_p, v_p = _each(_split, kap_s), _each(_split, r_s), _each(_split, v_s)
    b_d, k_d = _each(split_of(dup), b_t), _each(split_of(dup), k_t)
    a_ab = _each(lambda x, y: jnp.where(strict, _mm3_nt(x, y), 0.0), kap_p, b_d)
    a_ak = _each(lambda x, y: jnp.where(strict, _mm3_nt(x, y), 0.0), kap_p, k_d)
    a_rb = _each(lambda x, y: jnp.where(incl, _mm3_nt(x, y), 0.0), r_p, b_d)
    a_rk = _each(lambda x, y: jnp.where(incl, _mm3_nt(x, y), 0.0), r_p, k_d)

    x = [jnp.where(i == j, 1.0, 0.0).astype(F32)] * len(r)
    s = 1
    while s < c:
        low = (lax.shift_right_logical(i, s.bit_length()) == lax.shift_right_logical(j, s.bit_length())) \
            & ((i & (2 * s - 1)) >= s) & ((j & (2 * s - 1)) < s)
        xp = _each(_split, x)
        ms = _each(lambda z: _split(jnp.where(low, z, 0.0)), a_ab)
        t1 = _each(split_of(ident), _each(_mm3, xp, ms))
        x = _each(lambda z, u, w_: z - _mm3(u, w_), x, t1, xp)
        s *= 2

    aakv = _each(_mm3, _each(_split, a_ak), v_p)
    arkv = _each(_mm3, _each(_split, a_rk), v_p)
    bh_p = _each(split_of(stack), b_h)
    vtk = _each(_mm3, _each(lambda z: _split(z.T), v_s), _each(split_of(stack), k_h))
    rhs = _each(lambda z, u: _split(jnp.concatenate([z, u], axis=1)), kap_s, aakv)
    wu = _each(lambda z, u: -_mm3(z, u), _each(_split, x), rhs)
    w = _each(lambda z: z[:, :LANES], wu)
    upre = _each(lambda z: z[:, LANES:], wu)
    arb_p = _each(_split, a_rb)
    r2 = _each(lambda z, u, w_: z + _mm3(u, _split(w_)), r_s, arb_p, w)
    ypre = _each(lambda z, u, w_: z + _mm3(u, _split(w_)), arkv, arb_p, upre)
    g = _each(lambda l, w_, u: jnp.where(i == j, jnp.exp(l), 0.0) + _mm3(_split(w_.T), u), cl, w, bh_p)
    spre = _each(lambda z, w_, u: z + _mm3(_split(w_.T), u), vtk, upre, bh_p)
    return list(zip(r2, ypre, g, spre))


def _rwkv_core_kernel(r_ref, lw_ref, k_ref, v_ref, kk_ref, a_ref, s0_ref, y_o, st_o, s_scr, *, nch):
    cidx = pl.program_id(2)

    @pl.when(cidx == 0)
    def _():
        s_scr[...] = s0_ref[0, 0]

    c = R_CHUNK
    chunks = lambda ref: [ref[0, ch * c:(ch + 1) * c] for ch in range(nch)]
    factors = _rwkv_chunk_factors(chunks(r_ref), chunks(lw_ref), chunks(k_ref), chunks(v_ref), chunks(kk_ref),
                                  chunks(a_ref))
    s = s_scr[...]
    for ch, (r2, ypre, g, spre) in enumerate(factors):
        sp = _split(s)
        ys = _mm3_nt(_split(r2), sp) + ypre
        y_o[0, ch * c:(ch + 1) * c, :] = ys[:c] + ys[c:]
        s = _mm3(sp, _split(g)) + spre
    s_scr[...] = s
    st_o[0, 0] = s


def rwkv_scan(r, lw, k, v, kk, a, wkv0):
    b, t, rw = r.shape
    npair = rw // LANES
    assert t % R_CHUNK == 0 and 2 * R_HEAD_DIM == LANES
    nch = max(n for n in (1, 2, 4) if t % (n * R_CHUNK) == 0)
    tt = nch * R_CHUNK
    w4 = wkv0.reshape(b, npair, 2, R_HEAD_DIM, R_HEAD_DIM)
    zero = jnp.zeros_like(w4[:, :, 0])
    s0 = jnp.concatenate([jnp.concatenate([w4[:, :, 0], zero], axis=-1),
                          jnp.concatenate([zero, w4[:, :, 1]], axis=-1)], axis=-2)
    act = pl.BlockSpec((1, tt, LANES), lambda i, p, c: (i, c, p))
    st_spec = pl.BlockSpec((1, 1, LANES, LANES), lambda i, p, c: (i, p, 0, 0))
    y, st = pl.pallas_call(
        functools.partial(_rwkv_core_kernel, nch=nch),
        out_shape=[jax.ShapeDtypeStruct((b, t, rw), F32), jax.ShapeDtypeStruct((b, npair, LANES, LANES), F32)],
        grid=(b, npair, t // tt),
        in_specs=[act] * 6 + [st_spec],
        out_specs=[act, st_spec],
        scratch_shapes=[pltpu.VMEM((LANES, LANES), F32)],
        compiler_params=_cparams(("parallel", "parallel", "arbitrary")),
        name="rwkv_core",
    )(r, lw, k, v, kk, a, s0)
    wkv = jnp.stack([st[:, :, :R_HEAD_DIM, :R_HEAD_DIM], st[:, :, R_HEAD_DIM:, R_HEAD_DIM:]], axis=2)
    return y, wkv.reshape(wkv0.shape)


def _rwkv_post_kernel(y_ref, r_ref, k_ref, v_ref, g_ref, rk_ref, lw_ref, lb_ref, o_ref):
    seg = _head_sum_matrix()
    inv_n = 1.0 / R_HEAD_DIM
    for p in range(y_ref.shape[2] // LANES):
        ps = slice(p * LANES, (p + 1) * LANES)
        y = y_ref[0, :, ps]
        hsum = lambda z: jnp.dot(z, seg, precision=HI, preferred_element_type=F32)
        mean = hsum(y) * inv_n
        d = y - mean
        var = hsum(d * d) * inv_n
        yn = d * lax.rsqrt(var + R_GN_EPS) * lw_ref[:, ps] + lb_ref[:, ps]
        bonus = hsum(r_ref[0, :, ps] * k_ref[0, :, ps] * rk_ref[:, ps]) * v_ref[0, :, ps]
        o_ref[0, :, ps] = (yn + bonus) * g_ref[0, :, ps]


def rwkv_post(y, r, k, v, g, prm):
    b, t, rw = y.shape
    tt = _row_tile(t, 256)
    act = pl.BlockSpec((1, tt, rw), lambda i, j: (i, j, 0))
    par = pl.BlockSpec((1, rw), lambda i, j: (0, 0))
    return pl.pallas_call(
        _rwkv_post_kernel,
        out_shape=jax.ShapeDtypeStruct((b, t, rw), F32),
        grid=(b, t // tt),
        in_specs=[act] * 5 + [par] * 3,
        out_specs=act,
        compiler_params=_cparams(("parallel", "parallel")),
        name="rwkv_post",
    )(y, r, k, v, g, prm['rw_rk'].reshape(1, rw), prm['rw_lnx_w'].reshape(1, rw), prm['rw_lnx_b'].reshape(1, rw))


def rwkv_mix(ur, shift0, wkv0, prm):
    b, t, _ = ur.shape
    tp = -(-t // SUBLANES) * SUBLANES
    urp = jnp.pad(ur, ((0, 0), (0, tp - t), (0, 0)))
    r, lw, k, v, kk, a, g = rwkv_prep(urp, shift0, prm)
    tc = -(-t // R_CHUNK) * R_CHUNK
    fit = lambda z: jnp.pad(z[:, :t], ((0, 0), (0, tc - t), (0, 0)))
    r, lw, k, v, kk, a, g = (fit(z) for z in (r, lw, k, v, kk, a, g))
    y, wkv = rwkv_scan(r, lw, k, v, kk, a, wkv0)
    out = rwkv_post(y, r, k, v, g, prm)
    return out[:, :t], wkv, ur[:, -1]


def _pool_kernel(u_ref, pv_ref, p0_ref, w_ref, sc_ref, o_ref, *, pos0):
    t = pl.program_id(1)
    cur = u_ref[0]
    tt = cur.shape[0]
    ext = jnp.concatenate([jnp.where(t == 0, p0_ref[0], pv_ref[0]), cur], axis=0)
    sums = []
    s = ext
    w = 1
    while w < POOL_MAX:
        s = s + pltpu.roll(s, w, axis=0)
        w *= 2
        sums.append(s)
    pos = pos0 + t * tt + lax.broadcasted_iota(jnp.int32, (tt, 1), 0)
    for gi, w in enumerate(POOL_WINDOWS):
        gs = slice(gi * LANES, (gi + 1) * LANES)
        win = sums[w.bit_length() - 2][POOL_MAX:, gs]
        cnt = jnp.minimum(pos + 1, w).astype(F32)
        m = win / cnt - cur[:, gs]
        z = jnp.dot(m.astype(BF16), w_ref[gi], preferred_element_type=F32)
        o_ref[0, :, gs] = z * sc_ref[:, gs]


def pool_mix(uc, pool0, pos0, pool_w, pool_scale):
    b, t, width = uc.shape
    assert width == len(POOL_WINDOWS) * LANES and pool_w.shape[1] == LANES
    tp = -(-t // SUBLANES) * SUBLANES
    ucp = jnp.pad(uc, ((0, 0), (0, tp - t), (0, 0)))
    tt = _row_tile(tp, 256)
    assert tt % POOL_MAX == 0 or tp == tt
    p0 = jnp.pad(pool0, ((0, 0), (1, 0), (0, 0)))
    nprev = tt // POOL_MAX if tt % POOL_MAX == 0 else 0
    ext = jnp.concatenate([pool0, uc], axis=1)
    if nprev:
        pv, pv_spec = ucp, pl.BlockSpec((1, POOL_MAX, width), lambda i, j: (i, jnp.maximum(j * nprev - 1, 0), 0))
    else:
        pv, pv_spec = p0, pl.BlockSpec((1, POOL_MAX, width), lambda i, j: (i, 0, 0))
    z = pl.pallas_call(
        functools.partial(_pool_kernel, pos0=pos0),
        out_shape=jax.ShapeDtypeStruct((b, tp, width), F32),
        grid=(b, tp // tt),
        in_specs=[
            pl.BlockSpec((1, tt, width), lambda i, j: (i, j, 0)),
            pv_spec,
            pl.BlockSpec((1, POOL_MAX, width), lambda i, j: (i, 0, 0)),
            pl.BlockSpec(pool_w.shape, lambda i, j: (0, 0, 0)),
            pl.BlockSpec((1, width), lambda i, j: (0, 0)),
        ],
        out_specs=pl.BlockSpec((1, tt, width), lambda i, j: (i, j, 0)),
        compiler_params=_cparams(("parallel", "parallel")),
        name="pool_mix",
    )(ucp, pv, p0, pool_w.astype(BF16), pool_scale.reshape(1, width))
    return z[:, :t], ext[:, -(POOL_MAX - 1):]


def _mix_out_kernel(x_ref, ya_ref, yr_ref, yc_ref, wa_ref, wr_ref, wc_ref, o_ref):
    acc = jnp.dot(ya_ref[...].astype(BF16), wa_ref[...], preferred_element_type=F32)
    acc += jnp.dot(yr_ref[...].astype(BF16), wr_ref[...], preferred_element_type=F32)
    acc += jnp.dot(yc_ref[...].astype(BF16), wc_ref[...], preferred_element_type=F32)
    o_ref[...] = x_ref[...] + acc


def mix_out(x, ya, yr, yc, w_out):
    n, d = x.shape
    wa_, wr_, wc_ = ya.shape[1], yr.shape[1], yc.shape[1]
    tm = _row_tile(n, 512)
    tn = 1024
    act = lambda wd: pl.BlockSpec((tm, wd), lambda i, j: (i, 0))
    wsp = lambda wd: pl.BlockSpec((wd, tn), lambda i, j: (0, j))
    return pl.pallas_call(
        _mix_out_kernel,
        out_shape=jax.ShapeDtypeStruct((n, d), F32),
        grid=(n // tm, d // tn),
        in_specs=[pl.BlockSpec((tm, tn), lambda i, j: (i, j)), act(wa_), act(wr_), act(wc_),
                  wsp(wa_), wsp(wr_), wsp(wc_)],
        out_specs=pl.BlockSpec((tm, tn), lambda i, j: (i, j)),
        compiler_params=_cparams(("parallel", "arbitrary")),
        name="mix_out",
    )(x, ya, yr, yc, w_out[:wa_], w_out[wa_:wa_ + wr_], w_out[wa_ + wr_:])


def _cross_kernel(x_ref, g_ref, wq_ref, mk_ref, mv_ref, wo_ref, o_ref, *, nh):
    x = x_ref[0]
    h = _rms(x, g_ref[...]).astype(BF16)
    q = jnp.dot(h, wq_ref[...], preferred_element_type=F32)
    scale = M_HEAD_DIM ** -0.5
    outs = []
    for hh in range(nh):
        hs = slice(hh * M_HEAD_DIM, (hh + 1) * M_HEAD_DIM)
        s = lax.dot_general(q[:, hs].astype(BF16), mk_ref[0, :, hs].astype(BF16), NT_DIMS,
                            preferred_element_type=F32) * scale
        p = jnp.exp(s - s.max(axis=-1, keepdims=True))
        l = jnp.sum(p, axis=-1, keepdims=True)
        outs.append(jnp.dot(p.astype(BF16), mv_ref[0, :, hs].astype(BF16), preferred_element_type=F32) / l)
    o = jnp.concatenate(outs, axis=-1).astype(BF16)
    o_ref[0] = x + jnp.dot(o, wo_ref[...], preferred_element_type=F32)


def cross_attend(x, g, wq, mk, mv, wo):
    b, t0, d = x.shape
    t = -(-t0 // SUBLANES) * SUBLANES
    x = jnp.pad(x, ((0, 0), (0, t - t0), (0, 0)))
    mw = wq.shape[1]
    nm = mk.shape[1]
    tm = _row_tile(t, 512)
    out = pl.pallas_call(
        functools.partial(_cross_kernel, nh=mw // M_HEAD_DIM),
        out_shape=jax.ShapeDtypeStruct((b, t, d), F32),
        grid=(b, t // tm),
        in_specs=[
            pl.BlockSpec((1, tm, d), lambda i, j: (i, j, 0)),
            pl.BlockSpec((1, d), lambda i, j: (0, 0)),
            pl.BlockSpec((d, mw), lambda i, j: (0, 0)),
            pl.BlockSpec((1, nm, mw), lambda i, j: (i, 0, 0)),
            pl.BlockSpec((1, nm, mw), lambda i, j: (i, 0, 0)),
            pl.BlockSpec((mw, d), lambda i, j: (0, 0)),
        ],
        out_specs=pl.BlockSpec((1, tm, d), lambda i, j: (i, j, 0)),
        compiler_params=_cparams(("parallel", "parallel")),
        name="cross_attend",
    )(x, g.reshape(1, d), wq, mk, mv, wo)
    return out[:, :t0]


def _layer(x, prm, moba_fn, pos0, shift0, wkv0, pool0, mem_k, mem_v):
    b, t, d = x.shape
    n = b * t
    x2 = ffn_half_step(x.reshape(n, d), prm['norm_ffn1'], prm['ffn1_gate'], prm['ffn1_up'], prm['ffn1_down'])
    h = rmsnorm(x2, prm['norm_mix'], BF16)
    aw, rc, cw = prm['a_width'], prm['r_cols'], prm['c_width']
    w_in = prm['w_in']
    qa = matmul(h, w_in[:, :aw], 512).reshape(b, t, aw)
    ka = matmul(h, w_in[:, aw:2 * aw], 512).reshape(b, t, aw)
    va = matmul(h, w_in[:, 2 * aw:3 * aw], 512).reshape(b, t, aw)
    ur = matmul(h, w_in[:, 3 * aw:3 * aw + rc], rc // 2).reshape(b, t, rc)
    uc = matmul(h, w_in[:, 3 * aw + rc:], cw).reshape(b, t, cw)
    ya = moba_fn(qa, ka, va)
    yr, wkv, shift = rwkv_mix(ur, shift0, wkv0, prm)
    yc, pool_buf = pool_mix(uc, pool0, pos0, prm['pool_w'], prm['pool_scale'])
    x3 = mix_out(x2, ya.reshape(n, aw), yr.reshape(n, -1), yc.reshape(n, cw), prm['w_out'])
    x4 = cross_attend(x3.reshape(b, t, d), prm['norm_cross'], prm['mem_wq'], mem_k, mem_v, prm['mem_wo'])
    x5 = ffn_half_step(x4.reshape(n, d), prm['norm_ffn2'], prm['ffn2_gate'], prm['ffn2_up'], prm['ffn2_down'])
    return x5.reshape(b, t, d), ka, va, wkv, shift, pool_buf


def kernel(x_prompt, x_sample, cache_k, cache_v, cache_mem_k, cache_mem_v, state_wkv, state_shift, state_pool, page_table, mem_prompt, norm_ffn1, ffn1_gate, ffn1_up, ffn1_down, norm_mix, w_in, w_out, rw_mu, rw_w0, rw_w2, rw_a0, rw_a2, rw_g2, rw_kk, rw_ka, rw_rk, rw_lnx_w, rw_lnx_b, pool_w, pool_scale, norm_cross, norm_mem, mem_wq, mem_wk, mem_wv, mem_wo, norm_ffn2, ffn2_gate, ffn2_up, ffn2_down, norm_final):
    depth = w_in.shape[0]
    bp, tp, d = x_prompt.shape
    db, ts, _ = x_sample.shape
    n_heads, hd = cache_k.shape[3], cache_k.shape[4]
    aw = n_heads * hd
    r_heads, rn = rw_rk.shape[1], rw_rk.shape[2]
    r_cols = rw_mu.shape[1]
    c_width = pool_scale.shape[1]
    nm, m_heads, mhd = cache_mem_k.shape[2:]
    mw = m_heads * mhd
    assert hd == A_HEAD_DIM and rn == R_HEAD_DIM and mhd == M_HEAD_DIM
    past = page_table.shape[1] * PAGE_SIZE
    slopes = jnp.exp2(-8.0 * jnp.arange(1, n_heads + 1, dtype=F32) / n_heads)
    ck = cache_k.reshape(depth, cache_k.shape[1], PAGE_SIZE * n_heads, hd)
    cv = cache_v.reshape(depth, cache_v.shape[1], PAGE_SIZE * n_heads, hd)
    bf = lambda z: z.astype(BF16)
    big = dict(ffn1_gate=bf(ffn1_gate), ffn1_up=bf(ffn1_up), ffn1_down=bf(ffn1_down), w_in=bf(w_in),
               w_out=bf(w_out), mem_wq=bf(mem_wq), mem_wk=bf(mem_wk), mem_wv=bf(mem_wv), mem_wo=bf(mem_wo),
               ffn2_gate=bf(ffn2_gate), ffn2_up=bf(ffn2_up), ffn2_down=bf(ffn2_down))
    small = dict(norm_ffn1=norm_ffn1, norm_mix=norm_mix, rw_mu=rw_mu, rw_w0=rw_w0, rw_w2=rw_w2, rw_a0=rw_a0,
                 rw_a2=rw_a2, rw_g2=rw_g2, rw_kk=rw_kk, rw_ka=rw_ka, rw_rk=rw_rk.reshape(depth, -1),
                 rw_lnx_w=rw_lnx_w, rw_lnx_b=rw_lnx_b, pool_w=pool_w, pool_scale=pool_scale,
                 norm_cross=norm_cross, norm_ffn2=norm_ffn2)

    shift0 = jnp.zeros((bp, r_cols), F32)
    wkv0 = jnp.zeros((bp, r_heads, rn, rn), F32)
    pool0 = jnp.zeros((bp, POOL_MAX - 1, c_width), F32)
    xp, xs = x_prompt, x_sample
    outs = [[] for _ in range(12)]
    for l in range(depth):
        prm = {k: v[l] for k, v in big.items()}
        prm.update({k: v[l] for k, v in small.items()})
        prm.update(a_width=aw, r_cols=r_cols, c_width=c_width)
        hm = rmsnorm(mem_prompt.reshape(bp * nm, d), norm_mem[l], BF16)
        mk = matmul(hm, prm['mem_wk'], mw).reshape(bp, nm, mw)
        mv = matmul(hm, prm['mem_wv'], mw).reshape(bp, nm, mw)
        moba_p = functools.partial(moba_prompt, slopes=slopes)
        xp, k_, v_, w_, sh_, pl_ = _layer(xp, prm, moba_p, 0, shift0, wkv0, pool0, mk, mv)
        res_p = (k_.reshape(bp, tp, n_heads, hd), v_.reshape(bp, tp, n_heads, hd), w_, sh_, pl_,
                 mk.reshape(bp, nm, m_heads, mhd), mv.reshape(bp, nm, m_heads, mhd))
        moba_s = functools.partial(moba_sample, cache_k=ck, cache_v=cv, layer=l, page_table=page_table,
                                   slopes=slopes)
        xs, k_, v_, w_, sh_, pl_ = _layer(xs, prm, moba_s, past, state_shift[l], state_wkv[l], state_pool[l],
                                           cache_mem_k[l].reshape(db, nm, mw), cache_mem_v[l].reshape(db, nm, mw))
        res_s = (k_.reshape(db, ts, n_heads, hd), v_.reshape(db, ts, n_heads, hd), w_, sh_, pl_)
        for i, z in enumerate(res_p + res_s):
            outs[i].append(z)
    y_prompt = rmsnorm(xp.reshape(bp * tp, d), norm_final, F32).reshape(bp, tp, d)
    y_sample = rmsnorm(xs.reshape(db * ts, d), norm_final, F32).reshape(db, ts, d)
    return (y_prompt, y_sample) + tuple(jnp.stack(o) for o in outs)
```

```python
import functools

import jax
import jax.numpy as jnp
from jax import lax
from jax.experimental import pallas as pl
from jax.experimental.pallas import tpu as pltpu

F32 = jnp.float32
BF16 = jnp.bfloat16
HI = lax.Precision.HIGHEST

RMS_EPS = 1e-6
LANES = 128
SUBLANES = 8
VMEM_LIMIT = 48 * 1024 * 1024

PAGE_SIZE = 128
A_HEAD_DIM = 128
A_BLOCK = 256
A_TOPK = 3
A_QCHUNK = 128
R_HEAD_DIM = 64
R_CHUNK = 64
R_GN_EPS = 64e-5
POOL_WINDOWS = (2, 4, 8, 16)
POOL_MAX = 16
M_HEAD_DIM = 128
NEG = -1e30

NT_DIMS = (((1,), (1,)), ((), ()))


def _cparams(sem):
    return pltpu.CompilerParams(dimension_semantics=sem, vmem_limit_bytes=VMEM_LIMIT)


def _row_tile(n, pref):
    return pref if n % pref == 0 else n


def _rms(x, g):
    ms = jnp.mean(x * x, axis=-1, keepdims=True)
    return x * lax.rsqrt(ms + RMS_EPS) * g


def _rms_kernel(x_ref, g_ref, o_ref):
    o_ref[...] = _rms(x_ref[...], g_ref[...]).astype(o_ref.dtype)


def rmsnorm(x, g, out_dtype):
    n, d = x.shape
    tm = _row_tile(n, 512)
    return pl.pallas_call(
        _rms_kernel,
        out_shape=jax.ShapeDtypeStruct((n, d), out_dtype),
        grid=(n // tm,),
        in_specs=[pl.BlockSpec((tm, d), lambda i: (i, 0)), pl.BlockSpec((1, d), lambda i: (0, 0))],
        out_specs=pl.BlockSpec((tm, d), lambda i: (i, 0)),
        compiler_params=_cparams(("parallel",)),
        name="rmsnorm",
    )(x, g.reshape(1, d))


def _matmul_kernel(a_ref, w_ref, o_ref):
    o_ref[...] = jnp.dot(a_ref[...], w_ref[...], preferred_element_type=F32)


def matmul(a, w, tn):
    n, k = a.shape
    m = w.shape[1]
    tm = _row_tile(n, 512)
    return pl.pallas_call(
        _matmul_kernel,
        out_shape=jax.ShapeDtypeStruct((n, m), F32),
        grid=(n // tm, m // tn),
        in_specs=[pl.BlockSpec((tm, k), lambda i, j: (i, 0)), pl.BlockSpec((k, tn), lambda i, j: (0, j))],
        out_specs=pl.BlockSpec((tm, tn), lambda i, j: (i, j)),
        compiler_params=_cparams(("parallel", "arbitrary")),
        name="matmul",
    )(a, w)


IN_TILE = 512


def _in_proj_kernel(x_ref, g_ref, w_ref, *rest, first_tile):
    out_refs, h_scr = rest[:-1], rest[-1]
    j = pl.program_id(1)

    @pl.when(j == 0)
    def _():
        h_scr[...] = _rms(x_ref[...], g_ref[...]).astype(BF16)

    acc = jnp.dot(h_scr[...], w_ref[...], preferred_element_type=F32)
    for o_ref, lo, hi in zip(out_refs, first_tile[:-1], first_tile[1:]):
        @pl.when((j >= lo) & (j < hi))
        def _(o_ref=o_ref):
            o_ref[...] = acc


def in_proj(x, g, w, widths):
    n, d = x.shape
    tm = _row_tile(n, 512)
    tiles = [-(-wd // IN_TILE) for wd in widths]
    first_tile = [0]
    for nt in tiles:
        first_tile.append(first_tile[-1] + nt)
    assert w.shape[1] == first_tile[-1] * IN_TILE

    def out_spec(lo, nt):
        return pl.BlockSpec((tm, IN_TILE), lambda i, j: (i, jnp.clip(j - lo, 0, nt - 1)))

    return pl.pallas_call(
        functools.partial(_in_proj_kernel, first_tile=tuple(first_tile)),
        out_shape=[jax.ShapeDtypeStruct((n, nt * IN_TILE), F32) for nt in tiles],
        grid=(n // tm, first_tile[-1]),
        in_specs=[pl.BlockSpec((tm, d), lambda i, j: (i, 0)), pl.BlockSpec((1, d), lambda i, j: (0, 0)),
                  pl.BlockSpec((d, IN_TILE), lambda i, j: (0, j))],
        out_specs=[out_spec(lo, nt) for lo, nt in zip(first_tile[:-1], tiles)],
        scratch_shapes=[pltpu.VMEM((tm, d), BF16)],
        compiler_params=_cparams(("parallel", "arbitrary")),
        name="in_proj",
    )(x, g.reshape(1, d), w)


def _ffn_kernel(x_ref, g_ref, wg_ref, wu_ref, wd_ref, o_ref, h_scr, acc_scr):
    j = pl.program_id(1)

    @pl.when(j == 0)
    def _():
        h_scr[...] = _rms(x_ref[...], g_ref[...]).astype(BF16)
        acc_scr[...] = jnp.zeros_like(acc_scr)

    h = h_scr[...]
    gate = jnp.dot(h, wg_ref[...], preferred_element_type=F32)
    up = jnp.dot(h, wu_ref[...], preferred_element_type=F32)
    act = (gate * jax.nn.sigmoid(gate) * up).astype(BF16)
    acc_scr[...] += jnp.dot(act, wd_ref[...], preferred_element_type=F32)

    @pl.when(j == pl.num_programs(1) - 1)
    def _():
        o_ref[...] = x_ref[...] + 0.5 * acc_scr[...]


def ffn_half_step(x, g, wg, wu, wd, tf=512):
    n, d = x.shape
    f = wg.shape[1]
    tm = _row_tile(n, 512)
    return pl.pallas_call(
        _ffn_kernel,
        out_shape=jax.ShapeDtypeStruct((n, d), F32),
        grid=(n // tm, f // tf),
        in_specs=[
            pl.BlockSpec((tm, d), lambda i, j: (i, 0)),
            pl.BlockSpec((1, d), lambda i, j: (0, 0)),
            pl.BlockSpec((d, tf), lambda i, j: (0, j)),
            pl.BlockSpec((d, tf), lambda i, j: (0, j)),
            pl.BlockSpec((tf, d), lambda i, j: (j, 0)),
        ],
        out_specs=pl.BlockSpec((tm, d), lambda i, j: (i, 0)),
        scratch_shapes=[pltpu.VMEM((tm, d), BF16), pltpu.VMEM((tm, d), F32)],
        compiler_params=_cparams(("parallel", "arbitrary")),
        name="ffn_half_step",
    )(x, g.reshape(1, d), wg, wu, wd)


def _moba_prompt_kernel(slope_ref, q_ref, k_ref, v_ref, o_ref, kb_scr, vb_scr, km_scr, *, nb):
    h = pl.program_id(1)
    own = pl.program_id(2)
    qc = A_BLOCK
    nbp = km_scr.shape[0]

    @pl.when(own == 0)
    def _():
        k = k_ref[0]
        kb_scr[...] = k.astype(BF16)
        vb_scr[...] = v_ref[0].astype(BF16)
        km_scr[...] = jnp.zeros_like(km_scr)
        for n in range(nb):
            km_scr[n:n + 1, :] = jnp.sum(k[n * A_BLOCK:(n + 1) * A_BLOCK], axis=0, keepdims=True) * (1.0 / A_BLOCK)

    q = q_ref[0]
    slope = slope_ref[h]
    scale = A_HEAD_DIM ** -0.5

    g = lax.dot_general(km_scr[...], q, NT_DIMS, precision=HI, preferred_element_type=F32)
    blk = lax.broadcasted_iota(jnp.int32, (nbp, qc), 0)
    gm = jnp.where(blk < own, g, -jnp.inf)
    sel_t = jnp.zeros((nbp, qc), F32)
    for n in range(nb):
        gn = gm[n:n + 1, :]
        beats = (gm > gn) | ((gm == gn) & (blk < n))
        cnt = jnp.sum(jnp.where(beats, 1.0, 0.0), axis=0, keepdims=True)
        sel_n = jnp.where(cnt < A_TOPK, 1.0, 0.0) * jnp.where(n < own, 1.0, 0.0)
        sel_t = jnp.where(blk == n, sel_n, sel_t)
    sel = jnp.concatenate([sel_t, jnp.zeros((LANES - nbp, qc), F32)], axis=0).T

    qb = (q * scale).astype(BF16)
    row = lax.broadcasted_iota(jnp.int32, (qc, A_BLOCK), 0)
    off_k = lax.broadcasted_iota(jnp.int32, (qc, A_BLOCK), 1)

    def attend(own_k):
        nk = own_k + 1
        s = lax.dot_general(qb, kb_scr[0:nk * A_BLOCK, :], NT_DIMS, preferred_element_type=F32)
        blocks = []
        for n in range(nk):
            sn = s[:, n * A_BLOCK:(n + 1) * A_BLOCK] + slope * (n * A_BLOCK + off_k[0:1, :]).astype(F32)
            if n < own_k:
                allow = jnp.broadcast_to(sel[:, n:n + 1], (qc, A_BLOCK)) > 0.5
            else:
                allow = off_k <= row
            blocks.append(jnp.where(allow, sn, NEG))
        m = blocks[0].max(axis=-1, keepdims=True)
        for n in range(1, nk):
            m = jnp.maximum(m, blocks[n].max(axis=-1, keepdims=True))
        l = jnp.zeros((qc, 1), F32)
        o = jnp.zeros((qc, A_HEAD_DIM), F32)
        for n in range(nk):
            p = jnp.exp(blocks[n] - m)
            l = l + jnp.sum(p, axis=-1, keepdims=True)
            o = o + jnp.dot(p.astype(BF16), vb_scr[n * A_BLOCK:(n + 1) * A_BLOCK, :], preferred_element_type=F32)
        o_ref[0] = o / l

    for own_k in range(nb):
        pl.when(own == own_k)(functools.partial(attend, own_k))


def moba_prompt(q, k, v, slopes):
    b, t, w = q.shape
    nh = w // A_HEAD_DIM
    nb = t // A_BLOCK
    assert t % A_BLOCK == 0 and nb >= A_TOPK and nb <= LANES and A_BLOCK % A_QCHUNK == 0
    nbp = -(-nb // SUBLANES) * SUBLANES
    return pl.pallas_call(
        functools.partial(_moba_prompt_kernel, nb=nb),
        out_shape=jax.ShapeDtypeStruct((b, t, w), F32),
        grid=(b, nh, nb),
        in_specs=[
            pl.BlockSpec(memory_space=pltpu.SMEM),
            pl.BlockSpec((1, A_BLOCK, A_HEAD_DIM), lambda i, h, c: (i, c, h)),
            pl.BlockSpec((1, t, A_HEAD_DIM), lambda i, h, c: (i, 0, h)),
            pl.BlockSpec((1, t, A_HEAD_DIM), lambda i, h, c: (i, 0, h)),
        ],
        out_specs=pl.BlockSpec((1, A_BLOCK, A_HEAD_DIM), lambda i, h, c: (i, c, h)),
        scratch_shapes=[
            pltpu.VMEM((t, A_HEAD_DIM), BF16),
            pltpu.VMEM((t, A_HEAD_DIM), BF16),
            pltpu.VMEM((nbp, A_HEAD_DIM), F32),
        ],
        compiler_params=_cparams(("parallel", "parallel", "arbitrary")),
        name="moba_prompt",
    )(slopes, q, k, v)


def _moba_sample_kernel(pt_ref, q_ref, sl_ref, mb_ref, ob_ref, kn_ref, vn_ref, *rest, nh, nblk, past):
    page_refs, (o_ref, g_acc, m_acc, l_acc, o_scr) = rest[:-5], rest[-5:]
    n = pl.program_id(1)
    scale = A_HEAD_DIM ** -0.5
    rq = SUBLANES
    q = q_ref[0]
    lane = lax.broadcasted_iota(jnp.int32, (nh * rq, LANES), 1)

    @pl.when(n == 0)
    def _():
        g_acc[...] = jnp.zeros_like(g_acc)
        m_acc[...] = jnp.zeros_like(m_acc)
        l_acc[...] = jnp.zeros_like(l_acc)

    nbs = len(page_refs) // 4
    k_refs, v_refs = page_refs[:2 * nbs], page_refs[2 * nbs:]
    blk_ids = [n * nbs + jb for jb in range(nbs)]
    kpages = [(k_refs[2 * jb][...], k_refs[2 * jb + 1][...]) for jb in range(nbs)]
    head_sum = lambda pg: jnp.sum(pg.reshape(PAGE_SIZE, nh, A_HEAD_DIM), axis=0)
    kmean = [(head_sum(ka) + head_sum(kb)) * (1.0 / A_BLOCK) for ka, kb in kpages]
    kmean_rows = [jnp.concatenate([jnp.broadcast_to(km[h:h + 1], (rq, A_HEAD_DIM)) for h in range(nh)], axis=0)
                  for km in kmean]
    g = [jnp.sum(q * kr, axis=-1, keepdims=True) for kr in kmean_rows]
    kblk = [jnp.concatenate([ka, kb], axis=0).astype(BF16) for ka, kb in kpages]
    vblk = [jnp.concatenate([v_refs[2 * jb][...], v_refs[2 * jb + 1][...]], axis=0).astype(BF16)
            for jb in range(nbs)]
    qb = q.astype(BF16)
    s = [lax.dot_general(qb, kb_, NT_DIMS, preferred_element_type=F32) * scale + mb_ref[...] for kb_ in kblk]
    m = [s_.max(axis=-1, keepdims=True) for s_ in s]
    p = [jnp.exp(s_ - m_) for s_, m_ in zip(s, m)]
    l = [jnp.sum(p_, axis=-1, keepdims=True) for p_ in p]
    o = [jnp.dot(p_.astype(BF16), vb_, preferred_element_type=F32) for p_, vb_ in zip(p, vblk)]
    g_all, m_all, l_all = g_acc[...], m_acc[...], l_acc[...]
    for jb, blk in enumerate(blk_ids):
        here = lane == blk
        g_all = jnp.where(here, g[jb], g_all)
        m_all = jnp.where(here, m[jb] - sl_ref[:, 0:1] * (past - blk * A_BLOCK).astype(F32), m_all)
        l_all = jnp.where(here, l[jb], l_all)
        o_scr[blk] = o[jb]
    g_acc[...] = g_all
    m_acc[...] = m_all
    l_acc[...] = l_all

    @pl.when(n == nblk // nbs - 1)
    def _():
        gm = jnp.where(lane < nblk, g_acc[...], -jnp.inf)
        sel = jnp.zeros(gm.shape, F32)
        for i in range(nblk):
            gi = gm[:, i:i + 1]
            beats = (gm > gi) | ((gm == gi) & (lane < i))
            cnt = jnp.sum(jnp.where(beats, 1.0, 0.0), axis=-1, keepdims=True)
            sel = jnp.where(lane == i, jnp.where(cnt < A_TOPK, 1.0, 0.0), sel)
        selb = sel > 0.5
        mblk = m_acc[...]
        s_own = lax.dot_general(qb, kn_ref[0].astype(BF16), NT_DIMS, preferred_element_type=F32) * scale + ob_ref[...]
        mx = jnp.maximum(s_own.max(axis=-1, keepdims=True),
                         jnp.where(selb, mblk, NEG).max(axis=-1, keepdims=True))
        p_own = jnp.exp(s_own - mx)
        wgt = jnp.where(selb, jnp.exp(mblk - mx), 0.0)
        den = jnp.sum(p_own, axis=-1, keepdims=True) + jnp.sum(wgt * l_acc[...], axis=-1, keepdims=True)
        num = jnp.dot(p_own.astype(BF16), vn_ref[0].astype(BF16), preferred_element_type=F32)
        for i in range(nblk):
            num = num + wgt[:, i:i + 1] * o_scr[i]
        o_ref[0] = num / den


def moba_sample(q, k_new, v_new, cache_k, cache_v, layer, page_table, slopes):
    db, tn, w = q.shape
    nh = w // A_HEAD_DIM
    n_pages = page_table.shape[1]
    past = n_pages * PAGE_SIZE
    ppb = A_BLOCK // PAGE_SIZE
    rq = SUBLANES
    assert past % A_BLOCK == 0 and ppb == 2 and tn <= rq and tn * nh <= LANES and nh == SUBLANES
    nblk = past // A_BLOCK
    assert A_TOPK <= nblk <= LANES
    rows = nh * rq
    qr = jnp.pad(q.reshape(db, tn, nh, A_HEAD_DIM).transpose(0, 2, 1, 3), ((0, 0), (0, 0), (0, rq - tn), (0, 0)))
    qr = qr.reshape(db, rows, A_HEAD_DIM)
    new_rows = lambda z: jnp.pad(z.reshape(db, tn * nh, A_HEAD_DIM), ((0, 0), (0, LANES - tn * nh), (0, 0)))
    r_h = jnp.arange(rows, dtype=jnp.int32)[:, None] // rq
    r_t = jnp.arange(rows, dtype=jnp.int32)[:, None] % rq
    slope_r = slopes[r_h[:, 0]][:, None]
    col = jnp.arange(A_BLOCK * nh, dtype=jnp.int32)[None, :]
    mb = jnp.where(col % nh == r_h, -slope_r * (r_t - col // nh).astype(F32), NEG)
    colo = jnp.arange(LANES, dtype=jnp.int32)[None, :]
    jo = colo // nh
    ob = jnp.where((colo % nh == r_h) & (jo <= r_t) & (jo < tn), -slope_r * (r_t - jo).astype(F32), NEG)
    sl = jnp.broadcast_to(slope_r, (rows, LANES))

    nbs = 2 if nblk % 2 == 0 else 1
    npg = ppb * nbs

    def page_spec(j):
        return pl.BlockSpec((None, None, PAGE_SIZE * nh, A_HEAD_DIM),
                            lambda i, n, pt: (layer, pt[i, npg * n + j], 0, 0))

    const = lambda shape: pl.BlockSpec(shape, lambda i, n, pt: (0,) * len(shape))
    per_seq = lambda r: pl.BlockSpec((1, r, A_HEAD_DIM), lambda i, n, pt: (i, 0, 0))
    pages = [page_spec(j) for j in range(npg)]
    out = pl.pallas_call(
        functools.partial(_moba_sample_kernel, nh=nh, nblk=nblk, past=past),
        out_shape=jax.ShapeDtypeStruct((db, rows, A_HEAD_DIM), F32),
        grid_spec=pltpu.PrefetchScalarGridSpec(
            num_scalar_prefetch=1,
            grid=(db, nblk // nbs),
            in_specs=[per_seq(rows), const((rows, LANES)), const((rows, A_BLOCK * nh)), const((rows, LANES)),
                      per_seq(LANES), per_seq(LANES)] + pages + pages,
            out_specs=per_seq(rows),
            scratch_shapes=[pltpu.VMEM((rows, LANES), F32)] * 3 + [pltpu.VMEM((nblk, rows, A_HEAD_DIM), F32)],
        ),
        compiler_params=_cparams(("parallel", "arbitrary")),
        name="moba_sample",
    )(page_table, qr, sl, mb, ob, new_rows(k_new), new_rows(v_new), *([cache_k] * npg), *([cache_v] * npg))
    out = out.reshape(db, nh, rq, A_HEAD_DIM)[:, :, :tn].transpose(0, 2, 1, 3)
    return out.reshape(db, tn, w)


def _head_sum_matrix():
    i = lax.broadcasted_iota(jnp.int32, (LANES, LANES), 0)
    j = lax.broadcasted_iota(jnp.int32, (LANES, LANES), 1)
    return jnp.where(lax.shift_right_logical(i, 6) == lax.shift_right_logical(j, 6), 1.0, 0.0).astype(F32)


def _rwkv_prep_kernel(u_ref, pv_ref, s0_ref, mu_ref, w0_ref, w2_ref, a0_ref, a2_ref, g2_ref, kk_ref, ka_ref,
                      r_o, lw_o, k_o, v_o, kk_o, a_o, g_o, *, rw):
    t = pl.program_id(1)
    u = u_ref[0]
    tt = u.shape[0]
    prev_row = jnp.where(t == 0, s0_ref[0], pv_ref[0, SUBLANES - 1:SUBLANES, :])
    row = lax.broadcasted_iota(jnp.int32, u.shape, 0)
    prev = jnp.where(row == 0, prev_row, pltpu.roll(u, 1, axis=0))
    xs = u + (prev - u) * mu_ref[...]
    r = xs[:, 0:rw]
    k = xs[:, rw:2 * rw]
    v = xs[:, 2 * rw:3 * rw]
    lora_wa = xs[:, 3 * rw:3 * rw + LANES]
    gd = xs[:, 3 * rw + LANES:3 * rw + 2 * LANES]
    z = w0_ref[...] + jnp.dot(jnp.tanh(lora_wa), w2_ref[...], precision=HI, preferred_element_type=F32)
    nz = -z
    softplus = jnp.maximum(nz, 0.0) + jnp.log1p(jnp.exp(-jnp.abs(nz)))
    w_log = -softplus - 0.5
    lw_o[0] = -jnp.exp(w_log)
    a = jax.nn.sigmoid(a0_ref[...] + jnp.dot(lora_wa, a2_ref[...], precision=HI, preferred_element_type=F32))
    g_o[0] = jnp.dot(jax.nn.sigmoid(gd), g2_ref[...], precision=HI, preferred_element_type=F32)
    kk = k * kk_ref[...]
    seg = _head_sum_matrix()
    for p in range(rw // LANES):
        ps = slice(p * LANES, (p + 1) * LANES)
        kkp = kk[:, ps]
        ss = jnp.dot(kkp * kkp, seg, precision=HI, preferred_element_type=F32)
        kk_o[0, :, ps] = kkp / jnp.maximum(jnp.sqrt(ss), 1e-12)
    r_o[0] = r
    k_o[0] = k * (1.0 + (a - 1.0) * ka_ref[...])
    v_o[0] = v
    a_o[0] = a


def rwkv_prep(ur, shift0, prm):
    b, t, _ = ur.shape
    cols = prm['rw_mu'].shape[-1]
    rw = prm['rw_w0'].shape[-1]
    tt = _row_tile(t, 256)
    nlora = prm['rw_w2'].shape[0]
    assert 2 * nlora == LANES and prm['rw_g2'].shape[0] == LANES and cols == 3 * rw + 2 * LANES
    w2p = jnp.concatenate([prm['rw_w2'], jnp.zeros_like(prm['rw_w2'])], axis=0)
    a2p = jnp.concatenate([jnp.zeros_like(prm['rw_a2']), prm['rw_a2']], axis=0)
    row = lambda z: z.reshape(1, -1)
    full = lambda shape: pl.BlockSpec(shape, lambda i, j: (0,) * len(shape))
    outs = pl.pallas_call(
        functools.partial(_rwkv_prep_kernel, rw=rw),
        out_shape=[jax.ShapeDtypeStruct((b, t, rw), F32)] * 7,
        grid=(b, t // tt),
        in_specs=[
            pl.BlockSpec((1, tt, cols), lambda i, j: (i, j, 0)),
            pl.BlockSpec((1, SUBLANES, cols), lambda i, j: (i, jnp.maximum(j * (tt // SUBLANES) - 1, 0), 0)),
            pl.BlockSpec((1, 1, cols), lambda i, j: (i, 0, 0)),
            full((1, cols)), full((1, rw)), full((LANES, rw)), full((1, rw)), full((LANES, rw)),
            full((LANES, rw)), full((1, rw)), full((1, rw)),
        ],
        out_specs=[pl.BlockSpec((1, tt, rw), lambda i, j: (i, j, 0))] * 7,
        compiler_params=_cparams(("parallel", "parallel")),
        name="rwkv_prep",
    )(ur, ur, shift0.reshape(b, 1, cols), row(prm['rw_mu']), row(prm['rw_w0']), w2p, row(prm['rw_a0']), a2p,
      prm['rw_g2'], row(prm['rw_kk']), row(prm['rw_ka']))
    return outs


def _pair_masks():
    i = lax.broadcasted_iota(jnp.int32, (LANES, LANES), 0)
    j = lax.broadcasted_iota(jnp.int32, (LANES, LANES), 1)
    same = lax.shift_right_logical(i, 6) == lax.shift_right_logical(j, 6)
    return i, j, same


def _split(x):
    hi = x.astype(BF16)
    return hi, (x - hi.astype(F32)).astype(BF16)


def _mm3(a, b):
    d = lambda x, y: jnp.dot(x, y, preferred_element_type=F32)
    return d(a[0], b[0]) + d(a[0], b[1]) + d(a[1], b[0])


def _mm3_nt(a, b):
    d = lambda x, y: lax.dot_general(x, y, NT_DIMS, preferred_element_type=F32)
    return d(a[0], b[0]) + d(a[0], b[1]) + d(a[1], b[0])


def _each(f, *lists):
    return [f(*xs) for xs in zip(*lists)]


def _rwkv_chunk_factors(r, lw, k, v, kk, a):
    c = R_CHUNK
    row = lax.broadcasted_iota(jnp.int32, (c, LANES), 0)
    cum = lw
    s = 1
    while s < c:
        cum = _each(lambda z: z + jnp.where(row >= s, pltpu.roll(z, s, axis=0), 0.0), cum)
        s *= 2
    cl = _each(lambda z: z[c - 1:c, :], cum)
    beta = _each(lambda x, y: x * y, kk, a)
    kap_t = _each(lambda x, cu, l: x * jnp.exp(cu - l), kk, cum, lw)
    r_t = _each(lambda x, cu: x * jnp.exp(cu), r, cum)
    e_inv = _each(lambda cu: jnp.exp(-cu), cum)
    b_t = _each(lambda x, e: x * e, beta, e_inv)
    k_t = _each(lambda x, e: x * e, k, e_inv)
    e_end = _each(lambda l, cu: jnp.exp(l - cu), cl, cum)
    b_h = _each(lambda x, e: x * e, beta, e_end)
    k_h = _each(lambda x, e: x * e, k, e_end)

    lo = lax.broadcasted_iota(jnp.int32, (c, LANES), 1) < R_HEAD_DIM
    stack = lambda x: jnp.concatenate([jnp.where(lo, x, 0.0), jnp.where(lo, 0.0, x)], axis=0)
    dup = lambda x: jnp.concatenate([x, x], axis=0)
    split_of = lambda f: (lambda x: _split(f(x)))
    ident = lambda x: x

    i, j, same = _pair_masks()
    strict = same & (j < i)
    incl = same & (j <= i)
    kap_s, r_s, v_s = _each(stack, kap_t), _each(stack, r_t), _each(stack, v)
    kap_p, r_p, v_p = _each(_split, kap_s), _each(_split, r_s), _each(_split, v_s)
    b_d, k_d = _each(split_of(dup), b_t), _each(split_of(dup), k_t)
    a_ab = _each(lambda x, y: jnp.where(strict, _mm3_nt(x, y), 0.0), kap_p, b_d)
    a_ak = _each(lambda x, y: jnp.where(strict, _mm3_nt(x, y), 0.0), kap_p, k_d)
    a_rb = _each(lambda x, y: jnp.where(incl, _mm3_nt(x, y), 0.0), r_p, b_d)
    a_rk = _each(lambda x, y: jnp.where(incl, _mm3_nt(x, y), 0.0), r_p, k_d)

    x = [jnp.where(i == j, 1.0, 0.0).astype(F32)] * len(r)
    s = 1
    while s < c:
        low = (lax.shift_right_logical(i, s.bit_length()) == lax.shift_right_logical(j, s.bit_length())) \
            & ((i & (2 * s - 1)) >= s) & ((j & (2 * s - 1)) < s)
        xp = _each(_split, x)
        ms = _each(lambda z: _split(jnp.where(low, z, 0.0)), a_ab)
        t1 = _each(split_of(ident), _each(_mm3, xp, ms))
        x = _each(lambda z, u, w_: z - _mm3(u, w_), x, t1, xp)
        s *= 2

    aakv = _each(_mm3, _each(_split, a_ak), v_p)
    arkv = _each(_mm3, _each(_split, a_rk), v_p)
    bh_p = _each(split_of(stack), b_h)
    vtk = _each(_mm3, _each(lambda z: _split(z.T), v_s), _each(split_of(stack), k_h))
    rhs = _each(lambda z, u: _split(jnp.concatenate([z, u], axis=1)), kap_s, aakv)
    wu = _each(lambda z, u: -_mm3(z, u), _each(_split, x), rhs)
    w = _each(lambda z: z[:, :LANES], wu)
    upre = _each(lambda z: z[:, LANES:], wu)
    arb_p = _each(_split, a_rb)
    r2 = _each(lambda z, u, w_: z + _mm3(u, _split(w_)), r_s, arb_p, w)
    ypre = _each(lambda z, u, w_: z + _mm3(u, _split(w_)), arkv, arb_p, upre)
    g = _each(lambda l, w_, u: jnp.where(i == j, jnp.exp(l), 0.0) + _mm3(_split(w_.T), u), cl, w, bh_p)
    spre = _each(lambda z, w_, u: z + _mm3(_split(w_.T), u), vtk, upre, bh_p)
    return list(zip(r2, ypre, g, spre))


def _rwkv_core_kernel(r_ref, lw_ref, k_ref, v_ref, kk_ref, a_ref, s0_ref, y_o, st_o, s_scr, *, nch, npair):
    cidx = pl.program_id(1)

    @pl.when(cidx == 0)
    def _():
        s_scr[...] = s0_ref[0]

    c = R_CHUNK
    parts = lambda ref: [ref[0, ch * c:(ch + 1) * c, p * LANES:(p + 1) * LANES]
                         for ch in range(nch) for p in range(npair)]
    factors = _rwkv_chunk_factors(parts(r_ref), parts(lw_ref), parts(k_ref), parts(v_ref), parts(kk_ref),
                                  parts(a_ref))
    s = [s_scr[p] for p in range(npair)]
    for ch in range(nch):
        for p in range(npair):
            r2, ypre, g, spre = factors[ch * npair + p]
            sp = _split(s[p])
            ys = _mm3_nt(_split(r2), sp) + ypre
            y_o[0, ch * c:(ch + 1) * c, p * LANES:(p + 1) * LANES] = ys[:c] + ys[c:]
            s[p] = _mm3(sp, _split(g)) + spre
    for p in range(npair):
        s_scr[p] = s[p]
        st_o[0, p] = s[p]


def rwkv_scan(r, lw, k, v, kk, a, wkv0):
    b, t, rw = r.shape
    npair = rw // LANES
    assert t % R_CHUNK == 0 and 2 * R_HEAD_DIM == LANES
    nch = 2 if t % (2 * R_CHUNK) == 0 else 1
    tt = nch * R_CHUNK
    w4 = wkv0.reshape(b, npair, 2, R_HEAD_DIM, R_HEAD_DIM)
    zero = jnp.zeros_like(w4[:, :, 0])
    s0 = jnp.concatenate([jnp.concatenate([w4[:, :, 0], zero], axis=-1),
                          jnp.concatenate([zero, w4[:, :, 1]], axis=-1)], axis=-2)
    act = pl.BlockSpec((1, tt, rw), lambda i, c: (i, c, 0))
    st_spec = pl.BlockSpec((1, npair, LANES, LANES), lambda i, c: (i, 0, 0, 0))
    y, st = pl.pallas_call(
        functools.partial(_rwkv_core_kernel, nch=nch, npair=npair),
        out_shape=[jax.ShapeDtypeStruct((b, t, rw), F32), jax.ShapeDtypeStruct((b, npair, LANES, LANES), F32)],
        grid=(b, t // tt),
        in_specs=[act] * 6 + [st_spec],
        out_specs=[act, st_spec],
        scratch_shapes=[pltpu.VMEM((npair, LANES, LANES), F32)],
        compiler_params=_cparams(("parallel", "arbitrary")),
        name="rwkv_core",
    )(r, lw, k, v, kk, a, s0)
    wkv = jnp.stack([st[:, :, :R_HEAD_DIM, :R_HEAD_DIM], st[:, :, R_HEAD_DIM:, R_HEAD_DIM:]], axis=2)
    return y, wkv.reshape(wkv0.shape)


def _rwkv_post_kernel(y_ref, r_ref, k_ref, v_ref, g_ref, rk_ref, lw_ref, lb_ref, o_ref):
    seg = _head_sum_matrix()
    inv_n = 1.0 / R_HEAD_DIM
    for p in range(y_ref.shape[2] // LANES):
        ps = slice(p * LANES, (p + 1) * LANES)
        y = y_ref[0, :, ps]
        hsum = lambda z: jnp.dot(z, seg, precision=HI, preferred_element_type=F32)
        mean = hsum(y) * inv_n
        d = y - mean
        var = hsum(d * d) * inv_n
        yn = d * lax.rsqrt(var + R_GN_EPS) * lw_ref[:, ps] + lb_ref[:, ps]
        bonus = hsum(r_ref[0, :, ps] * k_ref[0, :, ps] * rk_ref[:, ps]) * v_ref[0, :, ps]
        o_ref[0, :, ps] = (yn + bonus) * g_ref[0, :, ps]


def rwkv_post(y, r, k, v, g, prm):
    b, t, rw = y.shape
    tt = _row_tile(t, 256)
    act = pl.BlockSpec((1, tt, rw), lambda i, j: (i, j, 0))
    par = pl.BlockSpec((1, rw), lambda i, j: (0, 0))
    return pl.pallas_call(
        _rwkv_post_kernel,
        out_shape=jax.ShapeDtypeStruct((b, t, rw), F32),
        grid=(b, t // tt),
        in_specs=[act] * 5 + [par] * 3,
        out_specs=act,
        compiler_params=_cparams(("parallel", "parallel")),
        name="rwkv_post",
    )(y, r, k, v, g, prm['rw_rk'].reshape(1, rw), prm['rw_lnx_w'].reshape(1, rw), prm['rw_lnx_b'].reshape(1, rw))


def rwkv_mix(ur, shift0, wkv0, prm):
    b, t, _ = ur.shape
    tp = -(-t // SUBLANES) * SUBLANES
    urp = jnp.pad(ur, ((0, 0), (0, tp - t), (0, 0)))
    r, lw, k, v, kk, a, g = rwkv_prep(urp, shift0, prm)
    tc = -(-t // R_CHUNK) * R_CHUNK
    fit = lambda z: jnp.pad(z[:, :t], ((0, 0), (0, tc - t), (0, 0)))
    r, lw, k, v, kk, a, g = (fit(z) for z in (r, lw, k, v, kk, a, g))
    y, wkv = rwkv_scan(r, lw, k, v, kk, a, wkv0)
    out = rwkv_post(y, r, k, v, g, prm)
    return out[:, :t], wkv, ur[:, -1, :prm['rw_mu'].shape[-1]]


def _pool_kernel(u_ref, pv_ref, p0_ref, w_ref, sc_ref, o_ref, *, pos0):
    t = pl.program_id(1)
    cur = u_ref[0]
    tt = cur.shape[0]
    ext = jnp.concatenate([jnp.where(t == 0, p0_ref[0], pv_ref[0]), cur], axis=0)
    sums = []
    s = ext
    w = 1
    while w < POOL_MAX:
        s = s + pltpu.roll(s, w, axis=0)
        w *= 2
        sums.append(s)
    pos = pos0 + t * tt + lax.broadcasted_iota(jnp.int32, (tt, 1), 0)
    for gi, w in enumerate(POOL_WINDOWS):
        gs = slice(gi * LANES, (gi + 1) * LANES)
        win = sums[w.bit_length() - 2][POOL_MAX:, gs]
        cnt = jnp.minimum(pos + 1, w).astype(F32)
        m = win / cnt - cur[:, gs]
        z = jnp.dot(m.astype(BF16), w_ref[gi], preferred_element_type=F32)
        o_ref[0, :, gs] = z * sc_ref[:, gs]


def pool_mix(uc, pool0, pos0, pool_w, pool_scale):
    b, t, width = uc.shape
    assert width == len(POOL_WINDOWS) * LANES and pool_w.shape[1] == LANES
    tp = -(-t // SUBLANES) * SUBLANES
    ucp = jnp.pad(uc, ((0, 0), (0, tp - t), (0, 0)))
    tt = _row_tile(tp, 256)
    assert tt % POOL_MAX == 0 or tp == tt
    p0 = jnp.pad(pool0, ((0, 0), (1, 0), (0, 0)))
    nprev = tt // POOL_MAX if tt % POOL_MAX == 0 else 0
    ext = jnp.concatenate([pool0, uc], axis=1)
    if nprev:
        pv, pv_spec = ucp, pl.BlockSpec((1, POOL_MAX, width), lambda i, j: (i, jnp.maximum(j * nprev - 1, 0), 0))
    else:
        pv, pv_spec = p0, pl.BlockSpec((1, POOL_MAX, width), lambda i, j: (i, 0, 0))
    z = pl.pallas_call(
        functools.partial(_pool_kernel, pos0=pos0),
        out_shape=jax.ShapeDtypeStruct((b, tp, width), F32),
        grid=(b, tp // tt),
        in_specs=[
            pl.BlockSpec((1, tt, width), lambda i, j: (i, j, 0)),
            pv_spec,
            pl.BlockSpec((1, POOL_MAX, width), lambda i, j: (i, 0, 0)),
            pl.BlockSpec(pool_w.shape, lambda i, j: (0, 0, 0)),
            pl.BlockSpec((1, width), lambda i, j: (0, 0)),
        ],
        out_specs=pl.BlockSpec((1, tt, width), lambda i, j: (i, j, 0)),
        compiler_params=_cparams(("parallel", "parallel")),
        name="pool_mix",
    )(ucp, pv, p0, pool_w.astype(BF16), pool_scale.reshape(1, width))
    return z[:, :t], ext[:, -(POOL_MAX - 1):]


def _mix_out_kernel(x_ref, ya_ref, yr_ref, yc_ref, wa_ref, wr_ref, wc_ref, o_ref):
    acc = jnp.dot(ya_ref[...].astype(BF16), wa_ref[...], preferred_element_type=F32)
    acc += jnp.dot(yr_ref[...].astype(BF16), wr_ref[...], preferred_element_type=F32)
    acc += jnp.dot(yc_ref[...].astype(BF16), wc_ref[...], preferred_element_type=F32)
    o_ref[...] = x_ref[...] + acc


def mix_out(x, ya, yr, yc, w_out):
    n, d = x.shape
    wa_, wr_, wc_ = ya.shape[1], yr.shape[1], yc.shape[1]
    tm = _row_tile(n, 512)
    tn = 1024
    act = lambda wd: pl.BlockSpec((tm, wd), lambda i, j: (i, 0))
    wsp = lambda wd: pl.BlockSpec((wd, tn), lambda i, j: (0, j))
    return pl.pallas_call(
        _mix_out_kernel,
        out_shape=jax.ShapeDtypeStruct((n, d), F32),
        grid=(n // tm, d // tn),
        in_specs=[pl.BlockSpec((tm, tn), lambda i, j: (i, j)), act(wa_), act(wr_), act(wc_),
                  wsp(wa_), wsp(wr_), wsp(wc_)],
        out_specs=pl.BlockSpec((tm, tn), lambda i, j: (i, j)),
        compiler_params=_cparams(("parallel", "arbitrary")),
        name="mix_out",
    )(x, ya, yr, yc, w_out[:wa_], w_out[wa_:wa_ + wr_], w_out[wa_ + wr_:])


def _cross_kernel(x_ref, g_ref, wq_ref, mk_ref, mv_ref, wo_ref, o_ref, *, nh):
    x = x_ref[0]
    h = _rms(x, g_ref[...]).astype(BF16)
    q = jnp.dot(h, wq_ref[...], preferred_element_type=F32)
    scale = M_HEAD_DIM ** -0.5
    outs = []
    for hh in range(nh):
        hs = slice(hh * M_HEAD_DIM, (hh + 1) * M_HEAD_DIM)
        s = lax.dot_general(q[:, hs].astype(BF16), mk_ref[0, :, hs].astype(BF16), NT_DIMS,
                            preferred_element_type=F32) * scale
        p = jnp.exp(s - s.max(axis=-1, keepdims=True))
        l = jnp.sum(p, axis=-1, keepdims=True)
        outs.append(jnp.dot(p.astype(BF16), mv_ref[0, :, hs].astype(BF16), preferred_element_type=F32) / l)
    o = jnp.concatenate(outs, axis=-1).astype(BF16)
    o_ref[0] = x + jnp.dot(o, wo_ref[...], preferred_element_type=F32)


def cross_attend(x, g, wq, mk, mv, wo):
    b, t0, d = x.shape
    t = -(-t0 // SUBLANES) * SUBLANES
    x = jnp.pad(x, ((0, 0), (0, t - t0), (0, 0)))
    mw = wq.shape[1]
    nm = mk.shape[1]
    tm = _row_tile(t, 512)
    out = pl.pallas_call(
        functools.partial(_cross_kernel, nh=mw // M_HEAD_DIM),
        out_shape=jax.ShapeDtypeStruct((b, t, d), F32),
        grid=(b, t // tm),
        in_specs=[
            pl.BlockSpec((1, tm, d), lambda i, j: (i, j, 0)),
            pl.BlockSpec((1, d), lambda i, j: (0, 0)),
            pl.BlockSpec((d, mw), lambda i, j: (0, 0)),
            pl.BlockSpec((1, nm, mw), lambda i, j: (i, 0, 0)),
            pl.BlockSpec((1, nm, mw), lambda i, j: (i, 0, 0)),
            pl.BlockSpec((mw, d), lambda i, j: (0, 0)),
        ],
        out_specs=pl.BlockSpec((1, tm, d), lambda i, j: (i, j, 0)),
        compiler_params=_cparams(("parallel", "parallel")),
        name="cross_attend",
    )(x, g.reshape(1, d), wq, mk, mv, wo)
    return out[:, :t0]


def _layer(x, prm, moba_fn, pos0, shift0, wkv0, pool0, mem_k, mem_v):
    b, t, d = x.shape
    n = b * t
    x2 = ffn_half_step(x.reshape(n, d), prm['norm_ffn1'], prm['ffn1_gate'], prm['ffn1_up'], prm['ffn1_down'])
    aw, rc, cw = prm['a_width'], prm['r_cols'], prm['c_width']
    qa, ka, va, ur, uc = (z.reshape(b, t, -1) for z in
                          in_proj(x2, prm['norm_mix'], prm['w_in'], (aw, aw, aw, rc, cw)))
    ya = moba_fn(qa, ka, va)
    yr, wkv, shift = rwkv_mix(ur, shift0, wkv0, prm)
    yc, pool_buf = pool_mix(uc, pool0, pos0, prm['pool_w'], prm['pool_scale'])
    x3 = mix_out(x2, ya.reshape(n, aw), yr.reshape(n, -1), yc.reshape(n, cw), prm['w_out'])
    x4 = cross_attend(x3.reshape(b, t, d), prm['norm_cross'], prm['mem_wq'], mem_k, mem_v, prm['mem_wo'])
    x5 = ffn_half_step(x4.reshape(n, d), prm['norm_ffn2'], prm['ffn2_gate'], prm['ffn2_up'], prm['ffn2_down'])
    return x5.reshape(b, t, d), ka, va, wkv, shift, pool_buf


def kernel(x_prompt, x_sample, cache_k, cache_v, cache_mem_k, cache_mem_v, state_wkv, state_shift, state_pool, page_table, mem_prompt, norm_ffn1, ffn1_gate, ffn1_up, ffn1_down, norm_mix, w_in, w_out, rw_mu, rw_w0, rw_w2, rw_a0, rw_a2, rw_g2, rw_kk, rw_ka, rw_rk, rw_lnx_w, rw_lnx_b, pool_w, pool_scale, norm_cross, norm_mem, mem_wq, mem_wk, mem_wv, mem_wo, norm_ffn2, ffn2_gate, ffn2_up, ffn2_down, norm_final):
    depth = w_in.shape[0]
    bp, tp, d = x_prompt.shape
    db, ts, _ = x_sample.shape
    n_heads, hd = cache_k.shape[3], cache_k.shape[4]
    aw = n_heads * hd
    r_heads, rn = rw_rk.shape[1], rw_rk.shape[2]
    r_cols = rw_mu.shape[1]
    c_width = pool_scale.shape[1]
    nm, m_heads, mhd = cache_mem_k.shape[2:]
    mw = m_heads * mhd
    assert hd == A_HEAD_DIM and rn == R_HEAD_DIM and mhd == M_HEAD_DIM
    past = page_table.shape[1] * PAGE_SIZE
    slopes = jnp.exp2(-8.0 * jnp.arange(1, n_heads + 1, dtype=F32) / n_heads)
    ck = cache_k.reshape(depth, cache_k.shape[1], PAGE_SIZE * n_heads, hd)
    cv = cache_v.reshape(depth, cache_v.shape[1], PAGE_SIZE * n_heads, hd)
    bf = lambda z: z.astype(BF16)
    bounds = (0, aw, 2 * aw, 3 * aw, 3 * aw + r_cols, w_in.shape[2])
    groups = [bf(w_in[:, :, lo:hi]) for lo, hi in zip(bounds[:-1], bounds[1:])]
    w_in_p = jnp.concatenate([jnp.pad(gw, ((0, 0), (0, 0), (0, -gw.shape[2] % IN_TILE))) for gw in groups], axis=2)
    big = dict(ffn1_gate=bf(ffn1_gate), ffn1_up=bf(ffn1_up), ffn1_down=bf(ffn1_down), w_in=w_in_p,
               w_out=bf(w_out), mem_wq=bf(mem_wq), mem_wk=bf(mem_wk), mem_wv=bf(mem_wv), mem_wo=bf(mem_wo),
               ffn2_gate=bf(ffn2_gate), ffn2_up=bf(ffn2_up), ffn2_down=bf(ffn2_down))
    small = dict(norm_ffn1=norm_ffn1, norm_mix=norm_mix, rw_mu=rw_mu, rw_w0=rw_w0, rw_w2=rw_w2, rw_a0=rw_a0,
                 rw_a2=rw_a2, rw_g2=rw_g2, rw_kk=rw_kk, rw_ka=rw_ka, rw_rk=rw_rk.reshape(depth, -1),
                 rw_lnx_w=rw_lnx_w, rw_lnx_b=rw_lnx_b, pool_w=pool_w, pool_scale=pool_scale,
                 norm_cross=norm_cross, norm_ffn2=norm_ffn2)

    shift0 = jnp.zeros((bp, r_cols), F32)
    wkv0 = jnp.zeros((bp, r_heads, rn, rn), F32)
    pool0 = jnp.zeros((bp, POOL_MAX - 1, c_width), F32)
    xp, xs = x_prompt, x_sample
    outs = [[] for _ in range(12)]
    for l in range(depth):
        prm = {k: v[l] for k, v in big.items()}
        prm.update({k: v[l] for k, v in small.items()})
        prm.update(a_width=aw, r_cols=r_cols, c_width=c_width)
        hm = rmsnorm(mem_prompt.reshape(bp * nm, d), norm_mem[l], BF16)
        mk = matmul(hm, prm['mem_wk'], mw).reshape(bp, nm, mw)
        mv = matmul(hm, prm['mem_wv'], mw).reshape(bp, nm, mw)
        moba_p = functools.partial(moba_prompt, slopes=slopes)
        xp, k_, v_, w_, sh_, pl_ = _layer(xp, prm, moba_p, 0, shift0, wkv0, pool0, mk, mv)
        res_p = (k_.reshape(bp, tp, n_heads, hd), v_.reshape(bp, tp, n_heads, hd), w_, sh_, pl_,
                 mk.reshape(bp, nm, m_heads, mhd), mv.reshape(bp, nm, m_heads, mhd))
        moba_s = functools.partial(moba_sample, cache_k=ck, cache_v=cv, layer=l, page_table=page_table,
                                   slopes=slopes)
        xs, k_, v_, w_, sh_, pl_ = _layer(xs, prm, moba_s, past, state_shift[l], state_wkv[l], state_pool[l],
                                           cache_mem_k[l].reshape(db, nm, mw), cache_mem_v[l].reshape(db, nm, mw))
        res_s = (k_.reshape(db, ts, n_heads, hd), v_.reshape(db, ts, n_heads, hd), w_, sh_, pl_)
        for i, z in enumerate(res_p + res_s):
            outs[i].append(z)
    y_prompt = rmsnorm(xp.reshape(bp * tp, d), norm_final, F32).reshape(bp, tp, d)
    y_sample = rmsnorm(xs.reshape(db * ts, d), norm_final, F32).reshape(db, ts, d)
    return (y_prompt, y_sample) + tuple(jnp.stack(o) for o in outs)
```

```python
import functools

import jax
import jax.numpy as jnp
from jax import lax
from jax.experimental import pallas as pl
from jax.experimental.pallas import tpu as pltpu

F32 = jnp.float32
BF16 = jnp.bfloat16
HI = lax.Precision.HIGHEST

RMS_EPS = 1e-6
LANES = 128
SUBLANES = 8
VMEM_LIMIT = 48 * 1024 * 1024

PAGE_SIZE = 128
A_HEAD_DIM = 128
A_BLOCK = 256
A_TOPK = 3
A_QCHUNK = 128
R_HEAD_DIM = 64
R_CHUNK = 64
R_GN_EPS = 64e-5
POOL_WINDOWS = (2, 4, 8, 16)
POOL_MAX = 16
M_HEAD_DIM = 128
NEG = -1e30

NT_DIMS = (((1,), (1,)), ((), ()))
TN_DIMS = (((0,), (0,)), ((), ()))


def _cparams(sem):
    return pltpu.CompilerParams(dimension_semantics=sem, vmem_limit_bytes=VMEM_LIMIT)


def _row_tile(n, pref):
    return pref if n % pref == 0 else n


def _rms(x, g):
    ms = jnp.mean(x * x, axis=-1, keepdims=True)
    return x * lax.rsqrt(ms + RMS_EPS) * g


def _rms_kernel(x_ref, g_ref, o_ref):
    o_ref[...] = _rms(x_ref[...], g_ref[...]).astype(o_ref.dtype)


def rmsnorm(x, g, out_dtype):
    n, d = x.shape
    tm = _row_tile(n, 512)
    return pl.pallas_call(
        _rms_kernel,
        out_shape=jax.ShapeDtypeStruct((n, d), out_dtype),
        grid=(n // tm,),
        in_specs=[pl.BlockSpec((tm, d), lambda i: (i, 0)), pl.BlockSpec((1, d), lambda i: (0, 0))],
        out_specs=pl.BlockSpec((tm, d), lambda i: (i, 0)),
        compiler_params=_cparams(("parallel",)),
        name="rmsnorm",
    )(x, g.reshape(1, d))


def _matmul_kernel(a_ref, w_ref, o_ref):
    o_ref[...] = jnp.dot(a_ref[...], w_ref[...], preferred_element_type=F32)


def matmul(a, w, tn):
    n, k = a.shape
    m = w.shape[1]
    tm = _row_tile(n, 512)
    return pl.pallas_call(
        _matmul_kernel,
        out_shape=jax.ShapeDtypeStruct((n, m), F32),
        grid=(n // tm, m // tn),
        in_specs=[pl.BlockSpec((tm, k), lambda i, j: (i, 0)), pl.BlockSpec((k, tn), lambda i, j: (0, j))],
        out_specs=pl.BlockSpec((tm, tn), lambda i, j: (i, j)),
        compiler_params=_cparams(("parallel", "arbitrary")),
        name="matmul",
    )(a, w)


IN_TILE = 512


def _in_proj_kernel(x_ref, g_ref, w_ref, *rest, first_tile):
    out_refs, h_scr = rest[:-1], rest[-1]
    j = pl.program_id(1)

    @pl.when(j == 0)
    def _():
        h_scr[...] = _rms(x_ref[...], g_ref[...]).astype(BF16)

    for o_ref, lo, hi in zip(out_refs, first_tile[:-1], first_tile[1:]):
        @pl.when((j >= lo) & (j < hi))
        def _(o_ref=o_ref):
            o_ref[...] = jnp.dot(h_scr[...], w_ref[...], preferred_element_type=F32)


def in_proj(x, g, w, widths):
    n, d = x.shape
    tm = _row_tile(n, 512)
    tiles = [-(-wd // IN_TILE) for wd in widths]
    first_tile = [0]
    for nt in tiles:
        first_tile.append(first_tile[-1] + nt)
    assert w.shape[1] == first_tile[-1] * IN_TILE

    def out_spec(lo, nt):
        return pl.BlockSpec((tm, IN_TILE), lambda i, j: (i, jnp.clip(j - lo, 0, nt - 1)))

    return pl.pallas_call(
        functools.partial(_in_proj_kernel, first_tile=tuple(first_tile)),
        out_shape=[jax.ShapeDtypeStruct((n, nt * IN_TILE), F32) for nt in tiles],
        grid=(n // tm, first_tile[-1]),
        in_specs=[pl.BlockSpec((tm, d), lambda i, j: (i, 0)), pl.BlockSpec((1, d), lambda i, j: (0, 0)),
                  pl.BlockSpec((d, IN_TILE), lambda i, j: (0, j))],
        out_specs=[out_spec(lo, nt) for lo, nt in zip(first_tile[:-1], tiles)],
        scratch_shapes=[pltpu.VMEM((tm, d), BF16)],
        compiler_params=_cparams(("parallel", "arbitrary")),
        name="in_proj",
    )(x, g.reshape(1, d), w)


def _ffn_kernel(x_ref, g_ref, wg_ref, wu_ref, wd_ref, o_ref, h_scr, acc_scr):
    j = pl.program_id(1)

    @pl.when(j == 0)
    def _():
        h_scr[...] = _rms(x_ref[...], g_ref[...]).astype(BF16)
        acc_scr[...] = jnp.zeros_like(acc_scr)

    h = h_scr[...]
    gate = jnp.dot(h, wg_ref[...], preferred_element_type=F32)
    up = jnp.dot(h, wu_ref[...], preferred_element_type=F32)
    act = (gate * jax.nn.sigmoid(gate) * up).astype(BF16)
    acc_scr[...] += jnp.dot(act, wd_ref[...], preferred_element_type=F32)

    @pl.when(j == pl.num_programs(1) - 1)
    def _():
        o_ref[...] = x_ref[...] + 0.5 * acc_scr[...]


def ffn_half_step(x, g, wg, wu, wd, tf=512):
    n, d = x.shape
    f = wg.shape[1]
    tm = _row_tile(n, 512)
    return pl.pallas_call(
        _ffn_kernel,
        out_shape=jax.ShapeDtypeStruct((n, d), F32),
        grid=(n // tm, f // tf),
        in_specs=[
            pl.BlockSpec((tm, d), lambda i, j: (i, 0)),
            pl.BlockSpec((1, d), lambda i, j: (0, 0)),
            pl.BlockSpec((d, tf), lambda i, j: (0, j)),
            pl.BlockSpec((d, tf), lambda i, j: (0, j)),
            pl.BlockSpec((tf, d), lambda i, j: (j, 0)),
        ],
        out_specs=pl.BlockSpec((tm, d), lambda i, j: (i, 0)),
        scratch_shapes=[pltpu.VMEM((tm, d), BF16), pltpu.VMEM((tm, d), F32)],
        compiler_params=_cparams(("parallel", "arbitrary")),
        name="ffn_half_step",
    )(x, g.reshape(1, d), wg, wu, wd)


def _moba_prompt_kernel(slope_ref, q_ref, k_ref, v_ref, o_ref, kb_scr, vb_scr, km_scr, *, nb):
    h = pl.program_id(1)
    own = pl.program_id(2)
    qc = A_BLOCK
    hps, nbp = km_scr.shape[0], km_scr.shape[1]
    heads = range(hps)
    hsl = lambda i: slice(i * A_HEAD_DIM, (i + 1) * A_HEAD_DIM)

    @pl.when(own == 0)
    def _():
        k = k_ref[0]
        kb_scr[...] = k.astype(BF16)
        vb_scr[...] = v_ref[0].astype(BF16)
        km_scr[...] = jnp.zeros_like(km_scr)
        for i in heads:
            for n in range(nb):
                km_scr[i, n:n + 1, :] = jnp.sum(k[n * A_BLOCK:(n + 1) * A_BLOCK, hsl(i)], axis=0,
                                                keepdims=True) * (1.0 / A_BLOCK)

    q = [q_ref[0, :, hsl(i)] for i in heads]
    slope = [slope_ref[h * hps + i] for i in heads]
    scale = A_HEAD_DIM ** -0.5

    g = [lax.dot_general(km_scr[i], q[i], NT_DIMS, precision=HI, preferred_element_type=F32) for i in heads]
    blk = lax.broadcasted_iota(jnp.int32, (nbp, qc), 0)
    gm = [jnp.where(blk < own, g_, -jnp.inf) for g_ in g]
    sel_t = [jnp.zeros((nbp, qc), F32) for _ in heads]
    for n in range(nb):
        for i in heads:
            gn = gm[i][n:n + 1, :]
            beats = (gm[i] > gn) | ((gm[i] == gn) & (blk < n))
            cnt = jnp.sum(jnp.where(beats, 1.0, 0.0), axis=0, keepdims=True)
            sel_n = jnp.where(cnt < A_TOPK, 1.0, 0.0) * jnp.where(n < own, 1.0, 0.0)
            sel_t[i] = jnp.where(blk == n, sel_n, sel_t[i])
    sel_b = [st.astype(BF16) for st in sel_t]

    qb = [(q_ * scale).astype(BF16) for q_ in q]
    row = lax.broadcasted_iota(jnp.int32, (qc, A_BLOCK), 0)
    off_k = lax.broadcasted_iota(jnp.int32, (qc, A_BLOCK), 1)

    def attend(own_k):
        nk = own_k + 1
        s = [lax.dot_general(qb[i], kb_scr[0:nk * A_BLOCK, hsl(i)], NT_DIMS, preferred_element_type=F32)
             for i in heads]
        if own_k:
            blk_of_key = lax.broadcasted_iota(jnp.int32, (nbp, own_k * A_BLOCK), 1) // A_BLOCK
            spread = jnp.where(blk_of_key == lax.broadcasted_iota(jnp.int32, (nbp, own_k * A_BLOCK), 0),
                               1.0, 0.0).astype(BF16)
            picked = [lax.dot_general(sel_b[i], spread, TN_DIMS, preferred_element_type=F32) for i in heads]
        blocks = [[] for _ in heads]
        for n in range(nk):
            for i in heads:
                sn = s[i][:, n * A_BLOCK:(n + 1) * A_BLOCK] + slope[i] * (n * A_BLOCK + off_k[0:1, :]).astype(F32)
                if n < own_k:
                    allow = picked[i][:, n * A_BLOCK:(n + 1) * A_BLOCK] > 0.5
                else:
                    allow = off_k <= row
                blocks[i].append(jnp.where(allow, sn, NEG))
        m = [blocks[i][0].max(axis=-1, keepdims=True) for i in heads]
        for n in range(1, nk):
            m = [jnp.maximum(m[i], blocks[i][n].max(axis=-1, keepdims=True)) for i in heads]
        l = [jnp.zeros((qc, 1), F32) for _ in heads]
        o = [jnp.zeros((qc, A_HEAD_DIM), F32) for _ in heads]
        for n in range(nk):
            for i in heads:
                p = jnp.exp(blocks[i][n] - m[i])
                l[i] = l[i] + jnp.sum(p, axis=-1, keepdims=True)
                o[i] = o[i] + jnp.dot(p.astype(BF16), vb_scr[n * A_BLOCK:(n + 1) * A_BLOCK, hsl(i)],
                                      preferred_element_type=F32)
        for i in heads:
            o_ref[0, :, hsl(i)] = o[i] / l[i]

    for own_k in range(nb):
        pl.when(own == own_k)(functools.partial(attend, own_k))


def moba_prompt(q, k, v, slopes):
    b, t, w = q.shape
    nh = w // A_HEAD_DIM
    nb = t // A_BLOCK
    assert t % A_BLOCK == 0 and nb >= A_TOPK and nb <= LANES and A_BLOCK % A_QCHUNK == 0
    nbp = -(-nb // SUBLANES) * SUBLANES
    hps = 2 if nh % 2 == 0 else 1
    hw = hps * A_HEAD_DIM
    return pl.pallas_call(
        functools.partial(_moba_prompt_kernel, nb=nb),
        out_shape=jax.ShapeDtypeStruct((b, t, w), F32),
        grid=(b, nh // hps, nb),
        in_specs=[
            pl.BlockSpec(memory_space=pltpu.SMEM),
            pl.BlockSpec((1, A_BLOCK, hw), lambda i, h, c: (i, c, h)),
            pl.BlockSpec((1, t, hw), lambda i, h, c: (i, 0, h)),
            pl.BlockSpec((1, t, hw), lambda i, h, c: (i, 0, h)),
        ],
        out_specs=pl.BlockSpec((1, A_BLOCK, hw), lambda i, h, c: (i, c, h)),
        scratch_shapes=[
            pltpu.VMEM((t, hw), BF16),
            pltpu.VMEM((t, hw), BF16),
            pltpu.VMEM((hps, nbp, A_HEAD_DIM), F32),
        ],
        compiler_params=_cparams(("parallel", "parallel", "arbitrary")),
        name="moba_prompt",
    )(slopes, q, k, v)


def _moba_sample_kernel(pt_ref, q_ref, sl_ref, mb_ref, ob_ref, kn_ref, vn_ref, *rest, nh, nblk, past):
    page_refs, (o_ref, g_acc, m_acc, l_acc, o_scr) = rest[:-5], rest[-5:]
    n = pl.program_id(1)
    scale = A_HEAD_DIM ** -0.5
    rq = SUBLANES
    q = q_ref[0]
    lane = lax.broadcasted_iota(jnp.int32, (nh * rq, LANES), 1)

    @pl.when(n == 0)
    def _():
        g_acc[...] = jnp.zeros_like(g_acc)
        m_acc[...] = jnp.zeros_like(m_acc)
        l_acc[...] = jnp.zeros_like(l_acc)

    nbs = len(page_refs) // 4
    k_refs, v_refs = page_refs[:2 * nbs], page_refs[2 * nbs:]
    blk_ids = [n * nbs + jb for jb in range(nbs)]
    kpages = [(k_refs[2 * jb][...], k_refs[2 * jb + 1][...]) for jb in range(nbs)]
    head_sum = lambda pg: jnp.sum(pg.reshape(PAGE_SIZE, nh, A_HEAD_DIM), axis=0)
    kmean = [(head_sum(ka) + head_sum(kb)) * (1.0 / A_BLOCK) for ka, kb in kpages]
    kmean_rows = [jnp.concatenate([jnp.broadcast_to(km[h:h + 1], (rq, A_HEAD_DIM)) for h in range(nh)], axis=0)
                  for km in kmean]
    g = [jnp.sum(q * kr, axis=-1, keepdims=True) for kr in kmean_rows]
    kblk = [jnp.concatenate([ka, kb], axis=0).astype(BF16) for ka, kb in kpages]
    vblk = [jnp.concatenate([v_refs[2 * jb][...], v_refs[2 * jb + 1][...]], axis=0).astype(BF16)
            for jb in range(nbs)]
    qb = q.astype(BF16)
    s = [lax.dot_general(qb, kb_, NT_DIMS, preferred_element_type=F32) * scale + mb_ref[...] for kb_ in kblk]
    m = [s_.max(axis=-1, keepdims=True) for s_ in s]
    p = [jnp.exp(s_ - m_) for s_, m_ in zip(s, m)]
    l = [jnp.sum(p_, axis=-1, keepdims=True) for p_ in p]
    o = [jnp.dot(p_.astype(BF16), vb_, preferred_element_type=F32) for p_, vb_ in zip(p, vblk)]
    g_all, m_all, l_all = g_acc[...], m_acc[...], l_acc[...]
    for jb, blk in enumerate(blk_ids):
        here = lane == blk
        g_all = jnp.where(here, g[jb], g_all)
        m_all = jnp.where(here, m[jb] - sl_ref[:, 0:1] * (past - blk * A_BLOCK).astype(F32), m_all)
        l_all = jnp.where(here, l[jb], l_all)
        o_scr[blk] = o[jb]
    g_acc[...] = g_all
    m_acc[...] = m_all
    l_acc[...] = l_all

    @pl.when(n == nblk // nbs - 1)
    def _():
        gm = jnp.where(lane < nblk, g_acc[...], -jnp.inf)
        sel = jnp.zeros(gm.shape, F32)
        for i in range(nblk):
            gi = gm[:, i:i + 1]
            beats = (gm > gi) | ((gm == gi) & (lane < i))
            cnt = jnp.sum(jnp.where(beats, 1.0, 0.0), axis=-1, keepdims=True)
            sel = jnp.where(lane == i, jnp.where(cnt < A_TOPK, 1.0, 0.0), sel)
        selb = sel > 0.5
        mblk = m_acc[...]
        s_own = lax.dot_general(qb, kn_ref[0].astype(BF16), NT_DIMS, preferred_element_type=F32) * scale + ob_ref[...]
        mx = jnp.maximum(s_own.max(axis=-1, keepdims=True),
                         jnp.where(selb, mblk, NEG).max(axis=-1, keepdims=True))
        p_own = jnp.exp(s_own - mx)
        wgt = jnp.where(selb, jnp.exp(mblk - mx), 0.0)
        den = jnp.sum(p_own, axis=-1, keepdims=True) + jnp.sum(wgt * l_acc[...], axis=-1, keepdims=True)
        num = jnp.dot(p_own.astype(BF16), vn_ref[0].astype(BF16), preferred_element_type=F32)
        for i in range(nblk):
            num = num + wgt[:, i:i + 1] * o_scr[i]
        o_ref[0] = num / den


def moba_sample(q, k_new, v_new, cache_k, cache_v, layer, page_table, slopes):
    db, tn, w = q.shape
    nh = w // A_HEAD_DIM
    n_pages = page_table.shape[1]
    past = n_pages * PAGE_SIZE
    ppb = A_BLOCK // PAGE_SIZE
    rq = SUBLANES
    assert past % A_BLOCK == 0 and ppb == 2 and tn <= rq and tn * nh <= LANES and nh == SUBLANES
    nblk = past // A_BLOCK
    assert A_TOPK <= nblk <= LANES
    rows = nh * rq
    qr = jnp.pad(q.reshape(db, tn, nh, A_HEAD_DIM).transpose(0, 2, 1, 3), ((0, 0), (0, 0), (0, rq - tn), (0, 0)))
    qr = qr.reshape(db, rows, A_HEAD_DIM)
    new_rows = lambda z: jnp.pad(z.reshape(db, tn * nh, A_HEAD_DIM), ((0, 0), (0, LANES - tn * nh), (0, 0)))
    r_h = jnp.arange(rows, dtype=jnp.int32)[:, None] // rq
    r_t = jnp.arange(rows, dtype=jnp.int32)[:, None] % rq
    slope_r = slopes[r_h[:, 0]][:, None]
    col = jnp.arange(A_BLOCK * nh, dtype=jnp.int32)[None, :]
    mb = jnp.where(col % nh == r_h, -slope_r * (r_t - col // nh).astype(F32), NEG)
    colo = jnp.arange(LANES, dtype=jnp.int32)[None, :]
    jo = colo // nh
    ob = jnp.where((colo % nh == r_h) & (jo <= r_t) & (jo < tn), -slope_r * (r_t - jo).astype(F32), NEG)
    sl = jnp.broadcast_to(slope_r, (rows, LANES))

    nbs = 2 if nblk % 2 == 0 else 1
    npg = ppb * nbs

    def page_spec(j):
        return pl.BlockSpec((None, None, PAGE_SIZE * nh, A_HEAD_DIM),
                            lambda i, n, pt: (layer, pt[i, npg * n + j], 0, 0))

    const = lambda shape: pl.BlockSpec(shape, lambda i, n, pt: (0,) * len(shape))
    per_seq = lambda r: pl.BlockSpec((1, r, A_HEAD_DIM), lambda i, n, pt: (i, 0, 0))
    pages = [page_spec(j) for j in range(npg)]
    out = pl.pallas_call(
        functools.partial(_moba_sample_kernel, nh=nh, nblk=nblk, past=past),
        out_shape=jax.ShapeDtypeStruct((db, rows, A_HEAD_DIM), F32),
        grid_spec=pltpu.PrefetchScalarGridSpec(
            num_scalar_prefetch=1,
            grid=(db, nblk // nbs),
            in_specs=[per_seq(rows), const((rows, LANES)), const((rows, A_BLOCK * nh)), const((rows, LANES)),
                      per_seq(LANES), per_seq(LANES)] + pages + pages,
            out_specs=per_seq(rows),
            scratch_shapes=[pltpu.VMEM((rows, LANES), F32)] * 3 + [pltpu.VMEM((nblk, rows, A_HEAD_DIM), F32)],
        ),
        compiler_params=_cparams(("parallel", "arbitrary")),
        name="moba_sample",
    )(page_table, qr, sl, mb, ob, new_rows(k_new), new_rows(v_new), *([cache_k] * npg), *([cache_v] * npg))
    out = out.reshape(db, nh, rq, A_HEAD_DIM)[:, :, :tn].transpose(0, 2, 1, 3)
    return out.reshape(db, tn, w)


def _head_sum_matrix():
    i = lax.broadcasted_iota(jnp.int32, (LANES, LANES), 0)
    j = lax.broadcasted_iota(jnp.int32, (LANES, LANES), 1)
    return jnp.where(lax.shift_right_logical(i, 6) == lax.shift_right_logical(j, 6), 1.0, 0.0).astype(F32)


def _rwkv_prep_kernel(u_ref, pv_ref, s0_ref, mu_ref, w0_ref, w2_ref, a0_ref, a2_ref, g2_ref, kk_ref, ka_ref,
                      r_o, lw_o, k_o, v_o, kk_o, a_o, g_o, *, rw):
    t = pl.program_id(1)
    u = u_ref[0]
    tt = u.shape[0]
    prev_row = jnp.where(t == 0, s0_ref[0], pv_ref[0, SUBLANES - 1:SUBLANES, :])
    row = lax.broadcasted_iota(jnp.int32, u.shape, 0)
    prev = jnp.where(row == 0, prev_row, pltpu.roll(u, 1, axis=0))
    xs = u + (prev - u) * mu_ref[...]
    r = xs[:, 0:rw]
    k = xs[:, rw:2 * rw]
    v = xs[:, 2 * rw:3 * rw]
    lora_wa = xs[:, 3 * rw:3 * rw + LANES]
    gd = xs[:, 3 * rw + LANES:3 * rw + 2 * LANES]
    z = w0_ref[...] + jnp.dot(jnp.tanh(lora_wa), w2_ref[...], precision=HI, preferred_element_type=F32)
    nz = -z
    softplus = jnp.maximum(nz, 0.0) + jnp.log1p(jnp.exp(-jnp.abs(nz)))
    w_log = -softplus - 0.5
    lw_o[0] = -jnp.exp(w_log)
    a = jax.nn.sigmoid(a0_ref[...] + jnp.dot(lora_wa, a2_ref[...], precision=HI, preferred_element_type=F32))
    g_o[0] = jnp.dot(jax.nn.sigmoid(gd), g2_ref[...], precision=HI, preferred_element_type=F32)
    kk = k * kk_ref[...]
    seg = _head_sum_matrix()
    for p in range(rw // LANES):
        ps = slice(p * LANES, (p + 1) * LANES)
        kkp = kk[:, ps]
        ss = jnp.dot(kkp * kkp, seg, precision=HI, preferred_element_type=F32)
        kk_o[0, :, ps] = kkp / jnp.maximum(jnp.sqrt(ss), 1e-12)
    r_o[0] = r
    k_o[0] = k * (1.0 + (a - 1.0) * ka_ref[...])
    v_o[0] = v
    a_o[0] = a


def rwkv_prep(ur, shift0, prm):
    b, t, _ = ur.shape
    cols = prm['rw_mu'].shape[-1]
    rw = prm['rw_w0'].shape[-1]
    tt = _row_tile(t, 256)
    nlora = prm['rw_w2'].shape[0]
    assert 2 * nlora == LANES and prm['rw_g2'].shape[0] == LANES and cols == 3 * rw + 2 * LANES
    w2p = jnp.concatenate([prm['rw_w2'], jnp.zeros_like(prm['rw_w2'])], axis=0)
    a2p = jnp.concatenate([jnp.zeros_like(prm['rw_a2']), prm['rw_a2']], axis=0)
    row = lambda z: z.reshape(1, -1)
    full = lambda shape: pl.BlockSpec(shape, lambda i, j: (0,) * len(shape))
    outs = pl.pallas_call(
        functools.partial(_rwkv_prep_kernel, rw=rw),
        out_shape=[jax.ShapeDtypeStruct((b, t, rw), F32)] * 7,
        grid=(b, t // tt),
        in_specs=[
            pl.BlockSpec((1, tt, cols), lambda i, j: (i, j, 0)),
            pl.BlockSpec((1, SUBLANES, cols), lambda i, j: (i, jnp.maximum(j * (tt // SUBLANES) - 1, 0), 0)),
            pl.BlockSpec((1, 1, cols), lambda i, j: (i, 0, 0)),
            full((1, cols)), full((1, rw)), full((LANES, rw)), full((1, rw)), full((LANES, rw)),
            full((LANES, rw)), full((1, rw)), full((1, rw)),
        ],
        out_specs=[pl.BlockSpec((1, tt, rw), lambda i, j: (i, j, 0))] * 7,
        compiler_params=_cparams(("parallel", "parallel")),
        name="rwkv_prep",
    )(ur, ur, shift0.reshape(b, 1, cols), row(prm['rw_mu']), row(prm['rw_w0']), w2p, row(prm['rw_a0']), a2p,
      prm['rw_g2'], row(prm['rw_kk']), row(prm['rw_ka']))
    return outs


def _pair_masks():
    i = lax.broadcasted_iota(jnp.int32, (LANES, LANES), 0)
    j = lax.broadcasted_iota(jnp.int32, (LANES, LANES), 1)
    same = lax.shift_right_logical(i, 6) == lax.shift_right_logical(j, 6)
    return i, j, same


def _split(x):
    hi = x.astype(BF16)
    return hi, (x - hi.astype(F32)).astype(BF16)


def _mm3(a, b):
    n = b[0].shape[1]
    lhs = jnp.concatenate(a, axis=1)
    rhs = jnp.concatenate([jnp.concatenate(b, axis=1),
                           jnp.concatenate([b[0], jnp.zeros_like(b[0])], axis=1)], axis=0)
    out = jnp.dot(lhs, rhs, preferred_element_type=F32)
    return out[:, :n] + out[:, n:]


def _mm3_nt(a, b):
    n = b[0].shape[0]
    lhs = jnp.concatenate(a, axis=1)
    rhs = jnp.concatenate([jnp.concatenate([b[0], b[0]], axis=1),
                           jnp.concatenate([b[1], jnp.zeros_like(b[1])], axis=1)], axis=0)
    out = lax.dot_general(lhs, rhs, NT_DIMS, preferred_element_type=F32)
    return out[:, :n] + out[:, n:]


def _each(f, *lists):
    return [f(*xs) for xs in zip(*lists)]


def _rwkv_chunk_factors(r, lw, k, v, kk, a):
    c = R_CHUNK
    row = lax.broadcasted_iota(jnp.int32, (c, LANES), 0)
    cum = lw
    s = 1
    while s < c:
        cum = _each(lambda z: z + jnp.where(row >= s, pltpu.roll(z, s, axis=0), 0.0), cum)
        s *= 2
    cl = _each(lambda z: z[c - 1:c, :], cum)
    beta = _each(lambda x, y: x * y, kk, a)
    kap_t = _each(lambda x, cu, l: x * jnp.exp(cu - l), kk, cum, lw)
    r_t = _each(lambda x, cu: x * jnp.exp(cu), r, cum)
    e_inv = _each(lambda cu: jnp.exp(-cu), cum)
    b_t = _each(lambda x, e: x * e, beta, e_inv)
    k_t = _each(lambda x, e: x * e, k, e_inv)
    e_end = _each(lambda l, cu: jnp.exp(l - cu), cl, cum)
    b_h = _each(lambda x, e: x * e, beta, e_end)
    k_h = _each(lambda x, e: x * e, k, e_end)

    lo = lax.broadcasted_iota(jnp.int32, (c, LANES), 1) < R_HEAD_DIM
    stack = lambda x: jnp.concatenate([jnp.where(lo, x, 0.0), jnp.where(lo, 0.0, x)], axis=0)
    dup = lambda x: jnp.concatenate([x, x], axis=0)
    split_of = lambda f: (lambda x: _split(f(x)))
    ident = lambda x: x

    i, j, same = _pair_masks()
    strict = same & (j < i)
    incl = same & (j <= i)
    kap_s, r_s, v_s = _each(stack, kap_t), _each(stack, r_t), _each(stack, v)
    kap_p, r_p, v_p = _each(_split, kap_s), _each(_split, r_s), _each(_split, v_s)
    b_d, k_d = _each(split_of(dup), b_t), _each(split_of(dup), k_t)
    a_ab = _each(lambda x, y: jnp.where(strict, _mm3_nt(x, y), 0.0), kap_p, b_d)
    a_ak = _each(lambda x, y: jnp.where(strict, _mm3_nt(x, y), 0.0), kap_p, k_d)
    a_rb = _each(lambda x, y: jnp.where(incl, _mm3_nt(x, y), 0.0), r_p, b_d)
    a_rk = _each(lambda x, y: jnp.where(incl, _mm3_nt(x, y), 0.0), r_p, k_d)

    x = [jnp.where(i == j, 1.0, 0.0).astype(F32)] * len(r)
    s = 1
    while s < c:
        low = (lax.shift_right_logical(i, s.bit_length()) == lax.shift_right_logical(j, s.bit_length())) \
            & ((i & (2 * s - 1)) >= s) & ((j & (2 * s - 1)) < s)
        xp = _each(_split, x)
        ms = _each(lambda z: _split(jnp.where(low, z, 0.0)), a_ab)
        t1 = _each(split_of(ident), _each(_mm3, xp, ms))
        x = _each(lambda z, u, w_: z - _mm3(u, w_), x, t1, xp)
        s *= 2

    aakv = _each(_mm3, _each(_split, a_ak), v_p)
    arkv = _each(_mm3, _each(_split, a_rk), v_p)
    bh_p = _each(split_of(stack), b_h)
    vtk = _each(_mm3, _each(lambda z: _split(z.T), v_s), _each(split_of(stack), k_h))
    rhs = _each(lambda z, u: _split(jnp.concatenate([z, u], axis=1)), kap_s, aakv)
    wu = _each(lambda z, u: -_mm3(z, u), _each(_split, x), rhs)
    w = _each(lambda z: z[:, :LANES], wu)
    upre = _each(lambda z: z[:, LANES:], wu)
    arb_p = _each(_split, a_rb)
    r2 = _each(lambda z, u, w_: z + _mm3(u, _split(w_)), r_s, arb_p, w)
    ypre = _each(lambda z, u, w_: z + _mm3(u, _split(w_)), arkv, arb_p, upre)
    g = _each(lambda l, w_, u: jnp.where(i == j, jnp.exp(l), 0.0) + _mm3(_split(w_.T), u), cl, w, bh_p)
    spre = _each(lambda z, w_, u: z + _mm3(_split(w_.T), u), vtk, upre, bh_p)
    return list(zip(r2, ypre, g, spre))


def _rwkv_core_kernel(r_ref, lw_ref, k_ref, v_ref, kk_ref, a_ref, s0_ref, y_o, st_o, s_scr, *, nch, npair):
    cidx = pl.program_id(1)

    @pl.when(cidx == 0)
    def _():
        s_scr[...] = s0_ref[0]

    c = R_CHUNK
    parts = lambda ref: [ref[0, ch * c:(ch + 1) * c, p * LANES:(p + 1) * LANES]
                         for ch in range(nch) for p in range(npair)]
    factors = _rwkv_chunk_factors(parts(r_ref), parts(lw_ref), parts(k_ref), parts(v_ref), parts(kk_ref),
                                  parts(a_ref))
    s = [s_scr[p] for p in range(npair)]
    for ch in range(nch):
        for p in range(npair):
            r2, ypre, g, spre = factors[ch * npair + p]
            sp = _split(s[p])
            ys = _mm3_nt(_split(r2), sp) + ypre
            y_o[0, ch * c:(ch + 1) * c, p * LANES:(p + 1) * LANES] = ys[:c] + ys[c:]
            s[p] = _mm3(sp, _split(g)) + spre
    for p in range(npair):
        s_scr[p] = s[p]
        st_o[0, p] = s[p]


def rwkv_scan(r, lw, k, v, kk, a, wkv0):
    b, t, rw = r.shape
    npair = rw // LANES
    assert t % R_CHUNK == 0 and 2 * R_HEAD_DIM == LANES
    nch = 2 if t % (2 * R_CHUNK) == 0 else 1
    tt = nch * R_CHUNK
    w4 = wkv0.reshape(b, npair, 2, R_HEAD_DIM, R_HEAD_DIM)
    zero = jnp.zeros_like(w4[:, :, 0])
    s0 = jnp.concatenate([jnp.concatenate([w4[:, :, 0], zero], axis=-1),
                          jnp.concatenate([zero, w4[:, :, 1]], axis=-1)], axis=-2)
    act = pl.BlockSpec((1, tt, rw), lambda i, c: (i, c, 0))
    st_spec = pl.BlockSpec((1, npair, LANES, LANES), lambda i, c: (i, 0, 0, 0))
    y, st = pl.pallas_call(
        functools.partial(_rwkv_core_kernel, nch=nch, npair=npair),
        out_shape=[jax.ShapeDtypeStruct((b, t, rw), F32), jax.ShapeDtypeStruct((b, npair, LANES, LANES), F32)],
        grid=(b, t // tt),
        in_specs=[act] * 6 + [st_spec],
        out_specs=[act, st_spec],
        scratch_shapes=[pltpu.VMEM((npair, LANES, LANES), F32)],
        compiler_params=_cparams(("parallel", "arbitrary")),
        name="rwkv_core",
    )(r, lw, k, v, kk, a, s0)
    wkv = jnp.stack([st[:, :, :R_HEAD_DIM, :R_HEAD_DIM], st[:, :, R_HEAD_DIM:, R_HEAD_DIM:]], axis=2)
    return y, wkv.reshape(wkv0.shape)


def _rwkv_post_kernel(y_ref, r_ref, k_ref, v_ref, g_ref, rk_ref, lw_ref, lb_ref, o_ref):
    seg = _head_sum_matrix()
    inv_n = 1.0 / R_HEAD_DIM
    for p in range(y_ref.shape[2] // LANES):
        ps = slice(p * LANES, (p + 1) * LANES)
        y = y_ref[0, :, ps]
        hsum = lambda z: jnp.dot(z, seg, precision=HI, preferred_element_type=F32)
        mean = hsum(y) * inv_n
        d = y - mean
        var = hsum(d * d) * inv_n
        yn = d * lax.rsqrt(var + R_GN_EPS) * lw_ref[:, ps] + lb_ref[:, ps]
        bonus = hsum(r_ref[0, :, ps] * k_ref[0, :, ps] * rk_ref[:, ps]) * v_ref[0, :, ps]
        o_ref[0, :, ps] = (yn + bonus) * g_ref[0, :, ps]


def rwkv_post(y, r, k, v, g, prm):
    b, t, rw = y.shape
    tt = _row_tile(t, 256)
    act = pl.BlockSpec((1, tt, rw), lambda i, j: (i, j, 0))
    par = pl.BlockSpec((1, rw), lambda i, j: (0, 0))
    return pl.pallas_call(
        _rwkv_post_kernel,
        out_shape=jax.ShapeDtypeStruct((b, t, rw), F32),
        grid=(b, t // tt),
        in_specs=[act] * 5 + [par] * 3,
        out_specs=act,
        compiler_params=_cparams(("parallel", "parallel")),
        name="rwkv_post",
    )(y, r, k, v, g, prm['rw_rk'].reshape(1, rw), prm['rw_lnx_w'].reshape(1, rw), prm['rw_lnx_b'].reshape(1, rw))


def rwkv_mix(ur, shift0, wkv0, prm):
    b, t, _ = ur.shape
    tp = -(-t // SUBLANES) * SUBLANES
    urp = jnp.pad(ur, ((0, 0), (0, tp - t), (0, 0)))
    r, lw, k, v, kk, a, g = rwkv_prep(urp, shift0, prm)
    tc = -(-t // R_CHUNK) * R_CHUNK
    fit = lambda z: jnp.pad(z[:, :t], ((0, 0), (0, tc - t), (0, 0)))
    r, lw, k, v, kk, a, g = (fit(z) for z in (r, lw, k, v, kk, a, g))
    y, wkv = rwkv_scan(r, lw, k, v, kk, a, wkv0)
    out = rwkv_post(y, r, k, v, g, prm)
    return out[:, :t], wkv, ur[:, -1, :prm['rw_mu'].shape[-1]]


def _pool_kernel(u_ref, pv_ref, p0_ref, w_ref, sc_ref, o_ref, *, pos0):
    t = pl.program_id(1)
    cur = u_ref[0]
    tt = cur.shape[0]
    ext = jnp.concatenate([jnp.where(t == 0, p0_ref[0], pv_ref[0]), cur], axis=0)
    sums = []
    s = ext
    w = 1
    while w < POOL_MAX:
        s = s + pltpu.roll(s, w, axis=0)
        w *= 2
        sums.append(s)
    pos = pos0 + t * tt + lax.broadcasted_iota(jnp.int32, (tt, 1), 0)
    for gi, w in enumerate(POOL_WINDOWS):
        gs = slice(gi * LANES, (gi + 1) * LANES)
        win = sums[w.bit_length() - 2][POOL_MAX:, gs]
        cnt = jnp.minimum(pos + 1, w).astype(F32)
        m = win / cnt - cur[:, gs]
        z = jnp.dot(m.astype(BF16), w_ref[gi], preferred_element_type=F32)
        o_ref[0, :, gs] = z * sc_ref[:, gs]


def pool_mix(uc, pool0, pos0, pool_w, pool_scale):
    b, t, width = uc.shape
    assert width == len(POOL_WINDOWS) * LANES and pool_w.shape[1] == LANES
    tp = -(-t // SUBLANES) * SUBLANES
    ucp = jnp.pad(uc, ((0, 0), (0, tp - t), (0, 0)))
    tt = _row_tile(tp, 256)
    assert tt % POOL_MAX == 0 or tp == tt
    p0 = jnp.pad(pool0, ((0, 0), (1, 0), (0, 0)))
    nprev = tt // POOL_MAX if tt % POOL_MAX == 0 else 0
    ext = jnp.concatenate([pool0, uc], axis=1)
    if nprev:
        pv, pv_spec = ucp, pl.BlockSpec((1, POOL_MAX, width), lambda i, j: (i, jnp.maximum(j * nprev - 1, 0), 0))
    else:
        pv, pv_spec = p0, pl.BlockSpec((1, POOL_MAX, width), lambda i, j: (i, 0, 0))
    z = pl.pallas_call(
        functools.partial(_pool_kernel, pos0=pos0),
        out_shape=jax.ShapeDtypeStruct((b, tp, width), F32),
        grid=(b, tp // tt),
        in_specs=[
            pl.BlockSpec((1, tt, width), lambda i, j: (i, j, 0)),
            pv_spec,
            pl.BlockSpec((1, POOL_MAX, width), lambda i, j: (i, 0, 0)),
            pl.BlockSpec(pool_w.shape, lambda i, j: (0, 0, 0)),
            pl.BlockSpec((1, width), lambda i, j: (0, 0)),
        ],
        out_specs=pl.BlockSpec((1, tt, width), lambda i, j: (i, j, 0)),
        compiler_params=_cparams(("parallel", "parallel")),
        name="pool_mix",
    )(ucp, pv, p0, pool_w.astype(BF16), pool_scale.reshape(1, width))
    return z[:, :t], ext[:, -(POOL_MAX - 1):]


def _mix_out_kernel(x_ref, ya_ref, yr_ref, yc_ref, wa_ref, wr_ref, wc_ref, o_ref):
    acc = jnp.dot(ya_ref[...].astype(BF16), wa_ref[...], preferred_element_type=F32)
    acc += jnp.dot(yr_ref[...].astype(BF16), wr_ref[...], preferred_element_type=F32)
    acc += jnp.dot(yc_ref[...].astype(BF16), wc_ref[...], preferred_element_type=F32)
    o_ref[...] = x_ref[...] + acc


def mix_out(x, ya, yr, yc, w_out):
    n, d = x.shape
    wa_, wr_, wc_ = ya.shape[1], yr.shape[1], yc.shape[1]
    tm = _row_tile(n, 512)
    tn = 1024
    act = lambda wd: pl.BlockSpec((tm, wd), lambda i, j: (i, 0))
    wsp = lambda wd: pl.BlockSpec((wd, tn), lambda i, j: (0, j))
    return pl.pallas_call(
        _mix_out_kernel,
        out_shape=jax.ShapeDtypeStruct((n, d), F32),
        grid=(n // tm, d // tn),
        in_specs=[pl.BlockSpec((tm, tn), lambda i, j: (i, j)), act(wa_), act(wr_), act(wc_),
                  wsp(wa_), wsp(wr_), wsp(wc_)],
        out_specs=pl.BlockSpec((tm, tn), lambda i, j: (i, j)),
        compiler_params=_cparams(("parallel", "arbitrary")),
        name="mix_out",
    )(x, ya, yr, yc, w_out[:wa_], w_out[wa_:wa_ + wr_], w_out[wa_ + wr_:])


def _cross_kernel(x_ref, g_ref, wq_ref, mk_ref, mv_ref, wo_ref, o_ref, *, nh):
    x = x_ref[0]
    h = _rms(x, g_ref[...]).astype(BF16)
    q = jnp.dot(h, wq_ref[...], preferred_element_type=F32)
    scale = M_HEAD_DIM ** -0.5
    outs = []
    for hh in range(nh):
        hs = slice(hh * M_HEAD_DIM, (hh + 1) * M_HEAD_DIM)
        s = lax.dot_general(q[:, hs].astype(BF16), mk_ref[0, :, hs].astype(BF16), NT_DIMS,
                            preferred_element_type=F32) * scale
        p = jnp.exp(s - s.max(axis=-1, keepdims=True))
        l = jnp.sum(p, axis=-1, keepdims=True)
        outs.append(jnp.dot(p.astype(BF16), mv_ref[0, :, hs].astype(BF16), preferred_element_type=F32) / l)
    o = jnp.concatenate(outs, axis=-1).astype(BF16)
    o_ref[0] = x + jnp.dot(o, wo_ref[...], preferred_element_type=F32)


def cross_attend(x, g, wq, mk, mv, wo):
    b, t0, d = x.shape
    t = -(-t0 // SUBLANES) * SUBLANES
    x = jnp.pad(x, ((0, 0), (0, t - t0), (0, 0)))
    mw = wq.shape[1]
    nm = mk.shape[1]
    tm = _row_tile(t, 512)
    out = pl.pallas_call(
        functools.partial(_cross_kernel, nh=mw // M_HEAD_DIM),
        out_shape=jax.ShapeDtypeStruct((b, t, d), F32),
        grid=(b, t // tm),
        in_specs=[
            pl.BlockSpec((1, tm, d), lambda i, j: (i, j, 0)),
            pl.BlockSpec((1, d), lambda i, j: (0, 0)),
            pl.BlockSpec((d, mw), lambda i, j: (0, 0)),
            pl.BlockSpec((1, nm, mw), lambda i, j: (i, 0, 0)),
            pl.BlockSpec((1, nm, mw), lambda i, j: (i, 0, 0)),
            pl.BlockSpec((mw, d), lambda i, j: (0, 0)),
        ],
        out_specs=pl.BlockSpec((1, tm, d), lambda i, j: (i, j, 0)),
        compiler_params=_cparams(("parallel", "parallel")),
        name="cross_attend",
    )(x, g.reshape(1, d), wq, mk, mv, wo)
    return out[:, :t0]


def _layer(x, prm, moba_fn, pos0, shift0, wkv0, pool0, mem_k, mem_v):
    b, t, d = x.shape
    n = b * t
    x2 = ffn_half_step(x.reshape(n, d), prm['norm_ffn1'], prm['ffn1_gate'], prm['ffn1_up'], prm['ffn1_down'])
    aw, rc, cw = prm['a_width'], prm['r_cols'], prm['c_width']
    qa, ka, va, ur, uc = (z.reshape(b, t, -1) for z in
                          in_proj(x2, prm['norm_mix'], prm['w_in'], (aw, aw, aw, rc, cw)))
    ya = moba_fn(qa, ka, va)
    yr, wkv, shift = rwkv_mix(ur, shift0, wkv0, prm)
    yc, pool_buf = pool_mix(uc, pool0, pos0, prm['pool_w'], prm['pool_scale'])
    x3 = mix_out(x2, ya.reshape(n, aw), yr.reshape(n, -1), yc.reshape(n, cw), prm['w_out'])
    x4 = cross_attend(x3.reshape(b, t, d), prm['norm_cross'], prm['mem_wq'], mem_k, mem_v, prm['mem_wo'])
    x5 = ffn_half_step(x4.reshape(n, d), prm['norm_ffn2'], prm['ffn2_gate'], prm['ffn2_up'], prm['ffn2_down'])
    return x5.reshape(b, t, d), ka, va, wkv, shift, pool_buf


def kernel(x_prompt, x_sample, cache_k, cache_v, cache_mem_k, cache_mem_v, state_wkv, state_shift, state_pool, page_table, mem_prompt, norm_ffn1, ffn1_gate, ffn1_up, ffn1_down, norm_mix, w_in, w_out, rw_mu, rw_w0, rw_w2, rw_a0, rw_a2, rw_g2, rw_kk, rw_ka, rw_rk, rw_lnx_w, rw_lnx_b, pool_w, pool_scale, norm_cross, norm_mem, mem_wq, mem_wk, mem_wv, mem_wo, norm_ffn2, ffn2_gate, ffn2_up, ffn2_down, norm_final):
    depth = w_in.shape[0]
    bp, tp, d = x_prompt.shape
    db, ts, _ = x_sample.shape
    n_heads, hd = cache_k.shape[3], cache_k.shape[4]
    aw = n_heads * hd
    r_heads, rn = rw_rk.shape[1], rw_rk.shape[2]
    r_cols = rw_mu.shape[1]
    c_width = pool_scale.shape[1]
    nm, m_heads, mhd = cache_mem_k.shape[2:]
    mw = m_heads * mhd
    assert hd == A_HEAD_DIM and rn == R_HEAD_DIM and mhd == M_HEAD_DIM
    past = page_table.shape[1] * PAGE_SIZE
    slopes = jnp.exp2(-8.0 * jnp.arange(1, n_heads + 1, dtype=F32) / n_heads)
    ck = cache_k.reshape(depth, cache_k.shape[1], PAGE_SIZE * n_heads, hd)
    cv = cache_v.reshape(depth, cache_v.shape[1], PAGE_SIZE * n_heads, hd)
    bf = lambda z: z.astype(BF16)
    bounds = (0, aw, 2 * aw, 3 * aw, 3 * aw + r_cols, w_in.shape[2])
    groups = [bf(w_in[:, :, lo:hi]) for lo, hi in zip(bounds[:-1], bounds[1:])]
    w_in_p = jnp.concatenate([jnp.pad(gw, ((0, 0), (0, 0), (0, -gw.shape[2] % IN_TILE))) for gw in groups], axis=2)
    big = dict(ffn1_gate=bf(ffn1_gate), ffn1_up=bf(ffn1_up), ffn1_down=bf(ffn1_down), w_in=w_in_p,
               w_out=bf(w_out), mem_wq=bf(mem_wq), mem_wk=bf(mem_wk), mem_wv=bf(mem_wv), mem_wo=bf(mem_wo),
               ffn2_gate=bf(ffn2_gate), ffn2_up=bf(ffn2_up), ffn2_down=bf(ffn2_down))
    small = dict(norm_ffn1=norm_ffn1, norm_mix=norm_mix, rw_mu=rw_mu, rw_w0=rw_w0, rw_w2=rw_w2, rw_a0=rw_a0,
                 rw_a2=rw_a2, rw_g2=rw_g2, rw_kk=rw_kk, rw_ka=rw_ka, rw_rk=rw_rk.reshape(depth, -1),
                 rw_lnx_w=rw_lnx_w, rw_lnx_b=rw_lnx_b, pool_w=pool_w, pool_scale=pool_scale,
                 norm_cross=norm_cross, norm_ffn2=norm_ffn2)

    shift0 = jnp.zeros((bp, r_cols), F32)
    wkv0 = jnp.zeros((bp, r_heads, rn, rn), F32)
    pool0 = jnp.zeros((bp, POOL_MAX - 1, c_width), F32)
    xp, xs = x_prompt, x_sample
    outs = [[] for _ in range(12)]
    for l in range(depth):
        prm = {k: v[l] for k, v in big.items()}
        prm.update({k: v[l] for k, v in small.items()})
        prm.update(a_width=aw, r_cols=r_cols, c_width=c_width)
        hm = rmsnorm(mem_prompt.reshape(bp * nm, d), norm_mem[l], BF16)
        mk = matmul(hm, prm['mem_wk'], mw).reshape(bp, nm, mw)
        mv = matmul(hm, prm['mem_wv'], mw).reshape(bp, nm, mw)
        moba_p = functools.partial(moba_prompt, slopes=slopes)
        xp, k_, v_, w_, sh_, pl_ = _layer(xp, prm, moba_p, 0, shift0, wkv0, pool0, mk, mv)
        res_p = (k_.reshape(bp, tp, n_heads, hd), v_.reshape(bp, tp, n_heads, hd), w_, sh_, pl_,
                 mk.reshape(bp, nm, m_heads, mhd), mv.reshape(bp, nm, m_heads, mhd))
        moba_s = functools.partial(moba_sample, cache_k=ck, cache_v=cv, layer=l, page_table=page_table,
                                   slopes=slopes)
        xs, k_, v_, w_, sh_, pl_ = _layer(xs, prm, moba_s, past, state_shift[l], state_wkv[l], state_pool[l],
                                           cache_mem_k[l].reshape(db, nm, mw), cache_mem_v[l].reshape(db, nm, mw))
        res_s = (k_.reshape(db, ts, n_heads, hd), v_.reshape(db, ts, n_heads, hd), w_, sh_, pl_)
        for i, z in enumerate(res_p + res_s):
            outs[i].append(z)
    y_prompt = rmsnorm(xp.reshape(bp * tp, d), norm_final, F32).reshape(bp, tp, d)
    y_sample = rmsnorm(xs.reshape(db * ts, d), norm_final, F32).reshape(db, ts, d)
    return (y_prompt, y_sample) + tuple(jnp.stack(o) for o in outs)
```

```python
import functools

import jax
import jax.numpy as jnp
from jax import lax
from jax.experimental import pallas as pl
from jax.experimental.pallas import tpu as pltpu

F32 = jnp.float32
BF16 = jnp.bfloat16
HI = lax.Precision.HIGHEST

RMS_EPS = 1e-6
LANES = 128
SUBLANES = 8
VMEM_LIMIT = 48 * 1024 * 1024

PAGE_SIZE = 128
A_HEAD_DIM = 128
A_BLOCK = 256
A_TOPK = 3
A_QCHUNK = 128
R_HEAD_DIM = 64
R_CHUNK = 64
R_GN_EPS = 64e-5
POOL_WINDOWS = (2, 4, 8, 16)
POOL_MAX = 16
M_HEAD_DIM = 128
NEG = -1e30

NT_DIMS = (((1,), (1,)), ((), ()))
TN_DIMS = (((0,), (0,)), ((), ()))


VMEM_LIMIT_BIG = 58 * 1024 * 1024


def _cparams(sem, vmem_limit=VMEM_LIMIT):
    return pltpu.CompilerParams(dimension_semantics=sem, vmem_limit_bytes=vmem_limit)


def _row_tile(n, pref):
    return pref if n % pref == 0 else n


def _rms(x, g):
    ms = jnp.mean(x * x, axis=-1, keepdims=True)
    return x * lax.rsqrt(ms + RMS_EPS) * g


def _rms_kernel(x_ref, g_ref, o_ref):
    o_ref[...] = _rms(x_ref[...], g_ref[...]).astype(o_ref.dtype)


def rmsnorm(x, g, out_dtype):
    n, d = x.shape
    tm = _row_tile(n, 512)
    return pl.pallas_call(
        _rms_kernel,
        out_shape=jax.ShapeDtypeStruct((n, d), out_dtype),
        grid=(n // tm,),
        in_specs=[pl.BlockSpec((tm, d), lambda i: (i, 0)), pl.BlockSpec((1, d), lambda i: (0, 0))],
        out_specs=pl.BlockSpec((tm, d), lambda i: (i, 0)),
        compiler_params=_cparams(("parallel",)),
        name="rmsnorm",
    )(x, g.reshape(1, d))


def _matmul_kernel(a_ref, w_ref, o_ref):
    o_ref[...] = jnp.dot(a_ref[...], w_ref[...], preferred_element_type=F32)


def matmul(a, w, tn):
    n, k = a.shape
    m = w.shape[1]
    tm = _row_tile(n, 512)
    return pl.pallas_call(
        _matmul_kernel,
        out_shape=jax.ShapeDtypeStruct((n, m), F32),
        grid=(n // tm, m // tn),
        in_specs=[pl.BlockSpec((tm, k), lambda i, j: (i, 0)), pl.BlockSpec((k, tn), lambda i, j: (0, j))],
        out_specs=pl.BlockSpec((tm, tn), lambda i, j: (i, j)),
        compiler_params=_cparams(("parallel", "arbitrary")),
        name="matmul",
    )(a, w)


IN_TILE = 512


def _in_proj_kernel(x_ref, g_ref, w_ref, *rest, first_tile):
    out_refs, h_scr = rest[:-1], rest[-1]
    j = pl.program_id(1)

    @pl.when(j == 0)
    def _():
        h_scr[...] = _rms(x_ref[...], g_ref[...]).astype(BF16)

    for o_ref, lo, hi in zip(out_refs, first_tile[:-1], first_tile[1:]):
        @pl.when((j >= lo) & (j < hi))
        def _(o_ref=o_ref):
            o_ref[...] = jnp.dot(h_scr[...], w_ref[...], preferred_element_type=F32)


def in_proj(x, g, w, layer, widths):
    n, d = x.shape
    tm = _row_tile(n, 1024)
    tiles = [-(-wd // IN_TILE) for wd in widths]
    first_tile = [0]
    for nt in tiles:
        first_tile.append(first_tile[-1] + nt)
    assert w.shape[2] == first_tile[-1] * IN_TILE

    def out_spec(lo, nt):
        return pl.BlockSpec((tm, IN_TILE), lambda i, j: (i, jnp.clip(j - lo, 0, nt - 1)))

    return pl.pallas_call(
        functools.partial(_in_proj_kernel, first_tile=tuple(first_tile)),
        out_shape=[jax.ShapeDtypeStruct((n, nt * IN_TILE), F32) for nt in tiles],
        grid=(n // tm, first_tile[-1]),
        in_specs=[pl.BlockSpec((tm, d), lambda i, j: (i, 0)), pl.BlockSpec((1, d), lambda i, j: (0, 0)),
                  pl.BlockSpec((None, d, IN_TILE), lambda i, j: (layer, 0, j))],
        out_specs=[out_spec(lo, nt) for lo, nt in zip(first_tile[:-1], tiles)],
        scratch_shapes=[pltpu.VMEM((tm, d), BF16)],
        compiler_params=_cparams(("parallel", "arbitrary"), VMEM_LIMIT_BIG),
        name="in_proj",
    )(x, g.reshape(1, d), w)


FFN_OUT_CHUNK = 512


def _ffn_kernel(x_ref, g_ref, wg_ref, wu_ref, wd_ref, o_ref, h_scr):
    j = pl.program_id(1)

    @pl.when(j == 0)
    def _():
        x = x_ref[...]
        h_scr[...] = _rms(x, g_ref[...]).astype(BF16)
        o_ref[...] = 2.0 * x

    h = h_scr[...]
    gate = jnp.dot(h, wg_ref[...], preferred_element_type=F32)
    up = jnp.dot(h, wu_ref[...], preferred_element_type=F32)
    act = (gate * jax.nn.sigmoid(gate) * up).astype(BF16)
    for c0 in range(0, o_ref.shape[1], FFN_OUT_CHUNK):
        cols = slice(c0, c0 + FFN_OUT_CHUNK)
        o_ref[:, cols] += jnp.dot(act, wd_ref[:, cols], preferred_element_type=F32)

    @pl.when(j == pl.num_programs(1) - 1)
    def _():
        o_ref[...] = 0.5 * o_ref[...]


def ffn_half_step(x, g, wg, wu, wd, layer, tf=256):
    n, d = x.shape
    f = wg.shape[2]
    tm = _row_tile(n, 1024)
    assert d % FFN_OUT_CHUNK == 0
    return pl.pallas_call(
        _ffn_kernel,
        out_shape=jax.ShapeDtypeStruct((n, d), F32),
        grid=(n // tm, f // tf),
        in_specs=[
            pl.BlockSpec((tm, d), lambda i, j: (i, 0)),
            pl.BlockSpec((1, d), lambda i, j: (0, 0)),
            pl.BlockSpec((None, d, tf), lambda i, j: (layer, 0, j)),
            pl.BlockSpec((None, d, tf), lambda i, j: (layer, 0, j)),
            pl.BlockSpec((None, tf, d), lambda i, j: (layer, j, 0)),
        ],
        out_specs=pl.BlockSpec((tm, d), lambda i, j: (i, 0)),
        scratch_shapes=[pltpu.VMEM((tm, d), BF16)],
        compiler_params=_cparams(("parallel", "arbitrary"), VMEM_LIMIT_BIG),
        name="ffn_half_step",
    )(x, g.reshape(1, d), wg, wu, wd)


def _moba_prompt_kernel(slope_ref, q_ref, k_ref, v_ref, o_ref, kb_scr, vb_scr, km_scr, *, nb):
    h = pl.program_id(1)
    own = pl.program_id(2)
    qc = A_BLOCK
    hps, nbp = km_scr.shape[0], km_scr.shape[1]
    heads = range(hps)
    hsl = lambda i: slice(i * A_HEAD_DIM, (i + 1) * A_HEAD_DIM)

    @pl.when(own == 0)
    def _():
        k = k_ref[0]
        kb_scr[...] = k.astype(BF16)
        vb_scr[...] = v_ref[0].astype(BF16)
        km_scr[...] = jnp.zeros_like(km_scr)
        for i in heads:
            for n in range(nb):
                km_scr[i, n:n + 1, :] = jnp.sum(k[n * A_BLOCK:(n + 1) * A_BLOCK, hsl(i)], axis=0,
                                                keepdims=True) * (1.0 / A_BLOCK)

    q = [q_ref[0, :, hsl(i)] for i in heads]
    slope = [slope_ref[h * hps + i] for i in heads]
    scale = A_HEAD_DIM ** -0.5

    g = [lax.dot_general(km_scr[i], q[i], NT_DIMS, precision=HI, preferred_element_type=F32) for i in heads]
    blk = lax.broadcasted_iota(jnp.int32, (nbp, qc), 0)
    gm = [jnp.where(blk < own, g_, -jnp.inf) for g_ in g]
    sel_t = [jnp.zeros((nbp, qc), F32) for _ in heads]
    for n in range(nb):
        for i in heads:
            gn = gm[i][n:n + 1, :]
            beats = (gm[i] > gn) | ((gm[i] == gn) & (blk < n))
            cnt = jnp.sum(jnp.where(beats, 1.0, 0.0), axis=0, keepdims=True)
            sel_n = jnp.where(cnt < A_TOPK, 1.0, 0.0) * jnp.where(n < own, 1.0, 0.0)
            sel_t[i] = jnp.where(blk == n, sel_n, sel_t[i])
    sel_b = [st.astype(BF16) for st in sel_t]

    qb = [(q_ * scale).astype(BF16) for q_ in q]
    row = lax.broadcasted_iota(jnp.int32, (qc, A_BLOCK), 0)
    off_k = lax.broadcasted_iota(jnp.int32, (qc, A_BLOCK), 1)

    def attend(own_k):
        nk = own_k + 1
        s = [lax.dot_general(qb[i], kb_scr[0:nk * A_BLOCK, hsl(i)], NT_DIMS, preferred_element_type=F32)
             for i in heads]
        if own_k:
            blk_of_key = lax.broadcasted_iota(jnp.int32, (nbp, own_k * A_BLOCK), 1) // A_BLOCK
            spread = jnp.where(blk_of_key == lax.broadcasted_iota(jnp.int32, (nbp, own_k * A_BLOCK), 0),
                               1.0, 0.0).astype(BF16)
            picked = [lax.dot_general(sel_b[i], spread, TN_DIMS, preferred_element_type=F32) for i in heads]
        blocks = [[] for _ in heads]
        for n in range(nk):
            for i in heads:
                sn = s[i][:, n * A_BLOCK:(n + 1) * A_BLOCK] + slope[i] * (n * A_BLOCK + off_k[0:1, :]).astype(F32)
                if n < own_k:
                    allow = picked[i][:, n * A_BLOCK:(n + 1) * A_BLOCK] > 0.5
                else:
                    allow = off_k <= row
                blocks[i].append(jnp.where(allow, sn, NEG))
        m = [blocks[i][0].max(axis=-1, keepdims=True) for i in heads]
        for n in range(1, nk):
            m = [jnp.maximum(m[i], blocks[i][n].max(axis=-1, keepdims=True)) for i in heads]
        l = [jnp.zeros((qc, 1), F32) for _ in heads]
        o = [jnp.zeros((qc, A_HEAD_DIM), F32) for _ in heads]
        for n in range(nk):
            for i in heads:
                p = jnp.exp(blocks[i][n] - m[i])
                l[i] = l[i] + jnp.sum(p, axis=-1, keepdims=True)
                o[i] = o[i] + jnp.dot(p.astype(BF16), vb_scr[n * A_BLOCK:(n + 1) * A_BLOCK, hsl(i)],
                                      preferred_element_type=F32)
        for i in heads:
            o_ref[0, :, hsl(i)] = o[i] / l[i]

    for own_k in range(nb):
        pl.when(own == own_k)(functools.partial(attend, own_k))


def moba_prompt(q, k, v, slopes):
    b, t, w = q.shape
    nh = w // A_HEAD_DIM
    nb = t // A_BLOCK
    assert t % A_BLOCK == 0 and nb >= A_TOPK and nb <= LANES and A_BLOCK % A_QCHUNK == 0
    nbp = -(-nb // SUBLANES) * SUBLANES
    hps = 2 if nh % 2 == 0 else 1
    hw = hps * A_HEAD_DIM
    return pl.pallas_call(
        functools.partial(_moba_prompt_kernel, nb=nb),
        out_shape=jax.ShapeDtypeStruct((b, t, w), F32),
        grid=(b, nh // hps, nb),
        in_specs=[
            pl.BlockSpec(memory_space=pltpu.SMEM),
            pl.BlockSpec((1, A_BLOCK, hw), lambda i, h, c: (i, c, h)),
            pl.BlockSpec((1, t, hw), lambda i, h, c: (i, 0, h)),
            pl.BlockSpec((1, t, hw), lambda i, h, c: (i, 0, h)),
        ],
        out_specs=pl.BlockSpec((1, A_BLOCK, hw), lambda i, h, c: (i, c, h)),
        scratch_shapes=[
            pltpu.VMEM((t, hw), BF16),
            pltpu.VMEM((t, hw), BF16),
            pltpu.VMEM((hps, nbp, A_HEAD_DIM), F32),
        ],
        compiler_params=_cparams(("parallel", "parallel", "arbitrary")),
        name="moba_prompt",
    )(slopes, q, k, v)


def _moba_sample_kernel(pt_ref, q_ref, sl_ref, mb_ref, ob_ref, kn_ref, vn_ref, *rest, nh, nblk, past):
    page_refs, (o_ref, g_acc, m_acc, l_acc, o_scr) = rest[:-5], rest[-5:]
    n = pl.program_id(1)
    scale = A_HEAD_DIM ** -0.5
    rq = SUBLANES
    q = q_ref[0]
    lane = lax.broadcasted_iota(jnp.int32, (nh * rq, LANES), 1)

    @pl.when(n == 0)
    def _():
        g_acc[...] = jnp.zeros_like(g_acc)
        m_acc[...] = jnp.zeros_like(m_acc)
        l_acc[...] = jnp.zeros_like(l_acc)

    nbs = len(page_refs) // 4
    k_refs, v_refs = page_refs[:2 * nbs], page_refs[2 * nbs:]
    blk_ids = [n * nbs + jb for jb in range(nbs)]
    kpages = [(k_refs[2 * jb][...], k_refs[2 * jb + 1][...]) for jb in range(nbs)]
    head_sum = lambda pg: jnp.sum(pg.reshape(PAGE_SIZE, nh, A_HEAD_DIM), axis=0)
    kmean = [(head_sum(ka) + head_sum(kb)) * (1.0 / A_BLOCK) for ka, kb in kpages]
    kmean_rows = [jnp.concatenate([jnp.broadcast_to(km[h:h + 1], (rq, A_HEAD_DIM)) for h in range(nh)], axis=0)
                  for km in kmean]
    g = [jnp.sum(q * kr, axis=-1, keepdims=True) for kr in kmean_rows]
    kblk = [jnp.concatenate([ka, kb], axis=0).astype(BF16) for ka, kb in kpages]
    vblk = [jnp.concatenate([v_refs[2 * jb][...], v_refs[2 * jb + 1][...]], axis=0).astype(BF16)
            for jb in range(nbs)]
    qb = q.astype(BF16)
    s = [lax.dot_general(qb, kb_, NT_DIMS, preferred_element_type=F32) * scale + mb_ref[...] for kb_ in kblk]
    m = [s_.max(axis=-1, keepdims=True) for s_ in s]
    p = [jnp.exp(s_ - m_) for s_, m_ in zip(s, m)]
    l = [jnp.sum(p_, axis=-1, keepdims=True) for p_ in p]
    o = [jnp.dot(p_.astype(BF16), vb_, preferred_element_type=F32) for p_, vb_ in zip(p, vblk)]
    g_all, m_all, l_all = g_acc[...], m_acc[...], l_acc[...]
    for jb, blk in enumerate(blk_ids):
        here = lane == blk
        g_all = jnp.where(here, g[jb], g_all)
        m_all = jnp.where(here, m[jb] - sl_ref[:, 0:1] * (past - blk * A_BLOCK).astype(F32), m_all)
        l_all = jnp.where(here, l[jb], l_all)
        o_scr[blk] = o[jb]
    g_acc[...] = g_all
    m_acc[...] = m_all
    l_acc[...] = l_all

    @pl.when(n == nblk // nbs - 1)
    def _():
        gm = jnp.where(lane < nblk, g_acc[...], -jnp.inf)
        sel = jnp.zeros(gm.shape, F32)
        for i in range(nblk):
            gi = gm[:, i:i + 1]
            beats = (gm > gi) | ((gm == gi) & (lane < i))
            cnt = jnp.sum(jnp.where(beats, 1.0, 0.0), axis=-1, keepdims=True)
            sel = jnp.where(lane == i, jnp.where(cnt < A_TOPK, 1.0, 0.0), sel)
        selb = sel > 0.5
        mblk = m_acc[...]
        s_own = lax.dot_general(qb, kn_ref[0].astype(BF16), NT_DIMS, preferred_element_type=F32) * scale + ob_ref[...]
        mx = jnp.maximum(s_own.max(axis=-1, keepdims=True),
                         jnp.where(selb, mblk, NEG).max(axis=-1, keepdims=True))
        p_own = jnp.exp(s_own - mx)
        wgt = jnp.where(selb, jnp.exp(mblk - mx), 0.0)
        den = jnp.sum(p_own, axis=-1, keepdims=True) + jnp.sum(wgt * l_acc[...], axis=-1, keepdims=True)
        num = jnp.dot(p_own.astype(BF16), vn_ref[0].astype(BF16), preferred_element_type=F32)
        for i in range(nblk):
            num = num + wgt[:, i:i + 1] * o_scr[i]
        o_ref[0] = num / den


def moba_sample(q, k_new, v_new, cache_k, cache_v, layer, page_table, slopes):
    db, tn, w = q.shape
    nh = w // A_HEAD_DIM
    n_pages = page_table.shape[1]
    past = n_pages * PAGE_SIZE
    ppb = A_BLOCK // PAGE_SIZE
    rq = SUBLANES
    assert past % A_BLOCK == 0 and ppb == 2 and tn <= rq and tn * nh <= LANES and nh == SUBLANES
    nblk = past // A_BLOCK
    assert A_TOPK <= nblk <= LANES
    rows = nh * rq
    qr = jnp.pad(q.reshape(db, tn, nh, A_HEAD_DIM).transpose(0, 2, 1, 3), ((0, 0), (0, 0), (0, rq - tn), (0, 0)))
    qr = qr.reshape(db, rows, A_HEAD_DIM)
    new_rows = lambda z: jnp.pad(z.reshape(db, tn * nh, A_HEAD_DIM), ((0, 0), (0, LANES - tn * nh), (0, 0)))
    r_h = jnp.arange(rows, dtype=jnp.int32)[:, None] // rq
    r_t = jnp.arange(rows, dtype=jnp.int32)[:, None] % rq
    slope_r = slopes[r_h[:, 0]][:, None]
    col = jnp.arange(A_BLOCK * nh, dtype=jnp.int32)[None, :]
    mb = jnp.where(col % nh == r_h, -slope_r * (r_t - col // nh).astype(F32), NEG)
    colo = jnp.arange(LANES, dtype=jnp.int32)[None, :]
    jo = colo // nh
    ob = jnp.where((colo % nh == r_h) & (jo <= r_t) & (jo < tn), -slope_r * (r_t - jo).astype(F32), NEG)
    sl = jnp.broadcast_to(slope_r, (rows, LANES))

    nbs = 2 if nblk % 2 == 0 else 1
    npg = ppb * nbs

    def page_spec(j):
        return pl.BlockSpec((None, None, PAGE_SIZE * nh, A_HEAD_DIM),
                            lambda i, n, pt: (layer, pt[i, npg * n + j], 0, 0))

    const = lambda shape: pl.BlockSpec(shape, lambda i, n, pt: (0,) * len(shape))
    per_seq = lambda r: pl.BlockSpec((1, r, A_HEAD_DIM), lambda i, n, pt: (i, 0, 0))
    pages = [page_spec(j) for j in range(npg)]
    out = pl.pallas_call(
        functools.partial(_moba_sample_kernel, nh=nh, nblk=nblk, past=past),
        out_shape=jax.ShapeDtypeStruct((db, rows, A_HEAD_DIM), F32),
        grid_spec=pltpu.PrefetchScalarGridSpec(
            num_scalar_prefetch=1,
            grid=(db, nblk // nbs),
            in_specs=[per_seq(rows), const((rows, LANES)), const((rows, A_BLOCK * nh)), const((rows, LANES)),
                      per_seq(LANES), per_seq(LANES)] + pages + pages,
            out_specs=per_seq(rows),
            scratch_shapes=[pltpu.VMEM((rows, LANES), F32)] * 3 + [pltpu.VMEM((nblk, rows, A_HEAD_DIM), F32)],
        ),
        compiler_params=_cparams(("parallel", "arbitrary")),
        name="moba_sample",
    )(page_table, qr, sl, mb, ob, new_rows(k_new), new_rows(v_new), *([cache_k] * npg), *([cache_v] * npg))
    out = out.reshape(db, nh, rq, A_HEAD_DIM)[:, :, :tn].transpose(0, 2, 1, 3)
    return out.reshape(db, tn, w)


def _head_sum_matrix():
    i = lax.broadcasted_iota(jnp.int32, (LANES, LANES), 0)
    j = lax.broadcasted_iota(jnp.int32, (LANES, LANES), 1)
    return jnp.where(lax.shift_right_logical(i, 6) == lax.shift_right_logical(j, 6), 1.0, 0.0).astype(F32)


def _rwkv_prep_kernel(u_ref, pv_ref, s0_ref, mu_ref, w0_ref, w2_ref, a0_ref, a2_ref, g2_ref, kk_ref, ka_ref,
                      r_o, lw_o, k_o, v_o, kk_o, a_o, g_o, *, rw):
    t = pl.program_id(1)
    u = u_ref[0]
    tt = u.shape[0]
    prev_row = jnp.where(t == 0, s0_ref[0], pv_ref[0, SUBLANES - 1:SUBLANES, :])
    row = lax.broadcasted_iota(jnp.int32, u.shape, 0)
    prev = jnp.where(row == 0, prev_row, pltpu.roll(u, 1, axis=0))
    xs = u + (prev - u) * mu_ref[...]
    r = xs[:, 0:rw]
    k = xs[:, rw:2 * rw]
    v = xs[:, 2 * rw:3 * rw]
    lora_wa = xs[:, 3 * rw:3 * rw + LANES]
    gd = xs[:, 3 * rw + LANES:3 * rw + 2 * LANES]
    z = w0_ref[...] + jnp.dot(jnp.tanh(lora_wa), w2_ref[...], precision=HI, preferred_element_type=F32)
    nz = -z
    softplus = jnp.maximum(nz, 0.0) + jnp.log1p(jnp.exp(-jnp.abs(nz)))
    w_log = -softplus - 0.5
    lw_o[0] = -jnp.exp(w_log)
    a = jax.nn.sigmoid(a0_ref[...] + jnp.dot(lora_wa, a2_ref[...], precision=HI, preferred_element_type=F32))
    g_o[0] = jnp.dot(jax.nn.sigmoid(gd), g2_ref[...], precision=HI, preferred_element_type=F32)
    kk = k * kk_ref[...]
    seg = _head_sum_matrix()
    for p in range(rw // LANES):
        ps = slice(p * LANES, (p + 1) * LANES)
        kkp = kk[:, ps]
        ss = jnp.dot(kkp * kkp, seg, precision=HI, preferred_element_type=F32)
        kk_o[0, :, ps] = kkp / jnp.maximum(jnp.sqrt(ss), 1e-12)
    r_o[0] = r
    k_o[0] = k * (1.0 + (a - 1.0) * ka_ref[...])
    v_o[0] = v
    a_o[0] = a


def rwkv_prep(ur, shift0, prm):
    b, t, _ = ur.shape
    cols = prm['rw_mu'].shape[-1]
    rw = prm['rw_w0'].shape[-1]
    tt = _row_tile(t, 256)
    nlora = prm['rw_w2'].shape[0]
    assert 2 * nlora == LANES and prm['rw_g2'].shape[0] == LANES and cols == 3 * rw + 2 * LANES
    w2p = jnp.concatenate([prm['rw_w2'], jnp.zeros_like(prm['rw_w2'])], axis=0)
    a2p = jnp.concatenate([jnp.zeros_like(prm['rw_a2']), prm['rw_a2']], axis=0)
    row = lambda z: z.reshape(1, -1)
    full = lambda shape: pl.BlockSpec(shape, lambda i, j: (0,) * len(shape))
    outs = pl.pallas_call(
        functools.partial(_rwkv_prep_kernel, rw=rw),
        out_shape=[jax.ShapeDtypeStruct((b, t, rw), F32)] * 7,
        grid=(b, t // tt),
        in_specs=[
            pl.BlockSpec((1, tt, cols), lambda i, j: (i, j, 0)),
            pl.BlockSpec((1, SUBLANES, cols), lambda i, j: (i, jnp.maximum(j * (tt // SUBLANES) - 1, 0), 0)),
            pl.BlockSpec((1, 1, cols), lambda i, j: (i, 0, 0)),
            full((1, cols)), full((1, rw)), full((LANES, rw)), full((1, rw)), full((LANES, rw)),
            full((LANES, rw)), full((1, rw)), full((1, rw)),
        ],
        out_specs=[pl.BlockSpec((1, tt, rw), lambda i, j: (i, j, 0))] * 7,
        compiler_params=_cparams(("parallel", "parallel")),
        name="rwkv_prep",
    )(ur, ur, shift0.reshape(b, 1, cols), row(prm['rw_mu']), row(prm['rw_w0']), w2p, row(prm['rw_a0']), a2p,
      prm['rw_g2'], row(prm['rw_kk']), row(prm['rw_ka']))
    return outs


def _pair_masks():
    i = lax.broadcasted_iota(jnp.int32, (LANES, LANES), 0)
    j = lax.broadcasted_iota(jnp.int32, (LANES, LANES), 1)
    same = lax.shift_right_logical(i, 6) == lax.shift_right_logical(j, 6)
    return i, j, same


def _split(x):
    hi = x.astype(BF16)
    return hi, (x - hi.astype(F32)).astype(BF16)


def _mm3(a, b):
    n = b[0].shape[1]
    lhs = jnp.concatenate(a, axis=1)
    rhs = jnp.concatenate([jnp.concatenate(b, axis=1),
                           jnp.concatenate([b[0], jnp.zeros_like(b[0])], axis=1)], axis=0)
    out = jnp.dot(lhs, rhs, preferred_element_type=F32)
    return out[:, :n] + out[:, n:]


def _mm3_nt(a, b):
    n = b[0].shape[0]
    lhs = jnp.concatenate(a, axis=1)
    rhs = jnp.concatenate([jnp.concatenate([b[0], b[0]], axis=1),
                           jnp.concatenate([b[1], jnp.zeros_like(b[1])], axis=1)], axis=0)
    out = lax.dot_general(lhs, rhs, NT_DIMS, preferred_element_type=F32)
    return out[:, :n] + out[:, n:]


def _each(f, *lists):
    return [f(*xs) for xs in zip(*lists)]


def _rwkv_chunk_factors(r, lw, k, v, kk, a):
    c = R_CHUNK
    row = lax.broadcasted_iota(jnp.int32, (c, LANES), 0)
    cum = lw
    s = 1
    while s < c:
        cum = _each(lambda z: z + jnp.where(row >= s, pltpu.roll(z, s, axis=0), 0.0), cum)
        s *= 2
    cl = _each(lambda z: z[c - 1:c, :], cum)
    beta = _each(lambda x, y: x * y, kk, a)
    kap_t = _each(lambda x, cu, l: x * jnp.exp(cu - l), kk, cum, lw)
    r_t = _each(lambda x, cu: x * jnp.exp(cu), r, cum)
    e_inv = _each(lambda cu: jnp.exp(-cu), cum)
    b_t = _each(lambda x, e: x * e, beta, e_inv)
    k_t = _each(lambda x, e: x * e, k, e_inv)
    e_end = _each(lambda l, cu: jnp.exp(l - cu), cl, cum)
    b_h = _each(lambda x, e: x * e, beta, e_end)
    k_h = _each(lambda x, e: x * e, k, e_end)

    lo = lax.broadcasted_iota(jnp.int32, (c, LANES), 1) < R_HEAD_DIM
    stack = lambda x: jnp.concatenate([jnp.where(lo, x, 0.0), jnp.where(lo, 0.0, x)], axis=0)
    dup = lambda x: jnp.concatenate([x, x], axis=0)
    split_of = lambda f: (lambda x: _split(f(x)))
    ident = lambda x: x

    i, j, same = _pair_masks()
    strict = same & (j < i)
    incl = same & (j <= i)
    kap_s, r_s, v_s = _each(stack, kap_t), _each(stack, r_t), _each(stack, v)
    kap_p, r_p, v_p = _each(_split, kap_s), _each(_split, r_s), _each(_split, v_s)
    b_d, k_d = _each(split_of(dup), b_t), _each(split_of(dup), k_t)
    a_ab = _each(lambda x, y: jnp.where(strict, _mm3_nt(x, y), 0.0), kap_p, b_d)
    a_ak = _each(lambda x, y: jnp.where(strict, _mm3_nt(x, y), 0.0), kap_p, k_d)
    a_rb = _each(lambda x, y: jnp.where(incl, _mm3_nt(x, y), 0.0), r_p, b_d)
    a_rk = _each(lambda x, y: jnp.where(incl, _mm3_nt(x, y), 0.0), r_p, k_d)

    x = [jnp.where(i == j, 1.0, 0.0).astype(F32)] * len(r)
    s = 1
    while s < c:
        low = (lax.shift_right_logical(i, s.bit_length()) == lax.shift_right_logical(j, s.bit_length())) \
            & ((i & (2 * s - 1)) >= s) & ((j & (2 * s - 1)) < s)
        xp = _each(_split, x)
        ms = _each(lambda z: _split(jnp.where(low, z, 0.0)), a_ab)
        t1 = _each(split_of(ident), _each(_mm3, xp, ms))
        x = _each(lambda z, u, w_: z - _mm3(u, w_), x, t1, xp)
        s *= 2

    aakv = _each(_mm3, _each(_split, a_ak), v_p)
    arkv = _each(_mm3, _each(_split, a_rk), v_p)
    bh_p = _each(split_of(stack), b_h)
    vtk = _each(_mm3, _each(lambda z: _split(z.T), v_s), _each(split_of(stack), k_h))
    rhs = _each(lambda z, u: _split(jnp.concatenate([z, u], axis=1)), kap_s, aakv)
    wu = _each(lambda z, u: -_mm3(z, u), _each(_split, x), rhs)
    w = _each(lambda z: z[:, :LANES], wu)
    upre = _each(lambda z: z[:, LANES:], wu)
    arb_p = _each(_split, a_rb)
    r2 = _each(lambda z, u, w_: z + _mm3(u, _split(w_)), r_s, arb_p, w)
    ypre = _each(lambda z, u, w_: z + _mm3(u, _split(w_)), arkv, arb_p, upre)
    g = _each(lambda l, w_, u: jnp.where(i == j, jnp.exp(l), 0.0) + _mm3(_split(w_.T), u), cl, w, bh_p)
    spre = _each(lambda z, w_, u: z + _mm3(_split(w_.T), u), vtk, upre, bh_p)
    return list(zip(r2, ypre, g, spre))


def _rwkv_core_kernel(r_ref, lw_ref, k_ref, v_ref, kk_ref, a_ref, s0_ref, y_o, st_o, s_scr, *, nch, npair):
    cidx = pl.program_id(1)

    @pl.when(cidx == 0)
    def _():
        s_scr[...] = s0_ref[0]

    c = R_CHUNK
    parts = lambda ref: [ref[0, ch * c:(ch + 1) * c, p * LANES:(p + 1) * LANES]
                         for ch in range(nch) for p in range(npair)]
    factors = _rwkv_chunk_factors(parts(r_ref), parts(lw_ref), parts(k_ref), parts(v_ref), parts(kk_ref),
                                  parts(a_ref))
    s = [s_scr[p] for p in range(npair)]
    for ch in range(nch):
        for p in range(npair):
            r2, ypre, g, spre = factors[ch * npair + p]
            sp = _split(s[p])
            ys = _mm3_nt(_split(r2), sp) + ypre
            y_o[0, ch * c:(ch + 1) * c, p * LANES:(p + 1) * LANES] = ys[:c] + ys[c:]
            s[p] = _mm3(sp, _split(g)) + spre
    for p in range(npair):
        s_scr[p] = s[p]
        st_o[0, p] = s[p]


def rwkv_scan(r, lw, k, v, kk, a, wkv0):
    b, t, rw = r.shape
    npair = rw // LANES
    assert t % R_CHUNK == 0 and 2 * R_HEAD_DIM == LANES
    nch = 2 if t % (2 * R_CHUNK) == 0 else 1
    tt = nch * R_CHUNK
    w4 = wkv0.reshape(b, npair, 2, R_HEAD_DIM, R_HEAD_DIM)
    zero = jnp.zeros_like(w4[:, :, 0])
    s0 = jnp.concatenate([jnp.concatenate([w4[:, :, 0], zero], axis=-1),
                          jnp.concatenate([zero, w4[:, :, 1]], axis=-1)], axis=-2)
    act = pl.BlockSpec((1, tt, rw), lambda i, c: (i, c, 0))
    st_spec = pl.BlockSpec((1, npair, LANES, LANES), lambda i, c: (i, 0, 0, 0))
    y, st = pl.pallas_call(
        functools.partial(_rwkv_core_kernel, nch=nch, npair=npair),
        out_shape=[jax.ShapeDtypeStruct((b, t, rw), F32), jax.ShapeDtypeStruct((b, npair, LANES, LANES), F32)],
        grid=(b, t // tt),
        in_specs=[act] * 6 + [st_spec],
        out_specs=[act, st_spec],
        scratch_shapes=[pltpu.VMEM((npair, LANES, LANES), F32)],
        compiler_params=_cparams(("parallel", "arbitrary")),
        name="rwkv_core",
    )(r, lw, k, v, kk, a, s0)
    wkv = jnp.stack([st[:, :, :R_HEAD_DIM, :R_HEAD_DIM], st[:, :, R_HEAD_DIM:, R_HEAD_DIM:]], axis=2)
    return y, wkv.reshape(wkv0.shape)


def _rwkv_post_kernel(y_ref, r_ref, k_ref, v_ref, g_ref, rk_ref, lw_ref, lb_ref, o_ref):
    seg = _head_sum_matrix()
    inv_n = 1.0 / R_HEAD_DIM
    for p in range(y_ref.shape[2] // LANES):
        ps = slice(p * LANES, (p + 1) * LANES)
        y = y_ref[0, :, ps]
        hsum = lambda z: jnp.dot(z, seg, precision=HI, preferred_element_type=F32)
        mean = hsum(y) * inv_n
        d = y - mean
        var = hsum(d * d) * inv_n
        yn = d * lax.rsqrt(var + R_GN_EPS) * lw_ref[:, ps] + lb_ref[:, ps]
        bonus = hsum(r_ref[0, :, ps] * k_ref[0, :, ps] * rk_ref[:, ps]) * v_ref[0, :, ps]
        o_ref[0, :, ps] = (yn + bonus) * g_ref[0, :, ps]


def rwkv_post(y, r, k, v, g, prm):
    b, t, rw = y.shape
    tt = _row_tile(t, 256)
    act = pl.BlockSpec((1, tt, rw), lambda i, j: (i, j, 0))
    par = pl.BlockSpec((1, rw), lambda i, j: (0, 0))
    return pl.pallas_call(
        _rwkv_post_kernel,
        out_shape=jax.ShapeDtypeStruct((b, t, rw), F32),
        grid=(b, t // tt),
        in_specs=[act] * 5 + [par] * 3,
        out_specs=act,
        compiler_params=_cparams(("parallel", "parallel")),
        name="rwkv_post",
    )(y, r, k, v, g, prm['rw_rk'].reshape(1, rw), prm['rw_lnx_w'].reshape(1, rw), prm['rw_lnx_b'].reshape(1, rw))


def rwkv_mix(ur, shift0, wkv0, prm):
    b, t, _ = ur.shape
    tp = -(-t // SUBLANES) * SUBLANES
    urp = jnp.pad(ur, ((0, 0), (0, tp - t), (0, 0)))
    r, lw, k, v, kk, a, g = rwkv_prep(urp, shift0, prm)
    tc = -(-t // R_CHUNK) * R_CHUNK
    fit = lambda z: jnp.pad(z[:, :t], ((0, 0), (0, tc - t), (0, 0)))
    r, lw, k, v, kk, a, g = (fit(z) for z in (r, lw, k, v, kk, a, g))
    y, wkv = rwkv_scan(r, lw, k, v, kk, a, wkv0)
    out = rwkv_post(y, r, k, v, g, prm)
    return out[:, :t], wkv, ur[:, -1, :prm['rw_mu'].shape[-1]]


def _pool_kernel(u_ref, pv_ref, p0_ref, w_ref, sc_ref, o_ref, *, pos0):
    t = pl.program_id(1)
    cur = u_ref[0]
    tt = cur.shape[0]
    ext = jnp.concatenate([jnp.where(t == 0, p0_ref[0], pv_ref[0]), cur], axis=0)
    sums = []
    s = ext
    w = 1
    while w < POOL_MAX:
        s = s + pltpu.roll(s, w, axis=0)
        w *= 2
        sums.append(s)
    pos = pos0 + t * tt + lax.broadcasted_iota(jnp.int32, (tt, 1), 0)
    for gi, w in enumerate(POOL_WINDOWS):
        gs = slice(gi * LANES, (gi + 1) * LANES)
        win = sums[w.bit_length() - 2][POOL_MAX:, gs]
        cnt = jnp.minimum(pos + 1, w).astype(F32)
        m = win / cnt - cur[:, gs]
        z = jnp.dot(m.astype(BF16), w_ref[gi], preferred_element_type=F32)
        o_ref[0, :, gs] = z * sc_ref[:, gs]


def pool_mix(uc, pool0, pos0, pool_w, pool_scale):
    b, t, width = uc.shape
    assert width == len(POOL_WINDOWS) * LANES and pool_w.shape[1] == LANES
    tp = -(-t // SUBLANES) * SUBLANES
    ucp = jnp.pad(uc, ((0, 0), (0, tp - t), (0, 0)))
    tt = _row_tile(tp, 256)
    assert tt % POOL_MAX == 0 or tp == tt
    p0 = jnp.pad(pool0, ((0, 0), (1, 0), (0, 0)))
    nprev = tt // POOL_MAX if tt % POOL_MAX == 0 else 0
    ext = jnp.concatenate([pool0, uc], axis=1)
    if nprev:
        pv, pv_spec = ucp, pl.BlockSpec((1, POOL_MAX, width), lambda i, j: (i, jnp.maximum(j * nprev - 1, 0), 0))
    else:
        pv, pv_spec = p0, pl.BlockSpec((1, POOL_MAX, width), lambda i, j: (i, 0, 0))
    z = pl.pallas_call(
        functools.partial(_pool_kernel, pos0=pos0),
        out_shape=jax.ShapeDtypeStruct((b, tp, width), F32),
        grid=(b, tp // tt),
        in_specs=[
            pl.BlockSpec((1, tt, width), lambda i, j: (i, j, 0)),
            pv_spec,
            pl.BlockSpec((1, POOL_MAX, width), lambda i, j: (i, 0, 0)),
            pl.BlockSpec(pool_w.shape, lambda i, j: (0, 0, 0)),
            pl.BlockSpec((1, width), lambda i, j: (0, 0)),
        ],
        out_specs=pl.BlockSpec((1, tt, width), lambda i, j: (i, j, 0)),
        compiler_params=_cparams(("parallel", "parallel")),
        name="pool_mix",
    )(ucp, pv, p0, pool_w.astype(BF16), pool_scale.reshape(1, width))
    return z[:, :t], ext[:, -(POOL_MAX - 1):]


def _mix_out_kernel(x_ref, ya_ref, yr_ref, yc_ref, wa_ref, wr_ref, wc_ref, o_ref):
    acc = jnp.dot(ya_ref[...].astype(BF16), wa_ref[...], preferred_element_type=F32)
    acc += jnp.dot(yr_ref[...].astype(BF16), wr_ref[...], preferred_element_type=F32)
    acc += jnp.dot(yc_ref[...].astype(BF16), wc_ref[...], preferred_element_type=F32)
    o_ref[...] = x_ref[...] + acc


def mix_out(x, ya, yr, yc, w_out, layer):
    n, d = x.shape
    wa_, wr_, wc_ = ya.shape[1], yr.shape[1], yc.shape[1]
    assert wa_ % wr_ == 0 and wr_ == wc_
    tm = _row_tile(n, 512)
    tn = d
    act = lambda wd: pl.BlockSpec((tm, wd), lambda i, j: (i, 0))
    wsp = lambda wd, blk: pl.BlockSpec((None, wd, tn), lambda i, j: (layer, blk, j))
    return pl.pallas_call(
        _mix_out_kernel,
        out_shape=jax.ShapeDtypeStruct((n, d), F32),
        grid=(n // tm, d // tn),
        in_specs=[pl.BlockSpec((tm, tn), lambda i, j: (i, j)), act(wa_), act(wr_), act(wc_),
                  wsp(wa_, 0), wsp(wr_, wa_ // wr_), wsp(wc_, (wa_ + wr_) // wc_)],
        out_specs=pl.BlockSpec((tm, tn), lambda i, j: (i, j)),
        compiler_params=_cparams(("parallel", "arbitrary")),
        name="mix_out",
    )(x, ya, yr, yc, w_out, w_out, w_out)


def _cross_kernel(x_ref, g_ref, wq_ref, mk_ref, mv_ref, wo_ref, o_ref, *, nh):
    x = x_ref[0]
    h = _rms(x, g_ref[...]).astype(BF16)
    q = jnp.dot(h, wq_ref[...], preferred_element_type=F32)
    scale = M_HEAD_DIM ** -0.5
    outs = []
    for hh in range(nh):
        hs = slice(hh * M_HEAD_DIM, (hh + 1) * M_HEAD_DIM)
        s = lax.dot_general(q[:, hs].astype(BF16), mk_ref[0, :, hs].astype(BF16), NT_DIMS,
                            preferred_element_type=F32) * scale
        p = jnp.exp(s - s.max(axis=-1, keepdims=True))
        l = jnp.sum(p, axis=-1, keepdims=True)
        outs.append(jnp.dot(p.astype(BF16), mv_ref[0, :, hs].astype(BF16), preferred_element_type=F32) / l)
    o = jnp.concatenate(outs, axis=-1).astype(BF16)
    o_ref[0] = x + jnp.dot(o, wo_ref[...], preferred_element_type=F32)


def cross_attend(x, g, wq, mk, mv, wo):
    b, t0, d = x.shape
    t = -(-t0 // SUBLANES) * SUBLANES
    x = jnp.pad(x, ((0, 0), (0, t - t0), (0, 0)))
    mw = wq.shape[1]
    nm = mk.shape[1]
    tm = _row_tile(t, 512)
    out = pl.pallas_call(
        functools.partial(_cross_kernel, nh=mw // M_HEAD_DIM),
        out_shape=jax.ShapeDtypeStruct((b, t, d), F32),
        grid=(b, t // tm),
        in_specs=[
            pl.BlockSpec((1, tm, d), lambda i, j: (i, j, 0)),
            pl.BlockSpec((1, d), lambda i, j: (0, 0)),
            pl.BlockSpec((d, mw), lambda i, j: (0, 0)),
            pl.BlockSpec((1, nm, mw), lambda i, j: (i, 0, 0)),
            pl.BlockSpec((1, nm, mw), lambda i, j: (i, 0, 0)),
            pl.BlockSpec((mw, d), lambda i, j: (0, 0)),
        ],
        out_specs=pl.BlockSpec((1, tm, d), lambda i, j: (i, j, 0)),
        compiler_params=_cparams(("parallel", "parallel")),
        name="cross_attend",
    )(x, g.reshape(1, d), wq, mk, mv, wo)
    return out[:, :t0]


def _layer(x, prm, moba_fn, pos0, shift0, wkv0, pool0, mem_k, mem_v):
    b, t, d = x.shape
    n = b * t
    lyr, stk = prm['layer'], prm['stacked']
    x2 = ffn_half_step(x.reshape(n, d), prm['norm_ffn1'], stk['ffn1_gate'], stk['ffn1_up'], stk['ffn1_down'], lyr)
    aw, rc, cw = prm['a_width'], prm['r_cols'], prm['c_width']
    qa, ka, va, ur, uc = (z.reshape(b, t, -1) for z in
                          in_proj(x2, prm['norm_mix'], stk['w_in'], lyr, (aw, aw, aw, rc, cw)))
    ya = moba_fn(qa, ka, va)
    yr, wkv, shift = rwkv_mix(ur, shift0, wkv0, prm)
    yc, pool_buf = pool_mix(uc, pool0, pos0, prm['pool_w'], prm['pool_scale'])
    x3 = mix_out(x2, ya.reshape(n, aw), yr.reshape(n, -1), yc.reshape(n, cw), stk['w_out'], lyr)
    x4 = cross_attend(x3.reshape(b, t, d), prm['norm_cross'], prm['mem_wq'], mem_k, mem_v, prm['mem_wo'])
    x5 = ffn_half_step(x4.reshape(n, d), prm['norm_ffn2'], stk['ffn2_gate'], stk['ffn2_up'], stk['ffn2_down'], lyr)
    return x5.reshape(b, t, d), ka, va, wkv, shift, pool_buf


def kernel(x_prompt, x_sample, cache_k, cache_v, cache_mem_k, cache_mem_v, state_wkv, state_shift, state_pool, page_table, mem_prompt, norm_ffn1, ffn1_gate, ffn1_up, ffn1_down, norm_mix, w_in, w_out, rw_mu, rw_w0, rw_w2, rw_a0, rw_a2, rw_g2, rw_kk, rw_ka, rw_rk, rw_lnx_w, rw_lnx_b, pool_w, pool_scale, norm_cross, norm_mem, mem_wq, mem_wk, mem_wv, mem_wo, norm_ffn2, ffn2_gate, ffn2_up, ffn2_down, norm_final):
    depth = w_in.shape[0]
    bp, tp, d = x_prompt.shape
    db, ts, _ = x_sample.shape
    n_heads, hd = cache_k.shape[3], cache_k.shape[4]
    aw = n_heads * hd
    r_heads, rn = rw_rk.shape[1], rw_rk.shape[2]
    r_cols = rw_mu.shape[1]
    c_width = pool_scale.shape[1]
    nm, m_heads, mhd = cache_mem_k.shape[2:]
    mw = m_heads * mhd
    assert hd == A_HEAD_DIM and rn == R_HEAD_DIM and mhd == M_HEAD_DIM
    past = page_table.shape[1] * PAGE_SIZE
    slopes = jnp.exp2(-8.0 * jnp.arange(1, n_heads + 1, dtype=F32) / n_heads)
    ck = cache_k.reshape(depth, cache_k.shape[1], PAGE_SIZE * n_heads, hd)
    cv = cache_v.reshape(depth, cache_v.shape[1], PAGE_SIZE * n_heads, hd)
    bf = lambda z: z.astype(BF16)
    bounds = (0, aw, 2 * aw, 3 * aw, 3 * aw + r_cols, w_in.shape[2])
    groups = [bf(w_in[:, :, lo:hi]) for lo, hi in zip(bounds[:-1], bounds[1:])]
    w_in_p = jnp.concatenate([jnp.pad(gw, ((0, 0), (0, 0), (0, -gw.shape[2] % IN_TILE))) for gw in groups], axis=2)
    stacked = dict(ffn1_gate=bf(ffn1_gate), ffn1_up=bf(ffn1_up), ffn1_down=bf(ffn1_down), w_in=w_in_p,
                   w_out=bf(w_out), ffn2_gate=bf(ffn2_gate), ffn2_up=bf(ffn2_up), ffn2_down=bf(ffn2_down))
    big = dict(mem_wq=bf(mem_wq), mem_wk=bf(mem_wk), mem_wv=bf(mem_wv), mem_wo=bf(mem_wo))
    small = dict(norm_ffn1=norm_ffn1, norm_mix=norm_mix, rw_mu=rw_mu, rw_w0=rw_w0, rw_w2=rw_w2, rw_a0=rw_a0,
                 rw_a2=rw_a2, rw_g2=rw_g2, rw_kk=rw_kk, rw_ka=rw_ka, rw_rk=rw_rk.reshape(depth, -1),
                 rw_lnx_w=rw_lnx_w, rw_lnx_b=rw_lnx_b, pool_w=pool_w, pool_scale=pool_scale,
                 norm_cross=norm_cross, norm_ffn2=norm_ffn2)

    shift0 = jnp.zeros((bp, r_cols), F32)
    wkv0 = jnp.zeros((bp, r_heads, rn, rn), F32)
    pool0 = jnp.zeros((bp, POOL_MAX - 1, c_width), F32)
    xp, xs = x_prompt, x_sample
    outs = [[] for _ in range(12)]
    for l in range(depth):
        prm = {k: v[l] for k, v in big.items()}
        prm.update({k: v[l] for k, v in small.items()})
        prm.update(a_width=aw, r_cols=r_cols, c_width=c_width, layer=l, stacked=stacked)
        hm = rmsnorm(mem_prompt.reshape(bp * nm, d), norm_mem[l], BF16)
        mk = matmul(hm, prm['mem_wk'], mw).reshape(bp, nm, mw)
        mv = matmul(hm, prm['mem_wv'], mw).reshape(bp, nm, mw)
        moba_p = functools.partial(moba_prompt, slopes=slopes)
        xp, k_, v_, w_, sh_, pl_ = _layer(xp, prm, moba_p, 0, shift0, wkv0, pool0, mk, mv)
        res_p = (k_.reshape(bp, tp, n_heads, hd), v_.reshape(bp, tp, n_heads, hd), w_, sh_, pl_,
                 mk.reshape(bp, nm, m_heads, mhd), mv.reshape(bp, nm, m_heads, mhd))
        moba_s = functools.partial(moba_sample, cache_k=ck, cache_v=cv, layer=l, page_table=page_table,
                                   slopes=slopes)
        xs, k_, v_, w_, sh_, pl_ = _layer(xs, prm, moba_s, past, state_shift[l], state_wkv[l], state_pool[l],
                                           cache_mem_k[l].reshape(db, nm, mw), cache_mem_v[l].reshape(db, nm, mw))
        res_s = (k_.reshape(db, ts, n_heads, hd), v_.reshape(db, ts, n_heads, hd), w_, sh_, pl_)
        for i, z in enumerate(res_p + res_s):
            outs[i].append(z)
    y_prompt = rmsnorm(xp.reshape(bp * tp, d), norm_final, F32).reshape(bp, tp, d)
    y_sample = rmsnorm(xs.reshape(db * ts, d), norm_final, F32).reshape(db, ts, d)
    return (y_prompt, y_sample) + tuple(jnp.stack(o) for o in outs)
```

```python
import functools

import jax
import jax.numpy as jnp
from jax import lax
from jax.experimental import pallas as pl
from jax.experimental.pallas import tpu as pltpu

F32 = jnp.float32
BF16 = jnp.bfloat16
HI = lax.Precision.HIGHEST

RMS_EPS = 1e-6
LANES = 128
SUBLANES = 8
VMEM_LIMIT = 48 * 1024 * 1024

PAGE_SIZE = 128
A_HEAD_DIM = 128
A_BLOCK = 256
A_TOPK = 3
A_QCHUNK = 128
R_HEAD_DIM = 64
R_CHUNK = 64
R_GN_EPS = 64e-5
POOL_WINDOWS = (2, 4, 8, 16)
POOL_MAX = 16
M_HEAD_DIM = 128
NEG = -1e30

NT_DIMS = (((1,), (1,)), ((), ()))
TN_DIMS = (((0,), (0,)), ((), ()))


VMEM_LIMIT_BIG = 58 * 1024 * 1024


def _cparams(sem, vmem_limit=VMEM_LIMIT):
    return pltpu.CompilerParams(dimension_semantics=sem, vmem_limit_bytes=vmem_limit)


def _row_tile(n, pref):
    return pref if n % pref == 0 else n


def _rms(x, g):
    ms = jnp.mean(x * x, axis=-1, keepdims=True)
    return x * lax.rsqrt(ms + RMS_EPS) * g


def _rms_kernel(x_ref, g_ref, o_ref):
    o_ref[...] = _rms(x_ref[...], g_ref[...]).astype(o_ref.dtype)


def rmsnorm(x, g, out_dtype):
    n, d = x.shape
    tm = _row_tile(n, 512)
    return pl.pallas_call(
        _rms_kernel,
        out_shape=jax.ShapeDtypeStruct((n, d), out_dtype),
        grid=(n // tm,),
        in_specs=[pl.BlockSpec((tm, d), lambda i: (i, 0)), pl.BlockSpec((1, d), lambda i: (0, 0))],
        out_specs=pl.BlockSpec((tm, d), lambda i: (i, 0)),
        compiler_params=_cparams(("parallel",)),
        name="rmsnorm",
    )(x, g.reshape(1, d))


def _matmul_kernel(a_ref, w_ref, o_ref):
    o_ref[...] = jnp.dot(a_ref[...], w_ref[...], preferred_element_type=F32)


def matmul(a, w, tn):
    n, k = a.shape
    m = w.shape[1]
    tm = _row_tile(n, 512)
    return pl.pallas_call(
        _matmul_kernel,
        out_shape=jax.ShapeDtypeStruct((n, m), F32),
        grid=(n // tm, m // tn),
        in_specs=[pl.BlockSpec((tm, k), lambda i, j: (i, 0)), pl.BlockSpec((k, tn), lambda i, j: (0, j))],
        out_specs=pl.BlockSpec((tm, tn), lambda i, j: (i, j)),
        compiler_params=_cparams(("parallel", "arbitrary")),
        name="matmul",
    )(a, w)


IN_TILE = 512


def _in_proj_kernel(x_ref, g_ref, w_ref, *rest, first_tile):
    out_refs, h_scr = rest[:-1], rest[-1]
    j = pl.program_id(1)

    @pl.when(j == 0)
    def _():
        h_scr[...] = _rms(x_ref[...], g_ref[...]).astype(BF16)

    for o_ref, lo, hi in zip(out_refs, first_tile[:-1], first_tile[1:]):
        @pl.when((j >= lo) & (j < hi))
        def _(o_ref=o_ref):
            o_ref[...] = jnp.dot(h_scr[...], w_ref[...], preferred_element_type=F32)


def in_proj(x, g, w, layer, widths):
    n, d = x.shape
    tm = _row_tile(n, 1024)
    tiles = [-(-wd // IN_TILE) for wd in widths]
    first_tile = [0]
    for nt in tiles:
        first_tile.append(first_tile[-1] + nt)
    assert w.shape[2] == first_tile[-1] * IN_TILE

    def out_spec(lo, nt):
        return pl.BlockSpec((tm, IN_TILE), lambda i, j: (i, jnp.clip(j - lo, 0, nt - 1)))

    return pl.pallas_call(
        functools.partial(_in_proj_kernel, first_tile=tuple(first_tile)),
        out_shape=[jax.ShapeDtypeStruct((n, nt * IN_TILE), F32) for nt in tiles],
        grid=(n // tm, first_tile[-1]),
        in_specs=[pl.BlockSpec((tm, d), lambda i, j: (i, 0)), pl.BlockSpec((1, d), lambda i, j: (0, 0)),
                  pl.BlockSpec((None, d, IN_TILE), lambda i, j: (layer, 0, j))],
        out_specs=[out_spec(lo, nt) for lo, nt in zip(first_tile[:-1], tiles)],
        scratch_shapes=[pltpu.VMEM((tm, d), BF16)],
        compiler_params=_cparams(("parallel", "arbitrary"), VMEM_LIMIT_BIG),
        name="in_proj",
    )(x, g.reshape(1, d), w)


FFN_OUT_CHUNK = 512


def _ffn_kernel(x_ref, g_ref, wg_ref, wu_ref, wd_ref, o_ref, h_scr):
    j = pl.program_id(1)

    @pl.when(j == 0)
    def _():
        x = x_ref[...]
        h_scr[...] = _rms(x, g_ref[...]).astype(BF16)
        o_ref[...] = 2.0 * x

    h = h_scr[...]
    gate = jnp.dot(h, wg_ref[...], preferred_element_type=F32)
    up = jnp.dot(h, wu_ref[...], preferred_element_type=F32)
    act = (gate * jax.nn.sigmoid(gate) * up).astype(BF16)
    for c0 in range(0, o_ref.shape[1], FFN_OUT_CHUNK):
        cols = slice(c0, c0 + FFN_OUT_CHUNK)
        o_ref[:, cols] += jnp.dot(act, wd_ref[:, cols], preferred_element_type=F32)

    @pl.when(j == pl.num_programs(1) - 1)
    def _():
        o_ref[...] = 0.5 * o_ref[...]


def ffn_half_step(x, g, wg, wu, wd, layer, tf=256):
    n, d = x.shape
    f = wg.shape[2]
    tm = _row_tile(n, 1024)
    assert d % FFN_OUT_CHUNK == 0
    return pl.pallas_call(
        _ffn_kernel,
        out_shape=jax.ShapeDtypeStruct((n, d), F32),
        grid=(n // tm, f // tf),
        in_specs=[
            pl.BlockSpec((tm, d), lambda i, j: (i, 0)),
            pl.BlockSpec((1, d), lambda i, j: (0, 0)),
            pl.BlockSpec((None, d, tf), lambda i, j: (layer, 0, j)),
            pl.BlockSpec((None, d, tf), lambda i, j: (layer, 0, j)),
            pl.BlockSpec((None, tf, d), lambda i, j: (layer, j, 0)),
        ],
        out_specs=pl.BlockSpec((tm, d), lambda i, j: (i, 0)),
        scratch_shapes=[pltpu.VMEM((tm, d), BF16)],
        compiler_params=_cparams(("parallel", "arbitrary"), VMEM_LIMIT_BIG),
        name="ffn_half_step",
    )(x, g.reshape(1, d), wg, wu, wd)


def _moba_prompt_kernel(slope_ref, q_ref, k_ref, v_ref, o_ref, kb_scr, vb_scr, km_scr, *, nb):
    h = pl.program_id(1)
    own = pl.program_id(2)
    qc = A_BLOCK
    hps, nbp = km_scr.shape[0], km_scr.shape[1]
    heads = range(hps)
    hsl = lambda i: slice(i * A_HEAD_DIM, (i + 1) * A_HEAD_DIM)

    @pl.when(own == 0)
    def _():
        k = k_ref[0]
        kb_scr[...] = k.astype(BF16)
        vb_scr[...] = v_ref[0].astype(BF16)
        km_scr[...] = jnp.zeros_like(km_scr)
        for i in heads:
            for n in range(nb):
                km_scr[i, n:n + 1, :] = jnp.sum(k[n * A_BLOCK:(n + 1) * A_BLOCK, hsl(i)], axis=0,
                                                keepdims=True) * (1.0 / A_BLOCK)

    q = [q_ref[0, :, hsl(i)] for i in heads]
    slope = [slope_ref[h * hps + i] for i in heads]
    scale = A_HEAD_DIM ** -0.5

    g = [lax.dot_general(km_scr[i], q[i], NT_DIMS, precision=HI, preferred_element_type=F32) for i in heads]
    blk = lax.broadcasted_iota(jnp.int32, (nbp, qc), 0)
    gm = [jnp.where(blk < own, g_, -jnp.inf) for g_ in g]
    sel_t = [jnp.zeros((nbp, qc), F32) for _ in heads]
    for n in range(nb):
        for i in heads:
            gn = gm[i][n:n + 1, :]
            beats = (gm[i] > gn) | ((gm[i] == gn) & (blk < n))
            cnt = jnp.sum(jnp.where(beats, 1.0, 0.0), axis=0, keepdims=True)
            sel_n = jnp.where(cnt < A_TOPK, 1.0, 0.0) * jnp.where(n < own, 1.0, 0.0)
            sel_t[i] = jnp.where(blk == n, sel_n, sel_t[i])
    sel_b = [st.astype(BF16) for st in sel_t]

    qb = [(q_ * scale).astype(BF16) for q_ in q]
    row = lax.broadcasted_iota(jnp.int32, (qc, A_BLOCK), 0)
    off_k = lax.broadcasted_iota(jnp.int32, (qc, A_BLOCK), 1)

    def attend(own_k):
        nk = own_k + 1
        s = [lax.dot_general(qb[i], kb_scr[0:nk * A_BLOCK, hsl(i)], NT_DIMS, preferred_element_type=F32)
             for i in heads]
        if own_k:
            blk_of_key = lax.broadcasted_iota(jnp.int32, (nbp, own_k * A_BLOCK), 1) // A_BLOCK
            spread = jnp.where(blk_of_key == lax.broadcasted_iota(jnp.int32, (nbp, own_k * A_BLOCK), 0),
                               1.0, 0.0).astype(BF16)
            picked = [lax.dot_general(sel_b[i], spread, TN_DIMS, preferred_element_type=F32) for i in heads]
        blocks = [[] for _ in heads]
        for n in range(nk):
            for i in heads:
                sn = s[i][:, n * A_BLOCK:(n + 1) * A_BLOCK] + slope[i] * (n * A_BLOCK + off_k[0:1, :]).astype(F32)
                if n < own_k:
                    allow = picked[i][:, n * A_BLOCK:(n + 1) * A_BLOCK] > 0.5
                else:
                    allow = off_k <= row
                blocks[i].append(jnp.where(allow, sn, NEG))
        m = [blocks[i][0].max(axis=-1, keepdims=True) for i in heads]
        for n in range(1, nk):
            m = [jnp.maximum(m[i], blocks[i][n].max(axis=-1, keepdims=True)) for i in heads]
        l = [jnp.zeros((qc, 1), F32) for _ in heads]
        o = [jnp.zeros((qc, A_HEAD_DIM), F32) for _ in heads]
        for n in range(nk):
            for i in heads:
                p = jnp.exp(blocks[i][n] - m[i])
                l[i] = l[i] + jnp.sum(p, axis=-1, keepdims=True)
                o[i] = o[i] + jnp.dot(p.astype(BF16), vb_scr[n * A_BLOCK:(n + 1) * A_BLOCK, hsl(i)],
                                      preferred_element_type=F32)
        for i in heads:
            o_ref[0, :, hsl(i)] = o[i] / l[i]

    for own_k in range(nb):
        pl.when(own == own_k)(functools.partial(attend, own_k))


def moba_prompt(q, k, v, slopes):
    b, t, w = q.shape
    nh = w // A_HEAD_DIM
    nb = t // A_BLOCK
    assert t % A_BLOCK == 0 and nb >= A_TOPK and nb <= LANES and A_BLOCK % A_QCHUNK == 0
    nbp = -(-nb // SUBLANES) * SUBLANES
    hps = 2 if nh % 2 == 0 else 1
    hw = hps * A_HEAD_DIM
    return pl.pallas_call(
        functools.partial(_moba_prompt_kernel, nb=nb),
        out_shape=jax.ShapeDtypeStruct((b, t, w), F32),
        grid=(b, nh // hps, nb),
        in_specs=[
            pl.BlockSpec(memory_space=pltpu.SMEM),
            pl.BlockSpec((1, A_BLOCK, hw), lambda i, h, c: (i, c, h)),
            pl.BlockSpec((1, t, hw), lambda i, h, c: (i, 0, h)),
            pl.BlockSpec((1, t, hw), lambda i, h, c: (i, 0, h)),
        ],
        out_specs=pl.BlockSpec((1, A_BLOCK, hw), lambda i, h, c: (i, c, h)),
        scratch_shapes=[
            pltpu.VMEM((t, hw), BF16),
            pltpu.VMEM((t, hw), BF16),
            pltpu.VMEM((hps, nbp, A_HEAD_DIM), F32),
        ],
        compiler_params=_cparams(("parallel", "parallel", "arbitrary")),
        name="moba_prompt",
    )(slopes, q, k, v)


def _moba_sample_kernel(pt_ref, q_ref, sl_ref, mb_ref, ob_ref, kn_ref, vn_ref, *rest, nh, nblk, past):
    page_refs, (o_ref, g_acc, m_acc, l_acc, o_scr) = rest[:-5], rest[-5:]
    n = pl.program_id(1)
    scale = A_HEAD_DIM ** -0.5
    rq = SUBLANES
    q = q_ref[0]
    lane = lax.broadcasted_iota(jnp.int32, (nh * rq, LANES), 1)

    @pl.when(n == 0)
    def _():
        g_acc[...] = jnp.zeros_like(g_acc)
        m_acc[...] = jnp.zeros_like(m_acc)
        l_acc[...] = jnp.zeros_like(l_acc)

    nbs = len(page_refs) // 4
    k_refs, v_refs = page_refs[:2 * nbs], page_refs[2 * nbs:]
    blk_ids = [n * nbs + jb for jb in range(nbs)]
    kpages = [(k_refs[2 * jb][...], k_refs[2 * jb + 1][...]) for jb in range(nbs)]
    head_sum = lambda pg: jnp.sum(pg.reshape(PAGE_SIZE, nh, A_HEAD_DIM), axis=0)
    kmean = [(head_sum(ka) + head_sum(kb)) * (1.0 / A_BLOCK) for ka, kb in kpages]
    kmean_rows = [jnp.concatenate([jnp.broadcast_to(km[h:h + 1], (rq, A_HEAD_DIM)) for h in range(nh)], axis=0)
                  for km in kmean]
    g = [jnp.sum(q * kr, axis=-1, keepdims=True) for kr in kmean_rows]
    kblk = [jnp.concatenate([ka, kb], axis=0).astype(BF16) for ka, kb in kpages]
    vblk = [jnp.concatenate([v_refs[2 * jb][...], v_refs[2 * jb + 1][...]], axis=0).astype(BF16)
            for jb in range(nbs)]
    qb = q.astype(BF16)
    s = [lax.dot_general(qb, kb_, NT_DIMS, preferred_element_type=F32) * scale + mb_ref[...] for kb_ in kblk]
    m = [s_.max(axis=-1, keepdims=True) for s_ in s]
    p = [jnp.exp(s_ - m_) for s_, m_ in zip(s, m)]
    l = [jnp.sum(p_, axis=-1, keepdims=True) for p_ in p]
    o = [jnp.dot(p_.astype(BF16), vb_, preferred_element_type=F32) for p_, vb_ in zip(p, vblk)]
    g_all, m_all, l_all = g_acc[...], m_acc[...], l_acc[...]
    for jb, blk in enumerate(blk_ids):
        here = lane == blk
        g_all = jnp.where(here, g[jb], g_all)
        m_all = jnp.where(here, m[jb] - sl_ref[:, 0:1] * (past - blk * A_BLOCK).astype(F32), m_all)
        l_all = jnp.where(here, l[jb], l_all)
        o_scr[blk] = o[jb]
    g_acc[...] = g_all
    m_acc[...] = m_all
    l_acc[...] = l_all

    @pl.when(n == nblk // nbs - 1)
    def _():
        gm = jnp.where(lane < nblk, g_acc[...], -jnp.inf)
        sel = jnp.zeros(gm.shape, F32)
        for i in range(nblk):
            gi = gm[:, i:i + 1]
            beats = (gm > gi) | ((gm == gi) & (lane < i))
            cnt = jnp.sum(jnp.where(beats, 1.0, 0.0), axis=-1, keepdims=True)
            sel = jnp.where(lane == i, jnp.where(cnt < A_TOPK, 1.0, 0.0), sel)
        selb = sel > 0.5
        mblk = m_acc[...]
        s_own = lax.dot_general(qb, kn_ref[0].astype(BF16), NT_DIMS, preferred_element_type=F32) * scale + ob_ref[...]
        mx = jnp.maximum(s_own.max(axis=-1, keepdims=True),
                         jnp.where(selb, mblk, NEG).max(axis=-1, keepdims=True))
        p_own = jnp.exp(s_own - mx)
        wgt = jnp.where(selb, jnp.exp(mblk - mx), 0.0)
        den = jnp.sum(p_own, axis=-1, keepdims=True) + jnp.sum(wgt * l_acc[...], axis=-1, keepdims=True)
        num = jnp.dot(p_own.astype(BF16), vn_ref[0].astype(BF16), preferred_element_type=F32)
        for i in range(nblk):
            num = num + wgt[:, i:i + 1] * o_scr[i]
        o_ref[0] = num / den


def moba_sample(q, k_new, v_new, cache_k, cache_v, layer, page_table, slopes):
    db, tn, w = q.shape
    nh = w // A_HEAD_DIM
    n_pages = page_table.shape[1]
    past = n_pages * PAGE_SIZE
    ppb = A_BLOCK // PAGE_SIZE
    rq = SUBLANES
    assert past % A_BLOCK == 0 and ppb == 2 and tn <= rq and tn * nh <= LANES and nh == SUBLANES
    nblk = past // A_BLOCK
    assert A_TOPK <= nblk <= LANES
    rows = nh * rq
    qr = jnp.pad(q.reshape(db, tn, nh, A_HEAD_DIM).transpose(0, 2, 1, 3), ((0, 0), (0, 0), (0, rq - tn), (0, 0)))
    qr = qr.reshape(db, rows, A_HEAD_DIM)
    new_rows = lambda z: jnp.pad(z.reshape(db, tn * nh, A_HEAD_DIM), ((0, 0), (0, LANES - tn * nh), (0, 0)))
    r_h = jnp.arange(rows, dtype=jnp.int32)[:, None] // rq
    r_t = jnp.arange(rows, dtype=jnp.int32)[:, None] % rq
    slope_r = slopes[r_h[:, 0]][:, None]
    col = jnp.arange(A_BLOCK * nh, dtype=jnp.int32)[None, :]
    mb = jnp.where(col % nh == r_h, -slope_r * (r_t - col // nh).astype(F32), NEG)
    colo = jnp.arange(LANES, dtype=jnp.int32)[None, :]
    jo = colo // nh
    ob = jnp.where((colo % nh == r_h) & (jo <= r_t) & (jo < tn), -slope_r * (r_t - jo).astype(F32), NEG)
    sl = jnp.broadcast_to(slope_r, (rows, LANES))

    nbs = max(n for n in (1, 2, 4) if nblk % n == 0)
    npg = ppb * nbs

    def page_spec(j):
        return pl.BlockSpec((None, None, PAGE_SIZE * nh, A_HEAD_DIM),
                            lambda i, n, pt: (layer, pt[i, npg * n + j], 0, 0))

    const = lambda shape: pl.BlockSpec(shape, lambda i, n, pt: (0,) * len(shape))
    per_seq = lambda r: pl.BlockSpec((1, r, A_HEAD_DIM), lambda i, n, pt: (i, 0, 0))
    pages = [page_spec(j) for j in range(npg)]
    out = pl.pallas_call(
        functools.partial(_moba_sample_kernel, nh=nh, nblk=nblk, past=past),
        out_shape=jax.ShapeDtypeStruct((db, rows, A_HEAD_DIM), F32),
        grid_spec=pltpu.PrefetchScalarGridSpec(
            num_scalar_prefetch=1,
            grid=(db, nblk // nbs),
            in_specs=[per_seq(rows), const((rows, LANES)), const((rows, A_BLOCK * nh)), const((rows, LANES)),
                      per_seq(LANES), per_seq(LANES)] + pages + pages,
            out_specs=per_seq(rows),
            scratch_shapes=[pltpu.VMEM((rows, LANES), F32)] * 3 + [pltpu.VMEM((nblk, rows, A_HEAD_DIM), F32)],
        ),
        compiler_params=_cparams(("parallel", "arbitrary")),
        name="moba_sample",
    )(page_table, qr, sl, mb, ob, new_rows(k_new), new_rows(v_new), *([cache_k] * npg), *([cache_v] * npg))
    out = out.reshape(db, nh, rq, A_HEAD_DIM)[:, :, :tn].transpose(0, 2, 1, 3)
    return out.reshape(db, tn, w)


def _head_sum_matrix():
    i = lax.broadcasted_iota(jnp.int32, (LANES, LANES), 0)
    j = lax.broadcasted_iota(jnp.int32, (LANES, LANES), 1)
    return jnp.where(lax.shift_right_logical(i, 6) == lax.shift_right_logical(j, 6), 1.0, 0.0).astype(F32)


def _rwkv_prep_kernel(u_ref, pv_ref, s0_ref, mu_ref, w0_ref, w2_ref, a0_ref, a2_ref, g2_ref, kk_ref, ka_ref,
                      r_o, lw_o, k_o, v_o, kk_o, a_o, g_o, *, rw):
    t = pl.program_id(1)
    u = u_ref[0]
    tt = u.shape[0]
    prev_row = jnp.where(t == 0, s0_ref[0], pv_ref[0, SUBLANES - 1:SUBLANES, :])
    row = lax.broadcasted_iota(jnp.int32, u.shape, 0)
    prev = jnp.where(row == 0, prev_row, pltpu.roll(u, 1, axis=0))
    xs = u + (prev - u) * mu_ref[...]
    r = xs[:, 0:rw]
    k = xs[:, rw:2 * rw]
    v = xs[:, 2 * rw:3 * rw]
    lora_wa = xs[:, 3 * rw:3 * rw + LANES]
    gd = xs[:, 3 * rw + LANES:3 * rw + 2 * LANES]
    z = w0_ref[...] + jnp.dot(jnp.tanh(lora_wa), w2_ref[...], precision=HI, preferred_element_type=F32)
    nz = -z
    softplus = jnp.maximum(nz, 0.0) + jnp.log1p(jnp.exp(-jnp.abs(nz)))
    w_log = -softplus - 0.5
    lw_o[0] = -jnp.exp(w_log)
    a = jax.nn.sigmoid(a0_ref[...] + jnp.dot(lora_wa, a2_ref[...], precision=HI, preferred_element_type=F32))
    g_o[0] = jnp.dot(jax.nn.sigmoid(gd), g2_ref[...], precision=HI, preferred_element_type=F32)
    kk = k * kk_ref[...]
    seg = _head_sum_matrix()
    for p in range(rw // LANES):
        ps = slice(p * LANES, (p + 1) * LANES)
        kkp = kk[:, ps]
        ss = jnp.dot(kkp * kkp, seg, precision=HI, preferred_element_type=F32)
        kk_o[0, :, ps] = kkp / jnp.maximum(jnp.sqrt(ss), 1e-12)
    r_o[0] = r
    k_o[0] = k * (1.0 + (a - 1.0) * ka_ref[...])
    v_o[0] = v
    a_o[0] = a


def rwkv_prep(ur, shift0, prm):
    b, t, _ = ur.shape
    cols = prm['rw_mu'].shape[-1]
    rw = prm['rw_w0'].shape[-1]
    tt = _row_tile(t, 256)
    nlora = prm['rw_w2'].shape[0]
    assert 2 * nlora == LANES and prm['rw_g2'].shape[0] == LANES and cols == 3 * rw + 2 * LANES
    w2p = jnp.concatenate([prm['rw_w2'], jnp.zeros_like(prm['rw_w2'])], axis=0)
    a2p = jnp.concatenate([jnp.zeros_like(prm['rw_a2']), prm['rw_a2']], axis=0)
    row = lambda z: z.reshape(1, -1)
    full = lambda shape: pl.BlockSpec(shape, lambda i, j: (0,) * len(shape))
    outs = pl.pallas_call(
        functools.partial(_rwkv_prep_kernel, rw=rw),
        out_shape=[jax.ShapeDtypeStruct((b, t, rw), F32)] * 7,
        grid=(b, t // tt),
        in_specs=[
            pl.BlockSpec((1, tt, cols), lambda i, j: (i, j, 0)),
            pl.BlockSpec((1, SUBLANES, cols), lambda i, j: (i, jnp.maximum(j * (tt // SUBLANES) - 1, 0), 0)),
            pl.BlockSpec((1, 1, cols), lambda i, j: (i, 0, 0)),
            full((1, cols)), full((1, rw)), full((LANES, rw)), full((1, rw)), full((LANES, rw)),
            full((LANES, rw)), full((1, rw)), full((1, rw)),
        ],
        out_specs=[pl.BlockSpec((1, tt, rw), lambda i, j: (i, j, 0))] * 7,
        compiler_params=_cparams(("parallel", "parallel")),
        name="rwkv_prep",
    )(ur, ur, shift0.reshape(b, 1, cols), row(prm['rw_mu']), row(prm['rw_w0']), w2p, row(prm['rw_a0']), a2p,
      prm['rw_g2'], row(prm['rw_kk']), row(prm['rw_ka']))
    return outs


def _pair_masks():
    i = lax.broadcasted_iota(jnp.int32, (LANES, LANES), 0)
    j = lax.broadcasted_iota(jnp.int32, (LANES, LANES), 1)
    same = lax.shift_right_logical(i, 6) == lax.shift_right_logical(j, 6)
    return i, j, same


def _split(x):
    hi = x.astype(BF16)
    return hi, (x - hi.astype(F32)).astype(BF16)


def _mm3(a, b):
    n = b[0].shape[1]
    lhs = jnp.concatenate(a, axis=1)
    rhs = jnp.concatenate([jnp.concatenate(b, axis=1),
                           jnp.concatenate([b[0], jnp.zeros_like(b[0])], axis=1)], axis=0)
    out = jnp.dot(lhs, rhs, preferred_element_type=F32)
    return out[:, :n] + out[:, n:]


def _mm3_nt(a, b):
    n = b[0].shape[0]
    lhs = jnp.concatenate(a, axis=1)
    rhs = jnp.concatenate([jnp.concatenate([b[0], b[0]], axis=1),
                           jnp.concatenate([b[1], jnp.zeros_like(b[1])], axis=1)], axis=0)
    out = lax.dot_general(lhs, rhs, NT_DIMS, preferred_element_type=F32)
    return out[:, :n] + out[:, n:]


def _each(f, *lists):
    return [f(*xs) for xs in zip(*lists)]


def _rwkv_chunk_factors(r, lw, k, v, kk, a):
    c = R_CHUNK
    row = lax.broadcasted_iota(jnp.int32, (c, LANES), 0)
    cum = lw
    s = 1
    while s < c:
        cum = _each(lambda z: z + jnp.where(row >= s, pltpu.roll(z, s, axis=0), 0.0), cum)
        s *= 2
    cl = _each(lambda z: z[c - 1:c, :], cum)
    beta = _each(lambda x, y: x * y, kk, a)
    kap_t = _each(lambda x, cu, l: x * jnp.exp(cu - l), kk, cum, lw)
    r_t = _each(lambda x, cu: x * jnp.exp(cu), r, cum)
    e_inv = _each(lambda cu: jnp.exp(-cu), cum)
    b_t = _each(lambda x, e: x * e, beta, e_inv)
    k_t = _each(lambda x, e: x * e, k, e_inv)
    e_end = _each(lambda l, cu: jnp.exp(l - cu), cl, cum)
    b_h = _each(lambda x, e: x * e, beta, e_end)
    k_h = _each(lambda x, e: x * e, k, e_end)

    lo = lax.broadcasted_iota(jnp.int32, (c, LANES), 1) < R_HEAD_DIM
    stack = lambda x: jnp.concatenate([jnp.where(lo, x, 0.0), jnp.where(lo, 0.0, x)], axis=0)
    dup = lambda x: jnp.concatenate([x, x], axis=0)
    split_of = lambda f: (lambda x: _split(f(x)))
    ident = lambda x: x

    i, j, same = _pair_masks()
    strict = same & (j < i)
    incl = same & (j <= i)
    kap_s, r_s, v_s = _each(stack, kap_t), _each(stack, r_t), _each(stack, v)
    kap_p, r_p, v_p = _each(_split, kap_s), _each(_split, r_s), _each(_split, v_s)
    b_d, k_d = _each(split_of(dup), b_t), _each(split_of(dup), k_t)
    a_ab = _each(lambda x, y: jnp.where(strict, _mm3_nt(x, y), 0.0), kap_p, b_d)
    a_ak = _each(lambda x, y: jnp.where(strict, _mm3_nt(x, y), 0.0), kap_p, k_d)
    a_rb = _each(lambda x, y: jnp.where(incl, _mm3_nt(x, y), 0.0), r_p, b_d)
    a_rk = _each(lambda x, y: jnp.where(incl, _mm3_nt(x, y), 0.0), r_p, k_d)

    x = [jnp.where(i == j, 1.0, 0.0).astype(F32)] * len(r)
    s = 1
    while s < c:
        low = (lax.shift_right_logical(i, s.bit_length()) == lax.shift_right_logical(j, s.bit_length())) \
            & ((i & (2 * s - 1)) >= s) & ((j & (2 * s - 1)) < s)
        xp = _each(_split, x)
        ms = _each(lambda z: _split(jnp.where(low, z, 0.0)), a_ab)
        t1 = _each(split_of(ident), _each(_mm3, xp, ms))
        x = _each(lambda z, u, w_: z - _mm3(u, w_), x, t1, xp)
        s *= 2

    aakv = _each(_mm3, _each(_split, a_ak), v_p)
    arkv = _each(_mm3, _each(_split, a_rk), v_p)
    bh_p = _each(split_of(stack), b_h)
    vtk = _each(_mm3, _each(lambda z: _split(z.T), v_s), _each(split_of(stack), k_h))
    rhs = _each(lambda z, u: _split(jnp.concatenate([z, u], axis=1)), kap_s, aakv)
    wu = _each(lambda z, u: -_mm3(z, u), _each(_split, x), rhs)
    w = _each(lambda z: z[:, :LANES], wu)
    upre = _each(lambda z: z[:, LANES:], wu)
    arb_p = _each(_split, a_rb)
    r2 = _each(lambda z, u, w_: z + _mm3(u, _split(w_)), r_s, arb_p, w)
    ypre = _each(lambda z, u, w_: z + _mm3(u, _split(w_)), arkv, arb_p, upre)
    g = _each(lambda l, w_, u: jnp.where(i == j, jnp.exp(l), 0.0) + _mm3(_split(w_.T), u), cl, w, bh_p)
    spre = _each(lambda z, w_, u: z + _mm3(_split(w_.T), u), vtk, upre, bh_p)
    return list(zip(r2, ypre, g, spre))


def _rwkv_core_kernel(r_ref, lw_ref, k_ref, v_ref, kk_ref, a_ref, s0_ref, y_o, st_o, s_scr, *, nch, npair):
    cidx = pl.program_id(1)

    @pl.when(cidx == 0)
    def _():
        s_scr[...] = s0_ref[0]

    c = R_CHUNK
    parts = lambda ref: [ref[0, ch * c:(ch + 1) * c, p * LANES:(p + 1) * LANES]
                         for ch in range(nch) for p in range(npair)]
    factors = _rwkv_chunk_factors(parts(r_ref), parts(lw_ref), parts(k_ref), parts(v_ref), parts(kk_ref),
                                  parts(a_ref))
    s = [s_scr[p] for p in range(npair)]
    for ch in range(nch):
        for p in range(npair):
            r2, ypre, g, spre = factors[ch * npair + p]
            sp = _split(s[p])
            ys = _mm3_nt(_split(r2), sp) + ypre
            y_o[0, ch * c:(ch + 1) * c, p * LANES:(p + 1) * LANES] = ys[:c] + ys[c:]
            s[p] = _mm3(sp, _split(g)) + spre
    for p in range(npair):
        s_scr[p] = s[p]
        st_o[0, p] = s[p]


def rwkv_scan(r, lw, k, v, kk, a, wkv0):
    b, t, rw = r.shape
    npair = rw // LANES
    assert t % R_CHUNK == 0 and 2 * R_HEAD_DIM == LANES
    nch = 2 if t % (2 * R_CHUNK) == 0 else 1
    tt = nch * R_CHUNK
    w4 = wkv0.reshape(b, npair, 2, R_HEAD_DIM, R_HEAD_DIM)
    zero = jnp.zeros_like(w4[:, :, 0])
    s0 = jnp.concatenate([jnp.concatenate([w4[:, :, 0], zero], axis=-1),
                          jnp.concatenate([zero, w4[:, :, 1]], axis=-1)], axis=-2)
    act = pl.BlockSpec((1, tt, rw), lambda i, c: (i, c, 0))
    st_spec = pl.BlockSpec((1, npair, LANES, LANES), lambda i, c: (i, 0, 0, 0))
    y, st = pl.pallas_call(
        functools.partial(_rwkv_core_kernel, nch=nch, npair=npair),
        out_shape=[jax.ShapeDtypeStruct((b, t, rw), F32), jax.ShapeDtypeStruct((b, npair, LANES, LANES), F32)],
        grid=(b, t // tt),
        in_specs=[act] * 6 + [st_spec],
        out_specs=[act, st_spec],
        scratch_shapes=[pltpu.VMEM((npair, LANES, LANES), F32)],
        compiler_params=_cparams(("parallel", "arbitrary")),
        name="rwkv_core",
    )(r, lw, k, v, kk, a, s0)
    wkv = jnp.stack([st[:, :, :R_HEAD_DIM, :R_HEAD_DIM], st[:, :, R_HEAD_DIM:, R_HEAD_DIM:]], axis=2)
    return y, wkv.reshape(wkv0.shape)


def _rwkv_post_kernel(y_ref, r_ref, k_ref, v_ref, g_ref, rk_ref, lw_ref, lb_ref, o_ref):
    seg = _head_sum_matrix()
    inv_n = 1.0 / R_HEAD_DIM
    for p in range(y_ref.shape[2] // LANES):
        ps = slice(p * LANES, (p + 1) * LANES)
        y = y_ref[0, :, ps]
        hsum = lambda z: jnp.dot(z, seg, precision=HI, preferred_element_type=F32)
        mean = hsum(y) * inv_n
        d = y - mean
        var = hsum(d * d) * inv_n
        yn = d * lax.rsqrt(var + R_GN_EPS) * lw_ref[:, ps] + lb_ref[:, ps]
        bonus = hsum(r_ref[0, :, ps] * k_ref[0, :, ps] * rk_ref[:, ps]) * v_ref[0, :, ps]
        o_ref[0, :, ps] = (yn + bonus) * g_ref[0, :, ps]


def rwkv_post(y, r, k, v, g, prm):
    b, t, rw = y.shape
    tt = _row_tile(t, 256)
    act = pl.BlockSpec((1, tt, rw), lambda i, j: (i, j, 0))
    par = pl.BlockSpec((1, rw), lambda i, j: (0, 0))
    return pl.pallas_call(
        _rwkv_post_kernel,
        out_shape=jax.ShapeDtypeStruct((b, t, rw), F32),
        grid=(b, t // tt),
        in_specs=[act] * 5 + [par] * 3,
        out_specs=act,
        compiler_params=_cparams(("parallel", "parallel")),
        name="rwkv_post",
    )(y, r, k, v, g, prm['rw_rk'].reshape(1, rw), prm['rw_lnx_w'].reshape(1, rw), prm['rw_lnx_b'].reshape(1, rw))


def rwkv_mix(ur, shift0, wkv0, prm):
    b, t, _ = ur.shape
    tp = -(-t // SUBLANES) * SUBLANES
    urp = jnp.pad(ur, ((0, 0), (0, tp - t), (0, 0)))
    r, lw, k, v, kk, a, g = rwkv_prep(urp, shift0, prm)
    tc = -(-t // R_CHUNK) * R_CHUNK
    fit = lambda z: jnp.pad(z[:, :t], ((0, 0), (0, tc - t), (0, 0)))
    r, lw, k, v, kk, a, g = (fit(z) for z in (r, lw, k, v, kk, a, g))
    y, wkv = rwkv_scan(r, lw, k, v, kk, a, wkv0)
    out = rwkv_post(y, r, k, v, g, prm)
    return out[:, :t], wkv, ur[:, -1, :prm['rw_mu'].shape[-1]]


def _pool_kernel(u_ref, pv_ref, p0_ref, w_ref, sc_ref, o_ref, *, pos0):
    t = pl.program_id(1)
    cur = u_ref[0]
    tt = cur.shape[0]
    ext = jnp.concatenate([jnp.where(t == 0, p0_ref[0], pv_ref[0]), cur], axis=0)
    sums = []
    s = ext
    w = 1
    while w < POOL_MAX:
        s = s + pltpu.roll(s, w, axis=0)
        w *= 2
        sums.append(s)
    pos = pos0 + t * tt + lax.broadcasted_iota(jnp.int32, (tt, 1), 0)
    for gi, w in enumerate(POOL_WINDOWS):
        gs = slice(gi * LANES, (gi + 1) * LANES)
        win = sums[w.bit_length() - 2][POOL_MAX:, gs]
        cnt = jnp.minimum(pos + 1, w).astype(F32)
        m = win / cnt - cur[:, gs]
        z = jnp.dot(m.astype(BF16), w_ref[gi], preferred_element_type=F32)
        o_ref[0, :, gs] = z * sc_ref[:, gs]


def pool_mix(uc, pool0, pos0, pool_w, pool_scale):
    b, t, width = uc.shape
    assert width == len(POOL_WINDOWS) * LANES and pool_w.shape[1] == LANES
    tp = -(-t // SUBLANES) * SUBLANES
    ucp = jnp.pad(uc, ((0, 0), (0, tp - t), (0, 0)))
    tt = _row_tile(tp, 256)
    assert tt % POOL_MAX == 0 or tp == tt
    p0 = jnp.pad(pool0, ((0, 0), (1, 0), (0, 0)))
    nprev = tt // POOL_MAX if tt % POOL_MAX == 0 else 0
    ext = jnp.concatenate([pool0, uc], axis=1)
    if nprev:
        pv, pv_spec = ucp, pl.BlockSpec((1, POOL_MAX, width), lambda i, j: (i, jnp.maximum(j * nprev - 1, 0), 0))
    else:
        pv, pv_spec = p0, pl.BlockSpec((1, POOL_MAX, width), lambda i, j: (i, 0, 0))
    z = pl.pallas_call(
        functools.partial(_pool_kernel, pos0=pos0),
        out_shape=jax.ShapeDtypeStruct((b, tp, width), F32),
        grid=(b, tp // tt),
        in_specs=[
            pl.BlockSpec((1, tt, width), lambda i, j: (i, j, 0)),
            pv_spec,
            pl.BlockSpec((1, POOL_MAX, width), lambda i, j: (i, 0, 0)),
            pl.BlockSpec(pool_w.shape, lambda i, j: (0, 0, 0)),
            pl.BlockSpec((1, width), lambda i, j: (0, 0)),
        ],
        out_specs=pl.BlockSpec((1, tt, width), lambda i, j: (i, j, 0)),
        compiler_params=_cparams(("parallel", "parallel")),
        name="pool_mix",
    )(ucp, pv, p0, pool_w.astype(BF16), pool_scale.reshape(1, width))
    return z[:, :t], ext[:, -(POOL_MAX - 1):]


def _mix_out_kernel(x_ref, ya_ref, yr_ref, yc_ref, wa_ref, wr_ref, wc_ref, o_ref):
    acc = jnp.dot(ya_ref[...].astype(BF16), wa_ref[...], preferred_element_type=F32)
    acc += jnp.dot(yr_ref[...].astype(BF16), wr_ref[...], preferred_element_type=F32)
    acc += jnp.dot(yc_ref[...].astype(BF16), wc_ref[...], preferred_element_type=F32)
    o_ref[...] = x_ref[...] + acc


def mix_out(x, ya, yr, yc, w_out, layer):
    n, d = x.shape
    wa_, wr_, wc_ = ya.shape[1], yr.shape[1], yc.shape[1]
    assert wa_ % wr_ == 0 and wr_ == wc_
    tm = _row_tile(n, 512)
    tn = d
    act = lambda wd: pl.BlockSpec((tm, wd), lambda i, j: (i, 0))
    wsp = lambda wd, blk: pl.BlockSpec((None, wd, tn), lambda i, j: (layer, blk, j))
    return pl.pallas_call(
        _mix_out_kernel,
        out_shape=jax.ShapeDtypeStruct((n, d), F32),
        grid=(n // tm, d // tn),
        in_specs=[pl.BlockSpec((tm, tn), lambda i, j: (i, j)), act(wa_), act(wr_), act(wc_),
                  wsp(wa_, 0), wsp(wr_, wa_ // wr_), wsp(wc_, (wa_ + wr_) // wc_)],
        out_specs=pl.BlockSpec((tm, tn), lambda i, j: (i, j)),
        compiler_params=_cparams(("parallel", "arbitrary")),
        name="mix_out",
    )(x, ya, yr, yc, w_out, w_out, w_out)


def _cross_kernel(x_ref, g_ref, wq_ref, mk_ref, mv_ref, wo_ref, o_ref, *, nh):
    x = x_ref[0]
    h = _rms(x, g_ref[...]).astype(BF16)
    q = jnp.dot(h, wq_ref[...], preferred_element_type=F32)
    scale = M_HEAD_DIM ** -0.5
    outs = []
    for hh in range(nh):
        hs = slice(hh * M_HEAD_DIM, (hh + 1) * M_HEAD_DIM)
        s = lax.dot_general(q[:, hs].astype(BF16), mk_ref[0, :, hs].astype(BF16), NT_DIMS,
                            preferred_element_type=F32) * scale
        p = jnp.exp(s - s.max(axis=-1, keepdims=True))
        l = jnp.sum(p, axis=-1, keepdims=True)
        outs.append(jnp.dot(p.astype(BF16), mv_ref[0, :, hs].astype(BF16), preferred_element_type=F32) / l)
    o = jnp.concatenate(outs, axis=-1).astype(BF16)
    o_ref[0] = x + jnp.dot(o, wo_ref[...], preferred_element_type=F32)


def cross_attend(x, g, wq, mk, mv, wo):
    b, t0, d = x.shape
    t = -(-t0 // SUBLANES) * SUBLANES
    x = jnp.pad(x, ((0, 0), (0, t - t0), (0, 0)))
    mw = wq.shape[1]
    nm = mk.shape[1]
    tm = _row_tile(t, 512)
    out = pl.pallas_call(
        functools.partial(_cross_kernel, nh=mw // M_HEAD_DIM),
        out_shape=jax.ShapeDtypeStruct((b, t, d), F32),
        grid=(b, t // tm),
        in_specs=[
            pl.BlockSpec((1, tm, d), lambda i, j: (i, j, 0)),
            pl.BlockSpec((1, d), lambda i, j: (0, 0)),
            pl.BlockSpec((d, mw), lambda i, j: (0, 0)),
            pl.BlockSpec((1, nm, mw), lambda i, j: (i, 0, 0)),
            pl.BlockSpec((1, nm, mw), lambda i, j: (i, 0, 0)),
            pl.BlockSpec((mw, d), lambda i, j: (0, 0)),
        ],
        out_specs=pl.BlockSpec((1, tm, d), lambda i, j: (i, j, 0)),
        compiler_params=_cparams(("parallel", "parallel")),
        name="cross_attend",
    )(x, g.reshape(1, d), wq, mk, mv, wo)
    return out[:, :t0]


def _layer(x, prm, moba_fn, pos0, shift0, wkv0, pool0, mem_k, mem_v):
    b, t, d = x.shape
    n = b * t
    lyr, stk = prm['layer'], prm['stacked']
    x2 = ffn_half_step(x.reshape(n, d), prm['norm_ffn1'], stk['ffn1_gate'], stk['ffn1_up'], stk['ffn1_down'], lyr)
    aw, rc, cw = prm['a_width'], prm['r_cols'], prm['c_width']
    qa, ka, va, ur, uc = (z.reshape(b, t, -1) for z in
                          in_proj(x2, prm['norm_mix'], stk['w_in'], lyr, (aw, aw, aw, rc, cw)))
    ya = moba_fn(qa, ka, va)
    yr, wkv, shift = rwkv_mix(ur, shift0, wkv0, prm)
    yc, pool_buf = pool_mix(uc, pool0, pos0, prm['pool_w'], prm['pool_scale'])
    x3 = mix_out(x2, ya.reshape(n, aw), yr.reshape(n, -1), yc.reshape(n, cw), stk['w_out'], lyr)
    x4 = cross_attend(x3.reshape(b, t, d), prm['norm_cross'], prm['mem_wq'], mem_k, mem_v, prm['mem_wo'])
    x5 = ffn_half_step(x4.reshape(n, d), prm['norm_ffn2'], stk['ffn2_gate'], stk['ffn2_up'], stk['ffn2_down'], lyr)
    return x5.reshape(b, t, d), ka, va, wkv, shift, pool_buf


def kernel(x_prompt, x_sample, cache_k, cache_v, cache_mem_k, cache_mem_v, state_wkv, state_shift, state_pool, page_table, mem_prompt, norm_ffn1, ffn1_gate, ffn1_up, ffn1_down, norm_mix, w_in, w_out, rw_mu, rw_w0, rw_w2, rw_a0, rw_a2, rw_g2, rw_kk, rw_ka, rw_rk, rw_lnx_w, rw_lnx_b, pool_w, pool_scale, norm_cross, norm_mem, mem_wq, mem_wk, mem_wv, mem_wo, norm_ffn2, ffn2_gate, ffn2_up, ffn2_down, norm_final):
    depth = w_in.shape[0]
    bp, tp, d = x_prompt.shape
    db, ts, _ = x_sample.shape
    n_heads, hd = cache_k.shape[3], cache_k.shape[4]
    aw = n_heads * hd
    r_heads, rn = rw_rk.shape[1], rw_rk.shape[2]
    r_cols = rw_mu.shape[1]
    c_width = pool_scale.shape[1]
    nm, m_heads, mhd = cache_mem_k.shape[2:]
    mw = m_heads * mhd
    assert hd == A_HEAD_DIM and rn == R_HEAD_DIM and mhd == M_HEAD_DIM
    past = page_table.shape[1] * PAGE_SIZE
    slopes = jnp.exp2(-8.0 * jnp.arange(1, n_heads + 1, dtype=F32) / n_heads)
    ck = cache_k.reshape(depth, cache_k.shape[1], PAGE_SIZE * n_heads, hd)
    cv = cache_v.reshape(depth, cache_v.shape[1], PAGE_SIZE * n_heads, hd)
    bf = lambda z: z.astype(BF16)
    bounds = (0, aw, 2 * aw, 3 * aw, 3 * aw + r_cols, w_in.shape[2])
    groups = [bf(w_in[:, :, lo:hi]) for lo, hi in zip(bounds[:-1], bounds[1:])]
    w_in_p = jnp.concatenate([jnp.pad(gw, ((0, 0), (0, 0), (0, -gw.shape[2] % IN_TILE))) for gw in groups], axis=2)
    stacked = dict(ffn1_gate=bf(ffn1_gate), ffn1_up=bf(ffn1_up), ffn1_down=bf(ffn1_down), w_in=w_in_p,
                   w_out=bf(w_out), ffn2_gate=bf(ffn2_gate), ffn2_up=bf(ffn2_up), ffn2_down=bf(ffn2_down))
    big = dict(mem_wq=bf(mem_wq), mem_wk=bf(mem_wk), mem_wv=bf(mem_wv), mem_wo=bf(mem_wo))
    small = dict(norm_ffn1=norm_ffn1, norm_mix=norm_mix, rw_mu=rw_mu, rw_w0=rw_w0, rw_w2=rw_w2, rw_a0=rw_a0,
                 rw_a2=rw_a2, rw_g2=rw_g2, rw_kk=rw_kk, rw_ka=rw_ka, rw_rk=rw_rk.reshape(depth, -1),
                 rw_lnx_w=rw_lnx_w, rw_lnx_b=rw_lnx_b, pool_w=pool_w, pool_scale=pool_scale,
                 norm_cross=norm_cross, norm_ffn2=norm_ffn2)

    shift0 = jnp.zeros((bp, r_cols), F32)
    wkv0 = jnp.zeros((bp, r_heads, rn, rn), F32)
    pool0 = jnp.zeros((bp, POOL_MAX - 1, c_width), F32)
    xp, xs = x_prompt, x_sample
    outs = [[] for _ in range(12)]
    for l in range(depth):
        prm = {k: v[l] for k, v in big.items()}
        prm.update({k: v[l] for k, v in small.items()})
        prm.update(a_width=aw, r_cols=r_cols, c_width=c_width, layer=l, stacked=stacked)
        hm = rmsnorm(mem_prompt.reshape(bp * nm, d), norm_mem[l], BF16)
        mk = matmul(hm, prm['mem_wk'], mw).reshape(bp, nm, mw)
        mv = matmul(hm, prm['mem_wv'], mw).reshape(bp, nm, mw)
        moba_p = functools.partial(moba_prompt, slopes=slopes)
        xp, k_, v_, w_, sh_, pl_ = _layer(xp, prm, moba_p, 0, shift0, wkv0, pool0, mk, mv)
        res_p = (k_, v_, w_, sh_, pl_, mk, mv)
        moba_s = functools.partial(moba_sample, cache_k=ck, cache_v=cv, layer=l, page_table=page_table,
                                   slopes=slopes)
        xs, k_, v_, w_, sh_, pl_ = _layer(xs, prm, moba_s, past, state_shift[l], state_wkv[l], state_pool[l],
                                           cache_mem_k[l].reshape(db, nm, mw), cache_mem_v[l].reshape(db, nm, mw))
        res_s = (k_, v_, w_, sh_, pl_)
        for i, z in enumerate(res_p + res_s):
            outs[i].append(z)
    y_prompt = rmsnorm(xp.reshape(bp * tp, d), norm_final, F32).reshape(bp, tp, d)
    y_sample = rmsnorm(xs.reshape(db * ts, d), norm_final, F32).reshape(db, ts, d)
    stk = [jnp.stack(o) for o in outs]
    heads = lambda z, nh_, hd_: z.reshape(z.shape[:-1] + (nh_, hd_))
    for i in (0, 1, 7, 8):
        stk[i] = heads(stk[i], n_heads, hd)
    for i in (5, 6):
        stk[i] = heads(stk[i], m_heads, mhd)
    return (y_prompt, y_sample) + tuple(stk)
```

```python
import functools

import jax
import jax.numpy as jnp
from jax import lax
from jax.experimental import pallas as pl
from jax.experimental.pallas import tpu as pltpu

F32 = jnp.float32
BF16 = jnp.bfloat16
HI = lax.Precision.HIGHEST

RMS_EPS = 1e-6
LANES = 128
SUBLANES = 8
VMEM_LIMIT = 48 * 1024 * 1024

PAGE_SIZE = 128
A_HEAD_DIM = 128
A_BLOCK = 256
A_TOPK = 3
A_QCHUNK = 128
R_HEAD_DIM = 64
R_CHUNK = 64
R_GN_EPS = 64e-5
POOL_WINDOWS = (2, 4, 8, 16)
POOL_MAX = 16
M_HEAD_DIM = 128
NEG = -1e30

NT_DIMS = (((1,), (1,)), ((), ()))
TN_DIMS = (((0,), (0,)), ((), ()))


VMEM_LIMIT_BIG = 58 * 1024 * 1024


def _cparams(sem, vmem_limit=VMEM_LIMIT):
    return pltpu.CompilerParams(dimension_semantics=sem, vmem_limit_bytes=vmem_limit)


def _row_tile(n, pref):
    return pref if n % pref == 0 else n


def _rms(x, g):
    ms = jnp.mean(x * x, axis=-1, keepdims=True)
    return x * lax.rsqrt(ms + RMS_EPS) * g


def _rms_kernel(x_ref, g_ref, o_ref):
    o_ref[...] = _rms(x_ref[...], g_ref[...]).astype(o_ref.dtype)


def rmsnorm(x, g, out_dtype):
    n, d = x.shape
    tm = _row_tile(n, 512)
    return pl.pallas_call(
        _rms_kernel,
        out_shape=jax.ShapeDtypeStruct((n, d), out_dtype),
        grid=(n // tm,),
        in_specs=[pl.BlockSpec((tm, d), lambda i: (i, 0)), pl.BlockSpec((1, d), lambda i: (0, 0))],
        out_specs=pl.BlockSpec((tm, d), lambda i: (i, 0)),
        compiler_params=_cparams(("parallel",)),
        name="rmsnorm",
    )(x, g.reshape(1, d))


def _matmul_kernel(a_ref, w_ref, o_ref):
    o_ref[...] = jnp.dot(a_ref[...], w_ref[...], preferred_element_type=F32)


def matmul(a, w, tn):
    n, k = a.shape
    m = w.shape[1]
    tm = _row_tile(n, 512)
    return pl.pallas_call(
        _matmul_kernel,
        out_shape=jax.ShapeDtypeStruct((n, m), F32),
        grid=(n // tm, m // tn),
        in_specs=[pl.BlockSpec((tm, k), lambda i, j: (i, 0)), pl.BlockSpec((k, tn), lambda i, j: (0, j))],
        out_specs=pl.BlockSpec((tm, tn), lambda i, j: (i, j)),
        compiler_params=_cparams(("parallel", "arbitrary")),
        name="matmul",
    )(a, w)


IN_TILE = 512


def _in_proj_kernel(x_ref, g_ref, w_ref, *rest, first_tile):
    out_refs, h_scr = rest[:-1], rest[-1]
    j = pl.program_id(1)

    @pl.when(j == 0)
    def _():
        h_scr[...] = _rms(x_ref[...], g_ref[...]).astype(BF16)

    for o_ref, lo, hi in zip(out_refs, first_tile[:-1], first_tile[1:]):
        @pl.when((j >= lo) & (j < hi))
        def _(o_ref=o_ref):
            o_ref[...] = jnp.dot(h_scr[...], w_ref[...], preferred_element_type=F32)


def in_proj(x, g, w, layer, widths):
    n, d = x.shape
    tm = _row_tile(n, 1024)
    tiles = [-(-wd // IN_TILE) for wd in widths]
    first_tile = [0]
    for nt in tiles:
        first_tile.append(first_tile[-1] + nt)
    assert w.shape[2] == first_tile[-1] * IN_TILE

    def out_spec(lo, nt):
        return pl.BlockSpec((tm, IN_TILE), lambda i, j: (i, jnp.clip(j - lo, 0, nt - 1)))

    return pl.pallas_call(
        functools.partial(_in_proj_kernel, first_tile=tuple(first_tile)),
        out_shape=[jax.ShapeDtypeStruct((n, nt * IN_TILE), F32) for nt in tiles],
        grid=(n // tm, first_tile[-1]),
        in_specs=[pl.BlockSpec((tm, d), lambda i, j: (i, 0)), pl.BlockSpec((1, d), lambda i, j: (0, 0)),
                  pl.BlockSpec((None, d, IN_TILE), lambda i, j: (layer, 0, j))],
        out_specs=[out_spec(lo, nt) for lo, nt in zip(first_tile[:-1], tiles)],
        scratch_shapes=[pltpu.VMEM((tm, d), BF16)],
        compiler_params=_cparams(("parallel", "arbitrary"), VMEM_LIMIT_BIG),
        name="in_proj",
    )(x, g.reshape(1, d), w)


FFN_OUT_CHUNK = 512


def _ffn_kernel(x_ref, g_ref, wg_ref, wu_ref, wd_ref, o_ref, h_scr):
    j = pl.program_id(1)

    @pl.when(j == 0)
    def _():
        x = x_ref[...]
        h_scr[...] = _rms(x, g_ref[...]).astype(BF16)
        o_ref[...] = 2.0 * x

    h = h_scr[...]
    gate = jnp.dot(h, wg_ref[...], preferred_element_type=F32)
    up = jnp.dot(h, wu_ref[...], preferred_element_type=F32)
    act = (gate * jax.nn.sigmoid(gate) * up).astype(BF16)
    for c0 in range(0, o_ref.shape[1], FFN_OUT_CHUNK):
        cols = slice(c0, c0 + FFN_OUT_CHUNK)
        o_ref[:, cols] += jnp.dot(act, wd_ref[:, cols], preferred_element_type=F32)

    @pl.when(j == pl.num_programs(1) - 1)
    def _():
        o_ref[...] = 0.5 * o_ref[...]


def ffn_half_step(x, g, wg, wu, wd, layer, tf=256):
    n, d = x.shape
    f = wg.shape[2]
    tm = _row_tile(n, 1024)
    assert d % FFN_OUT_CHUNK == 0
    return pl.pallas_call(
        _ffn_kernel,
        out_shape=jax.ShapeDtypeStruct((n, d), F32),
        grid=(n // tm, f // tf),
        in_specs=[
            pl.BlockSpec((tm, d), lambda i, j: (i, 0)),
            pl.BlockSpec((1, d), lambda i, j: (0, 0)),
            pl.BlockSpec((None, d, tf), lambda i, j: (layer, 0, j)),
            pl.BlockSpec((None, d, tf), lambda i, j: (layer, 0, j)),
            pl.BlockSpec((None, tf, d), lambda i, j: (layer, j, 0)),
        ],
        out_specs=pl.BlockSpec((tm, d), lambda i, j: (i, 0)),
        scratch_shapes=[pltpu.VMEM((tm, d), BF16)],
        compiler_params=_cparams(("parallel", "arbitrary"), VMEM_LIMIT_BIG),
        name="ffn_half_step",
    )(x, g.reshape(1, d), wg, wu, wd)


def _moba_prompt_kernel(slope_ref, q_ref, k_ref, v_ref, o_ref, kb_scr, vb_scr, km_scr, *, nb):
    h = pl.program_id(1)
    own = pl.program_id(2)
    qc = A_BLOCK
    hps, nbp = km_scr.shape[0], km_scr.shape[1]
    heads = range(hps)
    hsl = lambda i: slice(i * A_HEAD_DIM, (i + 1) * A_HEAD_DIM)

    @pl.when(own == 0)
    def _():
        k = k_ref[0]
        kb_scr[...] = k.astype(BF16)
        vb_scr[...] = v_ref[0].astype(BF16)
        km_scr[...] = jnp.zeros_like(km_scr)
        for i in heads:
            for n in range(nb):
                km_scr[i, n:n + 1, :] = jnp.sum(k[n * A_BLOCK:(n + 1) * A_BLOCK, hsl(i)], axis=0,
                                                keepdims=True) * (1.0 / A_BLOCK)

    q = [q_ref[0, :, hsl(i)] for i in heads]
    slope = [slope_ref[h * hps + i] for i in heads]
    scale = A_HEAD_DIM ** -0.5

    g = [_mm3_nt(_split(km_scr[i]), _split(q[i])) for i in heads]
    blk = lax.broadcasted_iota(jnp.int32, (nbp, qc), 0)
    gm = [jnp.where(blk < own, g_, -jnp.inf) for g_ in g]
    sel_t = [jnp.zeros((nbp, qc), F32) for _ in heads]
    for n in range(nb):
        for i in heads:
            gn = gm[i][n:n + 1, :]
            beats = (gm[i] > gn) | ((gm[i] == gn) & (blk < n))
            cnt = jnp.sum(jnp.where(beats, 1.0, 0.0), axis=0, keepdims=True)
            sel_n = jnp.where(cnt < A_TOPK, 1.0, 0.0) * jnp.where(n < own, 1.0, 0.0)
            sel_t[i] = jnp.where(blk == n, sel_n, sel_t[i])
    sel_b = [st.astype(BF16) for st in sel_t]

    log2e = 1.4426950408889634
    qb = [(q_ * (scale * log2e)).astype(BF16) for q_ in q]
    slope = [sl * log2e for sl in slope]
    row = lax.broadcasted_iota(jnp.int32, (qc, A_BLOCK), 0)
    off_k = lax.broadcasted_iota(jnp.int32, (qc, A_BLOCK), 1)

    def attend(own_k):
        nk = own_k + 1
        s = [lax.dot_general(qb[i], kb_scr[0:nk * A_BLOCK, hsl(i)], NT_DIMS, preferred_element_type=F32)
             for i in heads]
        if own_k:
            blk_of_key = lax.broadcasted_iota(jnp.int32, (nbp, own_k * A_BLOCK), 1) // A_BLOCK
            spread = jnp.where(blk_of_key == lax.broadcasted_iota(jnp.int32, (nbp, own_k * A_BLOCK), 0),
                               1.0, 0.0).astype(BF16)
            picked = [lax.dot_general(sel_b[i], spread, TN_DIMS, preferred_element_type=F32) for i in heads]
        blocks = [[] for _ in heads]
        for n in range(nk):
            for i in heads:
                sn = s[i][:, n * A_BLOCK:(n + 1) * A_BLOCK] + slope[i] * (n * A_BLOCK + off_k[0:1, :]).astype(F32)
                if n < own_k:
                    allow = picked[i][:, n * A_BLOCK:(n + 1) * A_BLOCK] > 0.5
                else:
                    allow = off_k <= row
                blocks[i].append(jnp.where(allow, sn, NEG))
        m = [blocks[i][0].max(axis=-1, keepdims=True) for i in heads]
        for n in range(1, nk):
            m = [jnp.maximum(m[i], blocks[i][n].max(axis=-1, keepdims=True)) for i in heads]
        l = [jnp.zeros((qc, 1), F32) for _ in heads]
        o = [jnp.zeros((qc, A_HEAD_DIM), F32) for _ in heads]
        for n in range(nk):
            for i in heads:
                p = jnp.exp2(blocks[i][n] - m[i])
                l[i] = l[i] + jnp.sum(p, axis=-1, keepdims=True)
                o[i] = o[i] + jnp.dot(p.astype(BF16), vb_scr[n * A_BLOCK:(n + 1) * A_BLOCK, hsl(i)],
                                      preferred_element_type=F32)
        for i in heads:
            o_ref[0, :, hsl(i)] = o[i] / l[i]

    for own_k in range(nb):
        pl.when(own == own_k)(functools.partial(attend, own_k))


def moba_prompt(q, k, v, slopes):
    b, t, w = q.shape
    nh = w // A_HEAD_DIM
    nb = t // A_BLOCK
    assert t % A_BLOCK == 0 and nb >= A_TOPK and nb <= LANES and A_BLOCK % A_QCHUNK == 0
    nbp = -(-nb // SUBLANES) * SUBLANES
    hps = 2 if nh % 2 == 0 else 1
    hw = hps * A_HEAD_DIM
    return pl.pallas_call(
        functools.partial(_moba_prompt_kernel, nb=nb),
        out_shape=jax.ShapeDtypeStruct((b, t, w), F32),
        grid=(b, nh // hps, nb),
        in_specs=[
            pl.BlockSpec(memory_space=pltpu.SMEM),
            pl.BlockSpec((1, A_BLOCK, hw), lambda i, h, c: (i, c, h)),
            pl.BlockSpec((1, t, hw), lambda i, h, c: (i, 0, h)),
            pl.BlockSpec((1, t, hw), lambda i, h, c: (i, 0, h)),
        ],
        out_specs=pl.BlockSpec((1, A_BLOCK, hw), lambda i, h, c: (i, c, h)),
        scratch_shapes=[
            pltpu.VMEM((t, hw), BF16),
            pltpu.VMEM((t, hw), BF16),
            pltpu.VMEM((hps, nbp, A_HEAD_DIM), F32),
        ],
        compiler_params=_cparams(("parallel", "parallel", "arbitrary")),
        name="moba_prompt",
    )(slopes, q, k, v)


def _moba_sample_kernel(pt_ref, q_ref, sl_ref, mb_ref, ob_ref, kn_ref, vn_ref, *rest, nh, nblk, past):
    page_refs, (o_ref, g_acc, m_acc, l_acc, o_scr) = rest[:-5], rest[-5:]
    n = pl.program_id(1)
    scale = A_HEAD_DIM ** -0.5
    rq = SUBLANES
    q = q_ref[0]
    lane = lax.broadcasted_iota(jnp.int32, (nh * rq, LANES), 1)

    @pl.when(n == 0)
    def _():
        g_acc[...] = jnp.zeros_like(g_acc)
        m_acc[...] = jnp.zeros_like(m_acc)
        l_acc[...] = jnp.zeros_like(l_acc)

    nbs = len(page_refs) // 4
    k_refs, v_refs = page_refs[:2 * nbs], page_refs[2 * nbs:]
    blk_ids = [n * nbs + jb for jb in range(nbs)]
    kpages = [(k_refs[2 * jb][...], k_refs[2 * jb + 1][...]) for jb in range(nbs)]
    head_sum = lambda pg: jnp.sum(pg.reshape(PAGE_SIZE, nh, A_HEAD_DIM), axis=0)
    kmean = [(head_sum(ka) + head_sum(kb)) * (1.0 / A_BLOCK) for ka, kb in kpages]
    kmean_rows = [jnp.concatenate([jnp.broadcast_to(km[h:h + 1], (rq, A_HEAD_DIM)) for h in range(nh)], axis=0)
                  for km in kmean]
    g = [jnp.sum(q * kr, axis=-1, keepdims=True) for kr in kmean_rows]
    kblk = [jnp.concatenate([ka, kb], axis=0).astype(BF16) for ka, kb in kpages]
    vblk = [jnp.concatenate([v_refs[2 * jb][...], v_refs[2 * jb + 1][...]], axis=0).astype(BF16)
            for jb in range(nbs)]
    qb = q.astype(BF16)
    s = [lax.dot_general(qb, kb_, NT_DIMS, preferred_element_type=F32) * scale + mb_ref[...] for kb_ in kblk]
    m = [s_.max(axis=-1, keepdims=True) for s_ in s]
    p = [jnp.exp(s_ - m_) for s_, m_ in zip(s, m)]
    l = [jnp.sum(p_, axis=-1, keepdims=True) for p_ in p]
    o = [jnp.dot(p_.astype(BF16), vb_, preferred_element_type=F32) for p_, vb_ in zip(p, vblk)]
    g_all, m_all, l_all = g_acc[...], m_acc[...], l_acc[...]
    for jb, blk in enumerate(blk_ids):
        here = lane == blk
        g_all = jnp.where(here, g[jb], g_all)
        m_all = jnp.where(here, m[jb] - sl_ref[:, 0:1] * (past - blk * A_BLOCK).astype(F32), m_all)
        l_all = jnp.where(here, l[jb], l_all)
        o_scr[blk] = o[jb]
    g_acc[...] = g_all
    m_acc[...] = m_all
    l_acc[...] = l_all

    @pl.when(n == nblk // nbs - 1)
    def _():
        gm = jnp.where(lane < nblk, g_acc[...], -jnp.inf)
        sel = jnp.zeros(gm.shape, F32)
        for i in range(nblk):
            gi = gm[:, i:i + 1]
            beats = (gm > gi) | ((gm == gi) & (lane < i))
            cnt = jnp.sum(jnp.where(beats, 1.0, 0.0), axis=-1, keepdims=True)
            sel = jnp.where(lane == i, jnp.where(cnt < A_TOPK, 1.0, 0.0), sel)
        selb = sel > 0.5
        mblk = m_acc[...]
        s_own = lax.dot_general(qb, kn_ref[0].astype(BF16), NT_DIMS, preferred_element_type=F32) * scale + ob_ref[...]
        mx = jnp.maximum(s_own.max(axis=-1, keepdims=True),
                         jnp.where(selb, mblk, NEG).max(axis=-1, keepdims=True))
        p_own = jnp.exp(s_own - mx)
        wgt = jnp.where(selb, jnp.exp(mblk - mx), 0.0)
        den = jnp.sum(p_own, axis=-1, keepdims=True) + jnp.sum(wgt * l_acc[...], axis=-1, keepdims=True)
        num = jnp.dot(p_own.astype(BF16), vn_ref[0].astype(BF16), preferred_element_type=F32)
        for i in range(nblk):
            num = num + wgt[:, i:i + 1] * o_scr[i]
        o_ref[0] = num / den


def moba_sample(q, k_new, v_new, cache_k, cache_v, layer, page_table, slopes):
    db, tn, w = q.shape
    nh = w // A_HEAD_DIM
    n_pages = page_table.shape[1]
    past = n_pages * PAGE_SIZE
    ppb = A_BLOCK // PAGE_SIZE
    rq = SUBLANES
    assert past % A_BLOCK == 0 and ppb == 2 and tn <= rq and tn * nh <= LANES and nh == SUBLANES
    nblk = past // A_BLOCK
    assert A_TOPK <= nblk <= LANES
    rows = nh * rq
    qr = jnp.pad(q.reshape(db, tn, nh, A_HEAD_DIM).transpose(0, 2, 1, 3), ((0, 0), (0, 0), (0, rq - tn), (0, 0)))
    qr = qr.reshape(db, rows, A_HEAD_DIM)
    new_rows = lambda z: jnp.pad(z.reshape(db, tn * nh, A_HEAD_DIM), ((0, 0), (0, LANES - tn * nh), (0, 0)))
    r_h = jnp.arange(rows, dtype=jnp.int32)[:, None] // rq
    r_t = jnp.arange(rows, dtype=jnp.int32)[:, None] % rq
    slope_r = slopes[r_h[:, 0]][:, None]
    col = jnp.arange(A_BLOCK * nh, dtype=jnp.int32)[None, :]
    mb = jnp.where(col % nh == r_h, -slope_r * (r_t - col // nh).astype(F32), NEG)
    colo = jnp.arange(LANES, dtype=jnp.int32)[None, :]
    jo = colo // nh
    ob = jnp.where((colo % nh == r_h) & (jo <= r_t) & (jo < tn), -slope_r * (r_t - jo).astype(F32), NEG)
    sl = jnp.broadcast_to(slope_r, (rows, LANES))

    nbs = max(n for n in (1, 2, 4, 8) if nblk % n == 0)
    npg = ppb * nbs

    def page_spec(j):
        return pl.BlockSpec((None, None, PAGE_SIZE * nh, A_HEAD_DIM),
                            lambda i, n, pt: (layer, pt[i, npg * n + j], 0, 0))

    const = lambda shape: pl.BlockSpec(shape, lambda i, n, pt: (0,) * len(shape))
    per_seq = lambda r: pl.BlockSpec((1, r, A_HEAD_DIM), lambda i, n, pt: (i, 0, 0))
    pages = [page_spec(j) for j in range(npg)]
    out = pl.pallas_call(
        functools.partial(_moba_sample_kernel, nh=nh, nblk=nblk, past=past),
        out_shape=jax.ShapeDtypeStruct((db, rows, A_HEAD_DIM), F32),
        grid_spec=pltpu.PrefetchScalarGridSpec(
            num_scalar_prefetch=1,
            grid=(db, nblk // nbs),
            in_specs=[per_seq(rows), const((rows, LANES)), const((rows, A_BLOCK * nh)), const((rows, LANES)),
                      per_seq(LANES), per_seq(LANES)] + pages + pages,
            out_specs=per_seq(rows),
            scratch_shapes=[pltpu.VMEM((rows, LANES), F32)] * 3 + [pltpu.VMEM((nblk, rows, A_HEAD_DIM), F32)],
        ),
        compiler_params=_cparams(("parallel", "arbitrary"), VMEM_LIMIT_BIG),
        name="moba_sample",
    )(page_table, qr, sl, mb, ob, new_rows(k_new), new_rows(v_new), *([cache_k] * npg), *([cache_v] * npg))
    out = out.reshape(db, nh, rq, A_HEAD_DIM)[:, :, :tn].transpose(0, 2, 1, 3)
    return out.reshape(db, tn, w)


def _head_sum(z):
    i = lax.broadcasted_iota(jnp.int32, (2 * LANES, LANES), 0) & (LANES - 1)
    j = lax.broadcasted_iota(jnp.int32, (2 * LANES, LANES), 1)
    member = jnp.where(lax.shift_right_logical(i, 6) == lax.shift_right_logical(j, 6), 1.0, 0.0).astype(BF16)
    return jnp.dot(jnp.concatenate(_split(z), axis=1), member, preferred_element_type=F32)


def _rwkv_prep_kernel(u_ref, pv_ref, s0_ref, mu_ref, w0_ref, w2_ref, a0_ref, a2_ref, g2_ref, kk_ref, ka_ref,
                      r_o, lw_o, k_o, v_o, kk_o, a_o, g_o, *, rw):
    t = pl.program_id(1)
    u = u_ref[0]
    tt = u.shape[0]
    prev_row = jnp.where(t == 0, s0_ref[0], pv_ref[0, SUBLANES - 1:SUBLANES, :])
    row = lax.broadcasted_iota(jnp.int32, u.shape, 0)
    prev = jnp.where(row == 0, prev_row, pltpu.roll(u, 1, axis=0))
    xs = u + (prev - u) * mu_ref[...]
    r = xs[:, 0:rw]
    k = xs[:, rw:2 * rw]
    v = xs[:, 2 * rw:3 * rw]
    lora_wa = xs[:, 3 * rw:3 * rw + LANES]
    gd = xs[:, 3 * rw + LANES:3 * rw + 2 * LANES]
    lora = lambda act, w_ref: _mm3(_split(act), _split(w_ref[...]))
    z = w0_ref[...] + lora(jnp.tanh(lora_wa), w2_ref)
    nz = -z
    softplus = jnp.maximum(nz, 0.0) + jnp.log1p(jnp.exp(-jnp.abs(nz)))
    w_log = -softplus - 0.5
    lw_o[0] = -jnp.exp(w_log)
    a = jax.nn.sigmoid(a0_ref[...] + lora(lora_wa, a2_ref))
    g_o[0] = lora(jax.nn.sigmoid(gd), g2_ref)
    kk = k * kk_ref[...]
    for p in range(rw // LANES):
        ps = slice(p * LANES, (p + 1) * LANES)
        kkp = kk[:, ps]
        kk_o[0, :, ps] = kkp / jnp.maximum(jnp.sqrt(_head_sum(kkp * kkp)), 1e-12)
    r_o[0] = r
    k_o[0] = k * (1.0 + (a - 1.0) * ka_ref[...])
    v_o[0] = v
    a_o[0] = a


def rwkv_prep(ur, shift0, prm):
    b, t, _ = ur.shape
    cols = prm['rw_mu'].shape[-1]
    rw = prm['rw_w0'].shape[-1]
    tt = _row_tile(t, 256)
    nlora = prm['rw_w2'].shape[0]
    assert 2 * nlora == LANES and prm['rw_g2'].shape[0] == LANES and cols == 3 * rw + 2 * LANES
    w2p = jnp.concatenate([prm['rw_w2'], jnp.zeros_like(prm['rw_w2'])], axis=0)
    a2p = jnp.concatenate([jnp.zeros_like(prm['rw_a2']), prm['rw_a2']], axis=0)
    row = lambda z: z.reshape(1, -1)
    full = lambda shape: pl.BlockSpec(shape, lambda i, j: (0,) * len(shape))
    outs = pl.pallas_call(
        functools.partial(_rwkv_prep_kernel, rw=rw),
        out_shape=[jax.ShapeDtypeStruct((b, t, rw), F32)] * 7,
        grid=(b, t // tt),
        in_specs=[
            pl.BlockSpec((1, tt, cols), lambda i, j: (i, j, 0)),
            pl.BlockSpec((1, SUBLANES, cols), lambda i, j: (i, jnp.maximum(j * (tt // SUBLANES) - 1, 0), 0)),
            pl.BlockSpec((1, 1, cols), lambda i, j: (i, 0, 0)),
            full((1, cols)), full((1, rw)), full((LANES, rw)), full((1, rw)), full((LANES, rw)),
            full((LANES, rw)), full((1, rw)), full((1, rw)),
        ],
        out_specs=[pl.BlockSpec((1, tt, rw), lambda i, j: (i, j, 0))] * 7,
        compiler_params=_cparams(("parallel", "parallel")),
        name="rwkv_prep",
    )(ur, ur, shift0.reshape(b, 1, cols), row(prm['rw_mu']), row(prm['rw_w0']), w2p, row(prm['rw_a0']), a2p,
      prm['rw_g2'], row(prm['rw_kk']), row(prm['rw_ka']))
    return outs


def _pair_masks():
    i = lax.broadcasted_iota(jnp.int32, (LANES, LANES), 0)
    j = lax.broadcasted_iota(jnp.int32, (LANES, LANES), 1)
    same = lax.shift_right_logical(i, 6) == lax.shift_right_logical(j, 6)
    return i, j, same


def _split(x):
    hi = x.astype(BF16)
    return hi, (x - hi.astype(F32)).astype(BF16)


def _mm3(a, b):
    n = b[0].shape[1]
    lhs = jnp.concatenate(a, axis=1)
    rhs = jnp.concatenate([jnp.concatenate(b, axis=1),
                           jnp.concatenate([b[0], jnp.zeros_like(b[0])], axis=1)], axis=0)
    out = jnp.dot(lhs, rhs, preferred_element_type=F32)
    return out[:, :n] + out[:, n:]


def _mm3_nt(a, b):
    n = b[0].shape[0]
    lhs = jnp.concatenate(a, axis=1)
    rhs = jnp.concatenate([jnp.concatenate([b[0], b[0]], axis=1),
                           jnp.concatenate([b[1], jnp.zeros_like(b[1])], axis=1)], axis=0)
    out = lax.dot_general(lhs, rhs, NT_DIMS, preferred_element_type=F32)
    return out[:, :n] + out[:, n:]


def _each(f, *lists):
    return [f(*xs) for xs in zip(*lists)]


def _rwkv_chunk_factors(r, lw, k, v, kk, a):
    c = R_CHUNK
    row = lax.broadcasted_iota(jnp.int32, (c, LANES), 0)
    cum = lw
    s = 1
    while s < c:
        cum = _each(lambda z: z + jnp.where(row >= s, pltpu.roll(z, s, axis=0), 0.0), cum)
        s *= 2
    cl = _each(lambda z: z[c - 1:c, :], cum)
    beta = _each(lambda x, y: x * y, kk, a)
    kap_t = _each(lambda x, cu, l: x * jnp.exp(cu - l), kk, cum, lw)
    r_t = _each(lambda x, cu: x * jnp.exp(cu), r, cum)
    e_inv = _each(lambda cu: jnp.exp(-cu), cum)
    b_t = _each(lambda x, e: x * e, beta, e_inv)
    k_t = _each(lambda x, e: x * e, k, e_inv)
    e_end = _each(lambda l, cu: jnp.exp(l - cu), cl, cum)
    b_h = _each(lambda x, e: x * e, beta, e_end)
    k_h = _each(lambda x, e: x * e, k, e_end)

    lo = lax.broadcasted_iota(jnp.int32, (c, LANES), 1) < R_HEAD_DIM
    stack = lambda x: jnp.concatenate([jnp.where(lo, x, 0.0), jnp.where(lo, 0.0, x)], axis=0)
    dup = lambda x: jnp.concatenate([x, x], axis=0)
    split_of = lambda f: (lambda x: _split(f(x)))
    ident = lambda x: x

    i, j, same = _pair_masks()
    strict = same & (j < i)
    incl = same & (j <= i)
    kap_s, r_s, v_s = _each(stack, kap_t), _each(stack, r_t), _each(stack, v)
    kap_p, r_p, v_p = _each(_split, kap_s), _each(_split, r_s), _each(_split, v_s)
    b_d, k_d = _each(split_of(dup), b_t), _each(split_of(dup), k_t)
    a_ab = _each(lambda x, y: jnp.where(strict, _mm3_nt(x, y), 0.0), kap_p, b_d)
    a_ak = _each(lambda x, y: jnp.where(strict, _mm3_nt(x, y), 0.0), kap_p, k_d)
    a_rb = _each(lambda x, y: jnp.where(incl, _mm3_nt(x, y), 0.0), r_p, b_d)
    a_rk = _each(lambda x, y: jnp.where(incl, _mm3_nt(x, y), 0.0), r_p, k_d)

    x = [jnp.where(i == j, 1.0, 0.0).astype(F32)] * len(r)
    s = 1
    while s < c:
        low = (lax.shift_right_logical(i, s.bit_length()) == lax.shift_right_logical(j, s.bit_length())) \
            & ((i & (2 * s - 1)) >= s) & ((j & (2 * s - 1)) < s)
        xp = _each(_split, x)
        ms = _each(lambda z: _split(jnp.where(low, z, 0.0)), a_ab)
        t1 = _each(split_of(ident), _each(_mm3, xp, ms))
        x = _each(lambda z, u, w_: z - _mm3(u, w_), x, t1, xp)
        s *= 2

    aakv = _each(_mm3, _each(_split, a_ak), v_p)
    arkv = _each(_mm3, _each(_split, a_rk), v_p)
    bh_p = _each(split_of(stack), b_h)
    vtk = _each(_mm3, _each(lambda z: _split(z.T), v_s), _each(split_of(stack), k_h))
    rhs = _each(lambda z, u: _split(jnp.concatenate([z, u], axis=1)), kap_s, aakv)
    wu = _each(lambda z, u: -_mm3(z, u), _each(_split, x), rhs)
    w = _each(lambda z: z[:, :LANES], wu)
    upre = _each(lambda z: z[:, LANES:], wu)
    arb_p = _each(_split, a_rb)
    r2 = _each(lambda z, u, w_: z + _mm3(u, _split(w_)), r_s, arb_p, w)
    ypre = _each(lambda z, u, w_: z + _mm3(u, _split(w_)), arkv, arb_p, upre)
    g = _each(lambda l, w_, u: jnp.where(i == j, jnp.exp(l), 0.0) + _mm3(_split(w_.T), u), cl, w, bh_p)
    spre = _each(lambda z, w_, u: z + _mm3(_split(w_.T), u), vtk, upre, bh_p)
    return list(zip(r2, ypre, g, spre))


def _rwkv_core_kernel(r_ref, lw_ref, k_ref, v_ref, kk_ref, a_ref, s0_ref, y_o, st_o, s_scr, *, nch, npair):
    cidx = pl.program_id(1)

    @pl.when(cidx == 0)
    def _():
        s_scr[...] = s0_ref[0]

    c = R_CHUNK
    parts = lambda ref: [ref[0, ch * c:(ch + 1) * c, p * LANES:(p + 1) * LANES]
                         for ch in range(nch) for p in range(npair)]
    factors = _rwkv_chunk_factors(parts(r_ref), parts(lw_ref), parts(k_ref), parts(v_ref), parts(kk_ref),
                                  parts(a_ref))
    s = [s_scr[p] for p in range(npair)]
    for ch in range(nch):
        for p in range(npair):
            r2, ypre, g, spre = factors[ch * npair + p]
            sp = _split(s[p])
            ys = _mm3_nt(_split(r2), sp) + ypre
            y_o[0, ch * c:(ch + 1) * c, p * LANES:(p + 1) * LANES] = ys[:c] + ys[c:]
            s[p] = _mm3(sp, _split(g)) + spre
    for p in range(npair):
        s_scr[p] = s[p]
        st_o[0, p] = s[p]


def rwkv_scan(r, lw, k, v, kk, a, wkv0):
    b, t, rw = r.shape
    npair = rw // LANES
    assert t % R_CHUNK == 0 and 2 * R_HEAD_DIM == LANES
    nch = 2 if t % (2 * R_CHUNK) == 0 else 1
    tt = nch * R_CHUNK
    w4 = wkv0.reshape(b, npair, 2, R_HEAD_DIM, R_HEAD_DIM)
    zero = jnp.zeros_like(w4[:, :, 0])
    s0 = jnp.concatenate([jnp.concatenate([w4[:, :, 0], zero], axis=-1),
                          jnp.concatenate([zero, w4[:, :, 1]], axis=-1)], axis=-2)
    act = pl.BlockSpec((1, tt, rw), lambda i, c: (i, c, 0))
    st_spec = pl.BlockSpec((1, npair, LANES, LANES), lambda i, c: (i, 0, 0, 0))
    y, st = pl.pallas_call(
        functools.partial(_rwkv_core_kernel, nch=nch, npair=npair),
        out_shape=[jax.ShapeDtypeStruct((b, t, rw), F32), jax.ShapeDtypeStruct((b, npair, LANES, LANES), F32)],
        grid=(b, t // tt),
        in_specs=[act] * 6 + [st_spec],
        out_specs=[act, st_spec],
        scratch_shapes=[pltpu.VMEM((npair, LANES, LANES), F32)],
        compiler_params=_cparams(("parallel", "arbitrary")),
        name="rwkv_core",
    )(r, lw, k, v, kk, a, s0)
    wkv = jnp.stack([st[:, :, :R_HEAD_DIM, :R_HEAD_DIM], st[:, :, R_HEAD_DIM:, R_HEAD_DIM:]], axis=2)
    return y, wkv.reshape(wkv0.shape)


def _rwkv_post_kernel(y_ref, r_ref, k_ref, v_ref, g_ref, rk_ref, lw_ref, lb_ref, o_ref):
    inv_n = 1.0 / R_HEAD_DIM
    hsum = _head_sum
    for p in range(y_ref.shape[2] // LANES):
        ps = slice(p * LANES, (p + 1) * LANES)
        y = y_ref[0, :, ps]
        mean = hsum(y) * inv_n
        d = y - mean
        var = hsum(d * d) * inv_n
        yn = d * lax.rsqrt(var + R_GN_EPS) * lw_ref[:, ps] + lb_ref[:, ps]
        bonus = hsum(r_ref[0, :, ps] * k_ref[0, :, ps] * rk_ref[:, ps]) * v_ref[0, :, ps]
        o_ref[0, :, ps] = (yn + bonus) * g_ref[0, :, ps]


def rwkv_post(y, r, k, v, g, prm):
    b, t, rw = y.shape
    tt = _row_tile(t, 256)
    act = pl.BlockSpec((1, tt, rw), lambda i, j: (i, j, 0))
    par = pl.BlockSpec((1, rw), lambda i, j: (0, 0))
    return pl.pallas_call(
        _rwkv_post_kernel,
        out_shape=jax.ShapeDtypeStruct((b, t, rw), F32),
        grid=(b, t // tt),
        in_specs=[act] * 5 + [par] * 3,
        out_specs=act,
        compiler_params=_cparams(("parallel", "parallel")),
        name="rwkv_post",
    )(y, r, k, v, g, prm['rw_rk'].reshape(1, rw), prm['rw_lnx_w'].reshape(1, rw), prm['rw_lnx_b'].reshape(1, rw))


def rwkv_mix(ur, shift0, wkv0, prm):
    b, t, _ = ur.shape
    tp = -(-t // SUBLANES) * SUBLANES
    urp = jnp.pad(ur, ((0, 0), (0, tp - t), (0, 0)))
    r, lw, k, v, kk, a, g = rwkv_prep(urp, shift0, prm)
    tc = -(-t // R_CHUNK) * R_CHUNK
    fit = lambda z: jnp.pad(z[:, :t], ((0, 0), (0, tc - t), (0, 0)))
    r, lw, k, v, kk, a, g = (fit(z) for z in (r, lw, k, v, kk, a, g))
    y, wkv = rwkv_scan(r, lw, k, v, kk, a, wkv0)
    out = rwkv_post(y, r, k, v, g, prm)
    return out[:, :t], wkv, ur[:, -1, :prm['rw_mu'].shape[-1]]


def _pool_kernel(u_ref, pv_ref, p0_ref, w_ref, sc_ref, o_ref, *, pos0):
    t = pl.program_id(1)
    cur = u_ref[0]
    tt = cur.shape[0]
    ext = jnp.concatenate([jnp.where(t == 0, p0_ref[0], pv_ref[0]), cur], axis=0)
    sums = []
    s = ext
    w = 1
    while w < POOL_MAX:
        s = s + pltpu.roll(s, w, axis=0)
        w *= 2
        sums.append(s)
    pos = pos0 + t * tt + lax.broadcasted_iota(jnp.int32, (tt, 1), 0)
    for gi, w in enumerate(POOL_WINDOWS):
        gs = slice(gi * LANES, (gi + 1) * LANES)
        win = sums[w.bit_length() - 2][POOL_MAX:, gs]
        cnt = jnp.minimum(pos + 1, w).astype(F32)
        m = win / cnt - cur[:, gs]
        z = jnp.dot(m.astype(BF16), w_ref[gi], preferred_element_type=F32)
        o_ref[0, :, gs] = z * sc_ref[:, gs]


def pool_mix(uc, pool0, pos0, pool_w, pool_scale):
    b, t, width = uc.shape
    assert width == len(POOL_WINDOWS) * LANES and pool_w.shape[1] == LANES
    tp = -(-t // SUBLANES) * SUBLANES
    ucp = jnp.pad(uc, ((0, 0), (0, tp - t), (0, 0)))
    tt = _row_tile(tp, 256)
    assert tt % POOL_MAX == 0 or tp == tt
    p0 = jnp.pad(pool0, ((0, 0), (1, 0), (0, 0)))
    nprev = tt // POOL_MAX if tt % POOL_MAX == 0 else 0
    ext = jnp.concatenate([pool0, uc], axis=1)
    if nprev:
        pv, pv_spec = ucp, pl.BlockSpec((1, POOL_MAX, width), lambda i, j: (i, jnp.maximum(j * nprev - 1, 0), 0))
    else:
        pv, pv_spec = p0, pl.BlockSpec((1, POOL_MAX, width), lambda i, j: (i, 0, 0))
    z = pl.pallas_call(
        functools.partial(_pool_kernel, pos0=pos0),
        out_shape=jax.ShapeDtypeStruct((b, tp, width), F32),
        grid=(b, tp // tt),
        in_specs=[
            pl.BlockSpec((1, tt, width), lambda i, j: (i, j, 0)),
            pv_spec,
            pl.BlockSpec((1, POOL_MAX, width), lambda i, j: (i, 0, 0)),
            pl.BlockSpec(pool_w.shape, lambda i, j: (0, 0, 0)),
            pl.BlockSpec((1, width), lambda i, j: (0, 0)),
        ],
        out_specs=pl.BlockSpec((1, tt, width), lambda i, j: (i, j, 0)),
        compiler_params=_cparams(("parallel", "parallel")),
        name="pool_mix",
    )(ucp, pv, p0, pool_w.astype(BF16), pool_scale.reshape(1, width))
    return z[:, :t], ext[:, -(POOL_MAX - 1):]


def _mix_out_kernel(x_ref, ya_ref, yr_ref, yc_ref, wa_ref, wr_ref, wc_ref, o_ref):
    acc = jnp.dot(ya_ref[...].astype(BF16), wa_ref[...], preferred_element_type=F32)
    acc += jnp.dot(yr_ref[...].astype(BF16), wr_ref[...], preferred_element_type=F32)
    acc += jnp.dot(yc_ref[...].astype(BF16), wc_ref[...], preferred_element_type=F32)
    o_ref[...] = x_ref[...] + acc


def mix_out(x, ya, yr, yc, w_out, layer):
    n, d = x.shape
    wa_, wr_, wc_ = ya.shape[1], yr.shape[1], yc.shape[1]
    assert wa_ % wr_ == 0 and wr_ == wc_
    tm = _row_tile(n, 512)
    tn = d
    act = lambda wd: pl.BlockSpec((tm, wd), lambda i, j: (i, 0))
    wsp = lambda wd, blk: pl.BlockSpec((None, wd, tn), lambda i, j: (layer, blk, j))
    return pl.pallas_call(
        _mix_out_kernel,
        out_shape=jax.ShapeDtypeStruct((n, d), F32),
        grid=(n // tm, d // tn),
        in_specs=[pl.BlockSpec((tm, tn), lambda i, j: (i, j)), act(wa_), act(wr_), act(wc_),
                  wsp(wa_, 0), wsp(wr_, wa_ // wr_), wsp(wc_, (wa_ + wr_) // wc_)],
        out_specs=pl.BlockSpec((tm, tn), lambda i, j: (i, j)),
        compiler_params=_cparams(("parallel", "arbitrary")),
        name="mix_out",
    )(x, ya, yr, yc, w_out, w_out, w_out)


def _cross_kernel(x_ref, g_ref, wq_ref, mk_ref, mv_ref, wo_ref, o_ref, *, nh):
    x = x_ref[0]
    h = _rms(x, g_ref[...]).astype(BF16)
    q = jnp.dot(h, wq_ref[...], preferred_element_type=F32)
    scale = M_HEAD_DIM ** -0.5
    outs = []
    for hh in range(nh):
        hs = slice(hh * M_HEAD_DIM, (hh + 1) * M_HEAD_DIM)
        s = lax.dot_general(q[:, hs].astype(BF16), mk_ref[0, :, hs].astype(BF16), NT_DIMS,
                            preferred_element_type=F32) * scale
        p = jnp.exp(s - s.max(axis=-1, keepdims=True))
        l = jnp.sum(p, axis=-1, keepdims=True)
        outs.append(jnp.dot(p.astype(BF16), mv_ref[0, :, hs].astype(BF16), preferred_element_type=F32) / l)
    o = jnp.concatenate(outs, axis=-1).astype(BF16)
    o_ref[0] = x + jnp.dot(o, wo_ref[...], preferred_element_type=F32)


def cross_attend(x, g, wq, mk, mv, wo):
    b, t0, d = x.shape
    t = -(-t0 // SUBLANES) * SUBLANES
    x = jnp.pad(x, ((0, 0), (0, t - t0), (0, 0)))
    mw = wq.shape[1]
    nm = mk.shape[1]
    tm = _row_tile(t, 512)
    out = pl.pallas_call(
        functools.partial(_cross_kernel, nh=mw // M_HEAD_DIM),
        out_shape=jax.ShapeDtypeStruct((b, t, d), F32),
        grid=(b, t // tm),
        in_specs=[
            pl.BlockSpec((1, tm, d), lambda i, j: (i, j, 0)),
            pl.BlockSpec((1, d), lambda i, j: (0, 0)),
            pl.BlockSpec((d, mw), lambda i, j: (0, 0)),
            pl.BlockSpec((1, nm, mw), lambda i, j: (i, 0, 0)),
            pl.BlockSpec((1, nm, mw), lambda i, j: (i, 0, 0)),
            pl.BlockSpec((mw, d), lambda i, j: (0, 0)),
        ],
        out_specs=pl.BlockSpec((1, tm, d), lambda i, j: (i, j, 0)),
        compiler_params=_cparams(("parallel", "parallel")),
        name="cross_attend",
    )(x, g.reshape(1, d), wq, mk, mv, wo)
    return out[:, :t0]


def _layer(x, prm, moba_fn, pos0, shift0, wkv0, pool0, mem_k, mem_v):
    b, t, d = x.shape
    n = b * t
    lyr, stk = prm['layer'], prm['stacked']
    x2 = ffn_half_step(x.reshape(n, d), prm['norm_ffn1'], stk['ffn1_gate'], stk['ffn1_up'], stk['ffn1_down'], lyr)
    aw, rc, cw = prm['a_width'], prm['r_cols'], prm['c_width']
    qa, ka, va, ur, uc = (z.reshape(b, t, -1) for z in
                          in_proj(x2, prm['norm_mix'], stk['w_in'], lyr, (aw, aw, aw, rc, cw)))
    ya = moba_fn(qa, ka, va)
    yr, wkv, shift = rwkv_mix(ur, shift0, wkv0, prm)
    yc, pool_buf = pool_mix(uc, pool0, pos0, prm['pool_w'], prm['pool_scale'])
    x3 = mix_out(x2, ya.reshape(n, aw), yr.reshape(n, -1), yc.reshape(n, cw), stk['w_out'], lyr)
    x4 = cross_attend(x3.reshape(b, t, d), prm['norm_cross'], prm['mem_wq'], mem_k, mem_v, prm['mem_wo'])
    x5 = ffn_half_step(x4.reshape(n, d), prm['norm_ffn2'], stk['ffn2_gate'], stk['ffn2_up'], stk['ffn2_down'], lyr)
    return x5.reshape(b, t, d), ka, va, wkv, shift, pool_buf


def kernel(x_prompt, x_sample, cache_k, cache_v, cache_mem_k, cache_mem_v, state_wkv, state_shift, state_pool, page_table, mem_prompt, norm_ffn1, ffn1_gate, ffn1_up, ffn1_down, norm_mix, w_in, w_out, rw_mu, rw_w0, rw_w2, rw_a0, rw_a2, rw_g2, rw_kk, rw_ka, rw_rk, rw_lnx_w, rw_lnx_b, pool_w, pool_scale, norm_cross, norm_mem, mem_wq, mem_wk, mem_wv, mem_wo, norm_ffn2, ffn2_gate, ffn2_up, ffn2_down, norm_final):
    depth = w_in.shape[0]
    bp, tp, d = x_prompt.shape
    db, ts, _ = x_sample.shape
    n_heads, hd = cache_k.shape[3], cache_k.shape[4]
    aw = n_heads * hd
    r_heads, rn = rw_rk.shape[1], rw_rk.shape[2]
    r_cols = rw_mu.shape[1]
    c_width = pool_scale.shape[1]
    nm, m_heads, mhd = cache_mem_k.shape[2:]
    mw = m_heads * mhd
    assert hd == A_HEAD_DIM and rn == R_HEAD_DIM and mhd == M_HEAD_DIM
    past = page_table.shape[1] * PAGE_SIZE
    slopes = jnp.exp2(-8.0 * jnp.arange(1, n_heads + 1, dtype=F32) / n_heads)
    ck = cache_k.reshape(depth, cache_k.shape[1], PAGE_SIZE * n_heads, hd)
    cv = cache_v.reshape(depth, cache_v.shape[1], PAGE_SIZE * n_heads, hd)
    bf = lambda z: z.astype(BF16)
    bounds = (0, aw, 2 * aw, 3 * aw, 3 * aw + r_cols, w_in.shape[2])
    groups = [bf(w_in[:, :, lo:hi]) for lo, hi in zip(bounds[:-1], bounds[1:])]
    w_in_p = jnp.concatenate([jnp.pad(gw, ((0, 0), (0, 0), (0, -gw.shape[2] % IN_TILE))) for gw in groups], axis=2)
    stacked = dict(ffn1_gate=bf(ffn1_gate), ffn1_up=bf(ffn1_up), ffn1_down=bf(ffn1_down), w_in=w_in_p,
                   w_out=bf(w_out), ffn2_gate=bf(ffn2_gate), ffn2_up=bf(ffn2_up), ffn2_down=bf(ffn2_down))
    big = dict(mem_wq=bf(mem_wq), mem_wk=bf(mem_wk), mem_wv=bf(mem_wv), mem_wo=bf(mem_wo))
    small = dict(norm_ffn1=norm_ffn1, norm_mix=norm_mix, rw_mu=rw_mu, rw_w0=rw_w0, rw_w2=rw_w2, rw_a0=rw_a0,
                 rw_a2=rw_a2, rw_g2=rw_g2, rw_kk=rw_kk, rw_ka=rw_ka, rw_rk=rw_rk.reshape(depth, -1),
                 rw_lnx_w=rw_lnx_w, rw_lnx_b=rw_lnx_b, pool_w=pool_w, pool_scale=pool_scale,
                 norm_cross=norm_cross, norm_ffn2=norm_ffn2)

    shift0 = jnp.zeros((bp, r_cols), F32)
    wkv0 = jnp.zeros((bp, r_heads, rn, rn), F32)
    pool0 = jnp.zeros((bp, POOL_MAX - 1, c_width), F32)
    xp, xs = x_prompt, x_sample
    outs = [[] for _ in range(12)]
    for l in range(depth):
        prm = {k: v[l] for k, v in big.items()}
        prm.update({k: v[l] for k, v in small.items()})
        prm.update(a_width=aw, r_cols=r_cols, c_width=c_width, layer=l, stacked=stacked)
        hm = rmsnorm(mem_prompt.reshape(bp * nm, d), norm_mem[l], BF16)
        mk = matmul(hm, prm['mem_wk'], mw).reshape(bp, nm, mw)
        mv = matmul(hm, prm['mem_wv'], mw).reshape(bp, nm, mw)
        moba_p = functools.partial(moba_prompt, slopes=slopes)
        xp, k_, v_, w_, sh_, pl_ = _layer(xp, prm, moba_p, 0, shift0, wkv0, pool0, mk, mv)
        res_p = (k_, v_, w_, sh_, pl_, mk, mv)
        moba_s = functools.partial(moba_sample, cache_k=ck, cache_v=cv, layer=l, page_table=page_table,
                                   slopes=slopes)
        xs, k_, v_, w_, sh_, pl_ = _layer(xs, prm, moba_s, past, state_shift[l], state_wkv[l], state_pool[l],
                                           cache_mem_k[l].reshape(db, nm, mw), cache_mem_v[l].reshape(db, nm, mw))
        res_s = (k_, v_, w_, sh_, pl_)
        for i, z in enumerate(res_p + res_s):
            outs[i].append(z)
    y_prompt = rmsnorm(xp.reshape(bp * tp, d), norm_final, F32).reshape(bp, tp, d)
    y_sample = rmsnorm(xs.reshape(db * ts, d), norm_final, F32).reshape(db, ts, d)
    stk = [jnp.stack(o) for o in outs]
    heads = lambda z, nh_, hd_: z.reshape(z.shape[:-1] + (nh_, hd_))
    for i in (0, 1, 7, 8):
        stk[i] = heads(stk[i], n_heads, hd)
    for i in (5, 6):
        stk[i] = heads(stk[i], m_heads, mhd)
    return (y_prompt, y_sample) + tuple(stk)
```

```python
import functools

import jax
import jax.numpy as jnp
from jax import lax
from jax.experimental import pallas as pl
from jax.experimental.pallas import tpu as pltpu

F32 = jnp.float32
BF16 = jnp.bfloat16
HI = lax.Precision.HIGHEST

RMS_EPS = 1e-6
LANES = 128
SUBLANES = 8
VMEM_LIMIT = 48 * 1024 * 1024

PAGE_SIZE = 128
A_HEAD_DIM = 128
A_BLOCK = 256
A_TOPK = 3
A_QCHUNK = 128
R_HEAD_DIM = 64
R_CHUNK = 64
R_GN_EPS = 64e-5
POOL_WINDOWS = (2, 4, 8, 16)
POOL_MAX = 16
M_HEAD_DIM = 128
NEG = -1e30

NT_DIMS = (((1,), (1,)), ((), ()))
TN_DIMS = (((0,), (0,)), ((), ()))


VMEM_LIMIT_BIG = 58 * 1024 * 1024


def _cparams(sem, vmem_limit=VMEM_LIMIT):
    return pltpu.CompilerParams(dimension_semantics=sem, vmem_limit_bytes=vmem_limit)


def _row_tile(n, pref):
    return pref if n % pref == 0 else n


def _rms(x, g):
    ms = jnp.mean(x * x, axis=-1, keepdims=True)
    return x * lax.rsqrt(ms + RMS_EPS) * g


def _rms_kernel(x_ref, g_ref, o_ref):
    o_ref[...] = _rms(x_ref[...], g_ref[...]).astype(o_ref.dtype)


def rmsnorm(x, g, out_dtype):
    n, d = x.shape
    tm = _row_tile(n, 512)
    return pl.pallas_call(
        _rms_kernel,
        out_shape=jax.ShapeDtypeStruct((n, d), out_dtype),
        grid=(n // tm,),
        in_specs=[pl.BlockSpec((tm, d), lambda i: (i, 0)), pl.BlockSpec((1, d), lambda i: (0, 0))],
        out_specs=pl.BlockSpec((tm, d), lambda i: (i, 0)),
        compiler_params=_cparams(("parallel",)),
        name="rmsnorm",
    )(x, g.reshape(1, d))


def _matmul_kernel(a_ref, w_ref, o_ref):
    o_ref[...] = jnp.dot(a_ref[...], w_ref[...], preferred_element_type=F32)


def matmul(a, w, tn):
    n, k = a.shape
    m = w.shape[1]
    tm = _row_tile(n, 512)
    return pl.pallas_call(
        _matmul_kernel,
        out_shape=jax.ShapeDtypeStruct((n, m), F32),
        grid=(n // tm, m // tn),
        in_specs=[pl.BlockSpec((tm, k), lambda i, j: (i, 0)), pl.BlockSpec((k, tn), lambda i, j: (0, j))],
        out_specs=pl.BlockSpec((tm, tn), lambda i, j: (i, j)),
        compiler_params=_cparams(("parallel", "arbitrary")),
        name="matmul",
    )(a, w)


IN_TILE = 512


def _in_proj_kernel(x_ref, g_ref, w_ref, *rest, first_tile):
    out_refs, h_scr = rest[:-1], rest[-1]
    j = pl.program_id(1)

    @pl.when(j == 0)
    def _():
        h_scr[...] = _rms(x_ref[...], g_ref[...]).astype(BF16)

    for o_ref, lo, hi in zip(out_refs, first_tile[:-1], first_tile[1:]):
        @pl.when((j >= lo) & (j < hi))
        def _(o_ref=o_ref):
            o_ref[...] = jnp.dot(h_scr[...], w_ref[...], preferred_element_type=F32)


def in_proj(x, g, w, layer, widths):
    n, d = x.shape
    tm = _row_tile(n, 1024)
    tiles = [-(-wd // IN_TILE) for wd in widths]
    first_tile = [0]
    for nt in tiles:
        first_tile.append(first_tile[-1] + nt)
    assert w.shape[2] == first_tile[-1] * IN_TILE

    def out_spec(lo, nt):
        return pl.BlockSpec((tm, IN_TILE), lambda i, j: (i, jnp.clip(j - lo, 0, nt - 1)))

    return pl.pallas_call(
        functools.partial(_in_proj_kernel, first_tile=tuple(first_tile)),
        out_shape=[jax.ShapeDtypeStruct((n, nt * IN_TILE), F32) for nt in tiles],
        grid=(n // tm, first_tile[-1]),
        in_specs=[pl.BlockSpec((tm, d), lambda i, j: (i, 0)), pl.BlockSpec((1, d), lambda i, j: (0, 0)),
                  pl.BlockSpec((None, d, IN_TILE), lambda i, j: (layer, 0, j))],
        out_specs=[out_spec(lo, nt) for lo, nt in zip(first_tile[:-1], tiles)],
        scratch_shapes=[pltpu.VMEM((tm, d), BF16)],
        compiler_params=_cparams(("parallel", "arbitrary"), VMEM_LIMIT_BIG),
        name="in_proj",
    )(x, g.reshape(1, d), w)


FFN_OUT_CHUNK = 512


def _ffn_kernel(x_ref, g_ref, *rest, final_norm):
    gf_ref = rest[0] if final_norm else None
    wg_ref, wu_ref, wd_ref, o_ref, h_scr = rest[-5:]
    j = pl.program_id(1)

    @pl.when(j == 0)
    def _():
        x = x_ref[...]
        h_scr[...] = _rms(x, g_ref[...]).astype(BF16)
        o_ref[...] = 2.0 * x

    h = h_scr[...]
    gate = jnp.dot(h, wg_ref[...], preferred_element_type=F32)
    up = jnp.dot(h, wu_ref[...], preferred_element_type=F32)
    act = (gate * jax.nn.sigmoid(gate) * up).astype(BF16)
    for c0 in range(0, o_ref.shape[1], FFN_OUT_CHUNK):
        cols = slice(c0, c0 + FFN_OUT_CHUNK)
        o_ref[:, cols] += jnp.dot(act, wd_ref[:, cols], preferred_element_type=F32)

    @pl.when(j == pl.num_programs(1) - 1)
    def _():
        y = 0.5 * o_ref[...]
        o_ref[...] = _rms(y, gf_ref[...]) if final_norm else y


def ffn_half_step(x, g, wg, wu, wd, layer, final_gain=None, tf=256):
    n, d = x.shape
    f = wg.shape[2]
    tm = _row_tile(n, 1024)
    assert d % FFN_OUT_CHUNK == 0
    final_norm = final_gain is not None
    gain_spec = pl.BlockSpec((1, d), lambda i, j: (0, 0))
    gains = [g.reshape(1, d)] + ([final_gain.reshape(1, d)] if final_norm else [])
    return pl.pallas_call(
        functools.partial(_ffn_kernel, final_norm=final_norm),
        out_shape=jax.ShapeDtypeStruct((n, d), F32),
        grid=(n // tm, f // tf),
        in_specs=[pl.BlockSpec((tm, d), lambda i, j: (i, 0))] + [gain_spec] * len(gains) + [
            pl.BlockSpec((None, d, tf), lambda i, j: (layer, 0, j)),
            pl.BlockSpec((None, d, tf), lambda i, j: (layer, 0, j)),
            pl.BlockSpec((None, tf, d), lambda i, j: (layer, j, 0)),
        ],
        out_specs=pl.BlockSpec((tm, d), lambda i, j: (i, 0)),
        scratch_shapes=[pltpu.VMEM((tm, d), BF16)],
        compiler_params=_cparams(("parallel", "arbitrary"), VMEM_LIMIT_BIG),
        name="ffn_half_step",
    )(x, *gains, wg, wu, wd)


def _moba_prompt_kernel(slope_ref, q_ref, k_ref, v_ref, o_ref, kb_scr, vb_scr, km_scr, *, nb):
    h = pl.program_id(1)
    own = pl.program_id(2)
    qc = A_BLOCK
    hps, nbp = km_scr.shape[0], km_scr.shape[1]
    heads = range(hps)
    hsl = lambda i: slice(i * A_HEAD_DIM, (i + 1) * A_HEAD_DIM)

    @pl.when(own == 0)
    def _():
        k = k_ref[0]
        kb_scr[...] = k.astype(BF16)
        vb_scr[...] = v_ref[0].astype(BF16)
        km_scr[...] = jnp.zeros_like(km_scr)
        for i in heads:
            for n in range(nb):
                km_scr[i, n:n + 1, :] = jnp.sum(k[n * A_BLOCK:(n + 1) * A_BLOCK, hsl(i)], axis=0,
                                                keepdims=True) * (1.0 / A_BLOCK)

    q = [q_ref[0, :, hsl(i)] for i in heads]
    slope = [slope_ref[h * hps + i] for i in heads]
    scale = A_HEAD_DIM ** -0.5

    g = [_mm3_nt(_split(km_scr[i]), _split(q[i])) for i in heads]
    blk = lax.broadcasted_iota(jnp.int32, (nbp, qc), 0)
    gm = [jnp.where(blk < own, g_, -jnp.inf) for g_ in g]
    sel_t = [jnp.zeros((nbp, qc), F32) for _ in heads]
    for n in range(nb):
        for i in heads:
            gn = gm[i][n:n + 1, :]
            beats = (gm[i] > gn) | ((gm[i] == gn) & (blk < n))
            cnt = jnp.sum(jnp.where(beats, 1.0, 0.0), axis=0, keepdims=True)
            sel_n = jnp.where(cnt < A_TOPK, 1.0, 0.0) * jnp.where(n < own, 1.0, 0.0)
            sel_t[i] = jnp.where(blk == n, sel_n, sel_t[i])
    sel_b = [st.astype(BF16) for st in sel_t]

    log2e = 1.4426950408889634
    qb = [(q_ * (scale * log2e)).astype(BF16) for q_ in q]
    slope = [sl * log2e for sl in slope]
    row = lax.broadcasted_iota(jnp.int32, (qc, A_BLOCK), 0)
    off_k = lax.broadcasted_iota(jnp.int32, (qc, A_BLOCK), 1)

    def attend(own_k):
        nk = own_k + 1
        s = [lax.dot_general(qb[i], kb_scr[0:nk * A_BLOCK, hsl(i)], NT_DIMS, preferred_element_type=F32)
             for i in heads]
        if own_k:
            blk_of_key = lax.broadcasted_iota(jnp.int32, (nbp, own_k * A_BLOCK), 1) // A_BLOCK
            spread = jnp.where(blk_of_key == lax.broadcasted_iota(jnp.int32, (nbp, own_k * A_BLOCK), 0),
                               1.0, 0.0).astype(BF16)
            picked = [lax.dot_general(sel_b[i], spread, TN_DIMS, preferred_element_type=F32) for i in heads]
        blocks = [[] for _ in heads]
        for n in range(nk):
            for i in heads:
                sn = s[i][:, n * A_BLOCK:(n + 1) * A_BLOCK] + slope[i] * (n * A_BLOCK + off_k[0:1, :]).astype(F32)
                if n < own_k:
                    allow = picked[i][:, n * A_BLOCK:(n + 1) * A_BLOCK] > 0.5
                else:
                    allow = off_k <= row
                blocks[i].append(jnp.where(allow, sn, NEG))
        m = [blocks[i][0].max(axis=-1, keepdims=True) for i in heads]
        for n in range(1, nk):
            m = [jnp.maximum(m[i], blocks[i][n].max(axis=-1, keepdims=True)) for i in heads]
        l = [jnp.zeros((qc, 1), F32) for _ in heads]
        o = [jnp.zeros((qc, A_HEAD_DIM), F32) for _ in heads]
        for n in range(nk):
            for i in heads:
                p = jnp.exp2(blocks[i][n] - m[i])
                l[i] = l[i] + jnp.sum(p, axis=-1, keepdims=True)
                o[i] = o[i] + jnp.dot(p.astype(BF16), vb_scr[n * A_BLOCK:(n + 1) * A_BLOCK, hsl(i)],
                                      preferred_element_type=F32)
        for i in heads:
            o_ref[0, :, hsl(i)] = o[i] / l[i]

    for own_k in range(nb):
        pl.when(own == own_k)(functools.partial(attend, own_k))


def moba_prompt(q, k, v, slopes):
    b, t, w = q.shape
    nh = w // A_HEAD_DIM
    nb = t // A_BLOCK
    assert t % A_BLOCK == 0 and nb >= A_TOPK and nb <= LANES and A_BLOCK % A_QCHUNK == 0
    nbp = -(-nb // SUBLANES) * SUBLANES
    hps = 2 if nh % 2 == 0 else 1
    hw = hps * A_HEAD_DIM
    return pl.pallas_call(
        functools.partial(_moba_prompt_kernel, nb=nb),
        out_shape=jax.ShapeDtypeStruct((b, t, w), F32),
        grid=(b, nh // hps, nb),
        in_specs=[
            pl.BlockSpec(memory_space=pltpu.SMEM),
            pl.BlockSpec((1, A_BLOCK, hw), lambda i, h, c: (i, c, h)),
            pl.BlockSpec((1, t, hw), lambda i, h, c: (i, 0, h)),
            pl.BlockSpec((1, t, hw), lambda i, h, c: (i, 0, h)),
        ],
        out_specs=pl.BlockSpec((1, A_BLOCK, hw), lambda i, h, c: (i, c, h)),
        scratch_shapes=[
            pltpu.VMEM((t, hw), BF16),
            pltpu.VMEM((t, hw), BF16),
            pltpu.VMEM((hps, nbp, A_HEAD_DIM), F32),
        ],
        compiler_params=_cparams(("parallel", "parallel", "arbitrary")),
        name="moba_prompt",
    )(slopes, q, k, v)


def _moba_sample_kernel(pt_ref, q_ref, sl_ref, mb_ref, ob_ref, kn_ref, vn_ref, *rest, nh, nblk, past):
    page_refs, (o_ref, g_acc, m_acc, l_acc, o_scr) = rest[:-5], rest[-5:]
    n = pl.program_id(1)
    scale = A_HEAD_DIM ** -0.5
    rq = SUBLANES
    q = q_ref[0]
    lane = lax.broadcasted_iota(jnp.int32, (nh * rq, LANES), 1)

    @pl.when(n == 0)
    def _():
        g_acc[...] = jnp.zeros_like(g_acc)
        m_acc[...] = jnp.zeros_like(m_acc)
        l_acc[...] = jnp.zeros_like(l_acc)

    nbs = len(page_refs) // 4
    k_refs, v_refs = page_refs[:2 * nbs], page_refs[2 * nbs:]
    blk_ids = [n * nbs + jb for jb in range(nbs)]
    kpages = [(k_refs[2 * jb][...], k_refs[2 * jb + 1][...]) for jb in range(nbs)]
    head_sum = lambda pg: jnp.sum(pg.reshape(PAGE_SIZE, nh, A_HEAD_DIM), axis=0)
    kmean = [(head_sum(ka) + head_sum(kb)) * (1.0 / A_BLOCK) for ka, kb in kpages]
    kmean_rows = [jnp.concatenate([jnp.broadcast_to(km[h:h + 1], (rq, A_HEAD_DIM)) for h in range(nh)], axis=0)
                  for km in kmean]
    g = [jnp.sum(q * kr, axis=-1, keepdims=True) for kr in kmean_rows]
    kblk = [jnp.concatenate([ka, kb], axis=0).astype(BF16) for ka, kb in kpages]
    vblk = [jnp.concatenate([v_refs[2 * jb][...], v_refs[2 * jb + 1][...]], axis=0).astype(BF16)
            for jb in range(nbs)]
    qb = q.astype(BF16)
    s = [lax.dot_general(qb, kb_, NT_DIMS, preferred_element_type=F32) * scale + mb_ref[...] for kb_ in kblk]
    m = [s_.max(axis=-1, keepdims=True) for s_ in s]
    p = [jnp.exp(s_ - m_) for s_, m_ in zip(s, m)]
    l = [jnp.sum(p_, axis=-1, keepdims=True) for p_ in p]
    o = [jnp.dot(p_.astype(BF16), vb_, preferred_element_type=F32) for p_, vb_ in zip(p, vblk)]
    g_all, m_all, l_all = g_acc[...], m_acc[...], l_acc[...]
    for jb, blk in enumerate(blk_ids):
        here = lane == blk
        g_all = jnp.where(here, g[jb], g_all)
        m_all = jnp.where(here, m[jb] - sl_ref[:, 0:1] * (past - blk * A_BLOCK).astype(F32), m_all)
        l_all = jnp.where(here, l[jb], l_all)
        o_scr[blk] = o[jb]
    g_acc[...] = g_all
    m_acc[...] = m_all
    l_acc[...] = l_all

    @pl.when(n == nblk // nbs - 1)
    def _():
        gm = jnp.where(lane < nblk, g_acc[...], -jnp.inf)
        sel = jnp.zeros(gm.shape, F32)
        for i in range(nblk):
            gi = gm[:, i:i + 1]
            beats = (gm > gi) | ((gm == gi) & (lane < i))
            cnt = jnp.sum(jnp.where(beats, 1.0, 0.0), axis=-1, keepdims=True)
            sel = jnp.where(lane == i, jnp.where(cnt < A_TOPK, 1.0, 0.0), sel)
        selb = sel > 0.5
        mblk = m_acc[...]
        s_own = lax.dot_general(qb, kn_ref[0].astype(BF16), NT_DIMS, preferred_element_type=F32) * scale + ob_ref[...]
        mx = jnp.maximum(s_own.max(axis=-1, keepdims=True),
                         jnp.where(selb, mblk, NEG).max(axis=-1, keepdims=True))
        p_own = jnp.exp(s_own - mx)
        wgt = jnp.where(selb, jnp.exp(mblk - mx), 0.0)
        den = jnp.sum(p_own, axis=-1, keepdims=True) + jnp.sum(wgt * l_acc[...], axis=-1, keepdims=True)
        num = jnp.dot(p_own.astype(BF16), vn_ref[0].astype(BF16), preferred_element_type=F32)
        for i in range(nblk):
            num = num + wgt[:, i:i + 1] * o_scr[i]
        o_ref[0] = num / den


def moba_sample(q, k_new, v_new, cache_k, cache_v, layer, page_table, slopes):
    db, tn, w = q.shape
    nh = w // A_HEAD_DIM
    n_pages = page_table.shape[1]
    past = n_pages * PAGE_SIZE
    ppb = A_BLOCK // PAGE_SIZE
    rq = SUBLANES
    assert past % A_BLOCK == 0 and ppb == 2 and tn <= rq and tn * nh <= LANES and nh == SUBLANES
    nblk = past // A_BLOCK
    assert A_TOPK <= nblk <= LANES
    rows = nh * rq
    qr = jnp.pad(q.reshape(db, tn, nh, A_HEAD_DIM).transpose(0, 2, 1, 3), ((0, 0), (0, 0), (0, rq - tn), (0, 0)))
    qr = qr.reshape(db, rows, A_HEAD_DIM)
    new_rows = lambda z: jnp.pad(z.reshape(db, tn * nh, A_HEAD_DIM), ((0, 0), (0, LANES - tn * nh), (0, 0)))
    r_h = jnp.arange(rows, dtype=jnp.int32)[:, None] // rq
    r_t = jnp.arange(rows, dtype=jnp.int32)[:, None] % rq
    slope_r = slopes[r_h[:, 0]][:, None]
    col = jnp.arange(A_BLOCK * nh, dtype=jnp.int32)[None, :]
    mb = jnp.where(col % nh == r_h, -slope_r * (r_t - col // nh).astype(F32), NEG)
    colo = jnp.arange(LANES, dtype=jnp.int32)[None, :]
    jo = colo // nh
    ob = jnp.where((colo % nh == r_h) & (jo <= r_t) & (jo < tn), -slope_r * (r_t - jo).astype(F32), NEG)
    sl = jnp.broadcast_to(slope_r, (rows, LANES))

    nbs = max(n for n in (1, 2, 4, 8) if nblk % n == 0)
    npg = ppb * nbs

    def page_spec(j):
        return pl.BlockSpec((None, None, PAGE_SIZE * nh, A_HEAD_DIM),
                            lambda i, n, pt: (layer, pt[i, npg * n + j], 0, 0))

    const = lambda shape: pl.BlockSpec(shape, lambda i, n, pt: (0,) * len(shape))
    per_seq = lambda r: pl.BlockSpec((1, r, A_HEAD_DIM), lambda i, n, pt: (i, 0, 0))
    pages = [page_spec(j) for j in range(npg)]
    out = pl.pallas_call(
        functools.partial(_moba_sample_kernel, nh=nh, nblk=nblk, past=past),
        out_shape=jax.ShapeDtypeStruct((db, rows, A_HEAD_DIM), F32),
        grid_spec=pltpu.PrefetchScalarGridSpec(
            num_scalar_prefetch=1,
            grid=(db, nblk // nbs),
            in_specs=[per_seq(rows), const((rows, LANES)), const((rows, A_BLOCK * nh)), const((rows, LANES)),
                      per_seq(LANES), per_seq(LANES)] + pages + pages,
            out_specs=per_seq(rows),
            scratch_shapes=[pltpu.VMEM((rows, LANES), F32)] * 3 + [pltpu.VMEM((nblk, rows, A_HEAD_DIM), F32)],
        ),
        compiler_params=_cparams(("parallel", "arbitrary"), VMEM_LIMIT_BIG),
        name="moba_sample",
    )(page_table, qr, sl, mb, ob, new_rows(k_new), new_rows(v_new), *([cache_k] * npg), *([cache_v] * npg))
    out = out.reshape(db, nh, rq, A_HEAD_DIM)[:, :, :tn].transpose(0, 2, 1, 3)
    return out.reshape(db, tn, w)


def _head_sum(z):
    i = lax.broadcasted_iota(jnp.int32, (2 * LANES, LANES), 0) & (LANES - 1)
    j = lax.broadcasted_iota(jnp.int32, (2 * LANES, LANES), 1)
    member = jnp.where(lax.shift_right_logical(i, 6) == lax.shift_right_logical(j, 6), 1.0, 0.0).astype(BF16)
    return jnp.dot(jnp.concatenate(_split(z), axis=1), member, preferred_element_type=F32)


def _rwkv_prep_kernel(u_ref, pv_ref, s0_ref, mu_ref, w0_ref, w2_ref, a0_ref, a2_ref, g2_ref, kk_ref, ka_ref,
                      r_o, lw_o, k_o, v_o, kk_o, a_o, g_o, *, rw):
    t = pl.program_id(1)
    u = u_ref[0]
    tt = u.shape[0]
    prev_row = jnp.where(t == 0, s0_ref[0], pv_ref[0, SUBLANES - 1:SUBLANES, :])
    row = lax.broadcasted_iota(jnp.int32, u.shape, 0)
    prev = jnp.where(row == 0, prev_row, pltpu.roll(u, 1, axis=0))
    xs = u + (prev - u) * mu_ref[...]
    r = xs[:, 0:rw]
    k = xs[:, rw:2 * rw]
    v = xs[:, 2 * rw:3 * rw]
    lora_wa = xs[:, 3 * rw:3 * rw + LANES]
    gd = xs[:, 3 * rw + LANES:3 * rw + 2 * LANES]
    lora = lambda act, w_ref: _mm3(_split(act), _split(w_ref[...]))
    z = w0_ref[...] + lora(jnp.tanh(lora_wa), w2_ref)
    nz = -z
    softplus = jnp.maximum(nz, 0.0) + jnp.log1p(jnp.exp(-jnp.abs(nz)))
    w_log = -softplus - 0.5
    lw_o[0] = -jnp.exp(w_log)
    a = jax.nn.sigmoid(a0_ref[...] + lora(lora_wa, a2_ref))
    g_o[0] = lora(jax.nn.sigmoid(gd), g2_ref)
    kk = k * kk_ref[...]
    for p in range(rw // LANES):
        ps = slice(p * LANES, (p + 1) * LANES)
        kkp = kk[:, ps]
        kk_o[0, :, ps] = kkp / jnp.maximum(jnp.sqrt(_head_sum(kkp * kkp)), 1e-12)
    r_o[0] = r
    k_o[0] = k * (1.0 + (a - 1.0) * ka_ref[...])
    v_o[0] = v
    a_o[0] = a


def rwkv_prep(ur, shift0, prm):
    b, t, _ = ur.shape
    cols = prm['rw_mu'].shape[-1]
    rw = prm['rw_w0'].shape[-1]
    tt = _row_tile(t, 256)
    nlora = prm['rw_w2'].shape[0]
    assert 2 * nlora == LANES and prm['rw_g2'].shape[0] == LANES and cols == 3 * rw + 2 * LANES
    w2p = jnp.concatenate([prm['rw_w2'], jnp.zeros_like(prm['rw_w2'])], axis=0)
    a2p = jnp.concatenate([jnp.zeros_like(prm['rw_a2']), prm['rw_a2']], axis=0)
    row = lambda z: z.reshape(1, -1)
    full = lambda shape: pl.BlockSpec(shape, lambda i, j: (0,) * len(shape))
    outs = pl.pallas_call(
        functools.partial(_rwkv_prep_kernel, rw=rw),
        out_shape=[jax.ShapeDtypeStruct((b, t, rw), F32)] * 7,
        grid=(b, t // tt),
        in_specs=[
            pl.BlockSpec((1, tt, cols), lambda i, j: (i, j, 0)),
            pl.BlockSpec((1, SUBLANES, cols), lambda i, j: (i, jnp.maximum(j * (tt // SUBLANES) - 1, 0), 0)),
            pl.BlockSpec((1, 1, cols), lambda i, j: (i, 0, 0)),
            full((1, cols)), full((1, rw)), full((LANES, rw)), full((1, rw)), full((LANES, rw)),
            full((LANES, rw)), full((1, rw)), full((1, rw)),
        ],
        out_specs=[pl.BlockSpec((1, tt, rw), lambda i, j: (i, j, 0))] * 7,
        compiler_params=_cparams(("parallel", "parallel")),
        name="rwkv_prep",
    )(ur, ur, shift0.reshape(b, 1, cols), row(prm['rw_mu']), row(prm['rw_w0']), w2p, row(prm['rw_a0']), a2p,
      prm['rw_g2'], row(prm['rw_kk']), row(prm['rw_ka']))
    return outs


def _pair_masks():
    i = lax.broadcasted_iota(jnp.int32, (LANES, LANES), 0)
    j = lax.broadcasted_iota(jnp.int32, (LANES, LANES), 1)
    same = lax.shift_right_logical(i, 6) == lax.shift_right_logical(j, 6)
    return i, j, same


def _split(x):
    hi = x.astype(BF16)
    return hi, (x - hi.astype(F32)).astype(BF16)


def _mm3(a, b):
    n = b[0].shape[1]
    lhs = jnp.concatenate(a, axis=1)
    rhs = jnp.concatenate([jnp.concatenate(b, axis=1),
                           jnp.concatenate([b[0], jnp.zeros_like(b[0])], axis=1)], axis=0)
    out = jnp.dot(lhs, rhs, preferred_element_type=F32)
    return out[:, :n] + out[:, n:]


def _mm3_nt(a, b):
    n = b[0].shape[0]
    lhs = jnp.concatenate(a, axis=1)
    rhs = jnp.concatenate([jnp.concatenate([b[0], b[0]], axis=1),
                           jnp.concatenate([b[1], jnp.zeros_like(b[1])], axis=1)], axis=0)
    out = lax.dot_general(lhs, rhs, NT_DIMS, preferred_element_type=F32)
    return out[:, :n] + out[:, n:]


def _each(f, *lists):
    return [f(*xs) for xs in zip(*lists)]


def _rwkv_chunk_factors(r, lw, k, v, kk, a):
    c = R_CHUNK
    row = lax.broadcasted_iota(jnp.int32, (c, LANES), 0)
    cum = lw
    s = 1
    while s < c:
        cum = _each(lambda z: z + jnp.where(row >= s, pltpu.roll(z, s, axis=0), 0.0), cum)
        s *= 2
    cl = _each(lambda z: z[c - 1:c, :], cum)
    beta = _each(lambda x, y: x * y, kk, a)
    kap_t = _each(lambda x, cu, l: x * jnp.exp(cu - l), kk, cum, lw)
    r_t = _each(lambda x, cu: x * jnp.exp(cu), r, cum)
    e_inv = _each(lambda cu: jnp.exp(-cu), cum)
    b_t = _each(lambda x, e: x * e, beta, e_inv)
    k_t = _each(lambda x, e: x * e, k, e_inv)
    e_end = _each(lambda l, cu: jnp.exp(l - cu), cl, cum)
    b_h = _each(lambda x, e: x * e, beta, e_end)
    k_h = _each(lambda x, e: x * e, k, e_end)

    lo = lax.broadcasted_iota(jnp.int32, (c, LANES), 1) < R_HEAD_DIM
    stack = lambda x: jnp.concatenate([jnp.where(lo, x, 0.0), jnp.where(lo, 0.0, x)], axis=0)
    dup = lambda x: jnp.concatenate([x, x], axis=0)
    split_of = lambda f: (lambda x: _split(f(x)))
    ident = lambda x: x

    i, j, same = _pair_masks()
    strict = same & (j < i)
    incl = same & (j <= i)
    kap_s, r_s, v_s = _each(stack, kap_t), _each(stack, r_t), _each(stack, v)
    kap_p, r_p, v_p = _each(_split, kap_s), _each(_split, r_s), _each(_split, v_s)
    b_d, k_d = _each(split_of(dup), b_t), _each(split_of(dup), k_t)
    a_ab = _each(lambda x, y: jnp.where(strict, _mm3_nt(x, y), 0.0), kap_p, b_d)
    a_ak = _each(lambda x, y: jnp.where(strict, _mm3_nt(x, y), 0.0), kap_p, k_d)
    a_rb = _each(lambda x, y: jnp.where(incl, _mm3_nt(x, y), 0.0), r_p, b_d)
    a_rk = _each(lambda x, y: jnp.where(incl, _mm3_nt(x, y), 0.0), r_p, k_d)

    x = [jnp.where(i == j, 1.0, 0.0).astype(F32)] * len(r)
    s = 1
    while s < c:
        low = (lax.shift_right_logical(i, s.bit_length()) == lax.shift_right_logical(j, s.bit_length())) \
            & ((i & (2 * s - 1)) >= s) & ((j & (2 * s - 1)) < s)
        xp = _each(_split, x)
        ms = _each(lambda z: _split(jnp.where(low, z, 0.0)), a_ab)
        t1 = _each(split_of(ident), _each(_mm3, xp, ms))
        x = _each(lambda z, u, w_: z - _mm3(u, w_), x, t1, xp)
        s *= 2

    aakv = _each(_mm3, _each(_split, a_ak), v_p)
    arkv = _each(_mm3, _each(_split, a_rk), v_p)
    bh_p = _each(split_of(stack), b_h)
    vtk = _each(_mm3, _each(lambda z: _split(z.T), v_s), _each(split_of(stack), k_h))
    rhs = _each(lambda z, u: _split(jnp.concatenate([z, u], axis=1)), kap_s, aakv)
    wu = _each(lambda z, u: -_mm3(z, u), _each(_split, x), rhs)
    w = _each(lambda z: z[:, :LANES], wu)
    upre = _each(lambda z: z[:, LANES:], wu)
    arb_p = _each(_split, a_rb)
    r2 = _each(lambda z, u, w_: z + _mm3(u, _split(w_)), r_s, arb_p, w)
    ypre = _each(lambda z, u, w_: z + _mm3(u, _split(w_)), arkv, arb_p, upre)
    g = _each(lambda l, w_, u: jnp.where(i == j, jnp.exp(l), 0.0) + _mm3(_split(w_.T), u), cl, w, bh_p)
    spre = _each(lambda z, w_, u: z + _mm3(_split(w_.T), u), vtk, upre, bh_p)
    return list(zip(r2, ypre, g, spre))


def _rwkv_core_kernel(r_ref, lw_ref, k_ref, v_ref, kk_ref, a_ref, s0_ref, y_o, st_o, s_scr, *, nch, npair):
    cidx = pl.program_id(1)

    @pl.when(cidx == 0)
    def _():
        s_scr[...] = s0_ref[0]

    c = R_CHUNK
    parts = lambda ref: [ref[0, ch * c:(ch + 1) * c, p * LANES:(p + 1) * LANES]
                         for ch in range(nch) for p in range(npair)]
    factors = _rwkv_chunk_factors(parts(r_ref), parts(lw_ref), parts(k_ref), parts(v_ref), parts(kk_ref),
                                  parts(a_ref))
    s = [s_scr[p] for p in range(npair)]
    for ch in range(nch):
        for p in range(npair):
            r2, ypre, g, spre = factors[ch * npair + p]
            sp = _split(s[p])
            ys = _mm3_nt(_split(r2), sp) + ypre
            y_o[0, ch * c:(ch + 1) * c, p * LANES:(p + 1) * LANES] = ys[:c] + ys[c:]
            s[p] = _mm3(sp, _split(g)) + spre
    for p in range(npair):
        s_scr[p] = s[p]
        st_o[0, p] = s[p]


def rwkv_scan(r, lw, k, v, kk, a, wkv0):
    b, t, rw = r.shape
    npair = rw // LANES
    assert t % R_CHUNK == 0 and 2 * R_HEAD_DIM == LANES
    nch = 2 if t % (2 * R_CHUNK) == 0 else 1
    tt = nch * R_CHUNK
    w4 = wkv0.reshape(b, npair, 2, R_HEAD_DIM, R_HEAD_DIM)
    zero = jnp.zeros_like(w4[:, :, 0])
    s0 = jnp.concatenate([jnp.concatenate([w4[:, :, 0], zero], axis=-1),
                          jnp.concatenate([zero, w4[:, :, 1]], axis=-1)], axis=-2)
    act = pl.BlockSpec((1, tt, rw), lambda i, c: (i, c, 0))
    st_spec = pl.BlockSpec((1, npair, LANES, LANES), lambda i, c: (i, 0, 0, 0))
    y, st = pl.pallas_call(
        functools.partial(_rwkv_core_kernel, nch=nch, npair=npair),
        out_shape=[jax.ShapeDtypeStruct((b, t, rw), F32), jax.ShapeDtypeStruct((b, npair, LANES, LANES), F32)],
        grid=(b, t // tt),
        in_specs=[act] * 6 + [st_spec],
        out_specs=[act, st_spec],
        scratch_shapes=[pltpu.VMEM((npair, LANES, LANES), F32)],
        compiler_params=_cparams(("parallel", "arbitrary")),
        name="rwkv_core",
    )(r, lw, k, v, kk, a, s0)
    wkv = jnp.stack([st[:, :, :R_HEAD_DIM, :R_HEAD_DIM], st[:, :, R_HEAD_DIM:, R_HEAD_DIM:]], axis=2)
    return y, wkv.reshape(wkv0.shape)


def _rwkv_post_kernel(y_ref, r_ref, k_ref, v_ref, g_ref, rk_ref, lw_ref, lb_ref, o_ref):
    inv_n = 1.0 / R_HEAD_DIM
    hsum = _head_sum
    for p in range(y_ref.shape[2] // LANES):
        ps = slice(p * LANES, (p + 1) * LANES)
        y = y_ref[0, :, ps]
        mean = hsum(y) * inv_n
        d = y - mean
        var = hsum(d * d) * inv_n
        yn = d * lax.rsqrt(var + R_GN_EPS) * lw_ref[:, ps] + lb_ref[:, ps]
        bonus = hsum(r_ref[0, :, ps] * k_ref[0, :, ps] * rk_ref[:, ps]) * v_ref[0, :, ps]
        o_ref[0, :, ps] = (yn + bonus) * g_ref[0, :, ps]


def rwkv_post(y, r, k, v, g, prm):
    b, t, rw = y.shape
    tt = _row_tile(t, 256)
    act = pl.BlockSpec((1, tt, rw), lambda i, j: (i, j, 0))
    par = pl.BlockSpec((1, rw), lambda i, j: (0, 0))
    return pl.pallas_call(
        _rwkv_post_kernel,
        out_shape=jax.ShapeDtypeStruct((b, t, rw), F32),
        grid=(b, t // tt),
        in_specs=[act] * 5 + [par] * 3,
        out_specs=act,
        compiler_params=_cparams(("parallel", "parallel")),
        name="rwkv_post",
    )(y, r, k, v, g, prm['rw_rk'].reshape(1, rw), prm['rw_lnx_w'].reshape(1, rw), prm['rw_lnx_b'].reshape(1, rw))


def rwkv_mix(ur, shift0, wkv0, prm):
    b, t, _ = ur.shape
    tp = -(-t // SUBLANES) * SUBLANES
    urp = jnp.pad(ur, ((0, 0), (0, tp - t), (0, 0)))
    r, lw, k, v, kk, a, g = rwkv_prep(urp, shift0, prm)
    tc = -(-t // R_CHUNK) * R_CHUNK
    fit = lambda z: jnp.pad(z[:, :t], ((0, 0), (0, tc - t), (0, 0)))
    r, lw, k, v, kk, a, g = (fit(z) for z in (r, lw, k, v, kk, a, g))
    y, wkv = rwkv_scan(r, lw, k, v, kk, a, wkv0)
    out = rwkv_post(y, r, k, v, g, prm)
    return out[:, :t], wkv, ur[:, -1, :prm['rw_mu'].shape[-1]]


def _pool_kernel(u_ref, pv_ref, p0_ref, w_ref, sc_ref, o_ref, *, pos0):
    t = pl.program_id(1)
    cur = u_ref[0]
    tt = cur.shape[0]
    ext = jnp.concatenate([jnp.where(t == 0, p0_ref[0], pv_ref[0]), cur], axis=0)
    sums = []
    s = ext
    w = 1
    while w < POOL_MAX:
        s = s + pltpu.roll(s, w, axis=0)
        w *= 2
        sums.append(s)
    pos = pos0 + t * tt + lax.broadcasted_iota(jnp.int32, (tt, 1), 0)
    for gi, w in enumerate(POOL_WINDOWS):
        gs = slice(gi * LANES, (gi + 1) * LANES)
        win = sums[w.bit_length() - 2][POOL_MAX:, gs]
        cnt = jnp.minimum(pos + 1, w).astype(F32)
        m = win / cnt - cur[:, gs]
        z = jnp.dot(m.astype(BF16), w_ref[gi], preferred_element_type=F32)
        o_ref[0, :, gs] = z * sc_ref[:, gs]


def pool_mix(uc, pool0, pos0, pool_w, pool_scale):
    b, t, width = uc.shape
    assert width == len(POOL_WINDOWS) * LANES and pool_w.shape[1] == LANES
    tp = -(-t // SUBLANES) * SUBLANES
    ucp = jnp.pad(uc, ((0, 0), (0, tp - t), (0, 0)))
    tt = _row_tile(tp, 256)
    assert tt % POOL_MAX == 0 or tp == tt
    p0 = jnp.pad(pool0, ((0, 0), (1, 0), (0, 0)))
    nprev = tt // POOL_MAX if tt % POOL_MAX == 0 else 0
    ext = jnp.concatenate([pool0, uc], axis=1)
    if nprev:
        pv, pv_spec = ucp, pl.BlockSpec((1, POOL_MAX, width), lambda i, j: (i, jnp.maximum(j * nprev - 1, 0), 0))
    else:
        pv, pv_spec = p0, pl.BlockSpec((1, POOL_MAX, width), lambda i, j: (i, 0, 0))
    z = pl.pallas_call(
        functools.partial(_pool_kernel, pos0=pos0),
        out_shape=jax.ShapeDtypeStruct((b, tp, width), F32),
        grid=(b, tp // tt),
        in_specs=[
            pl.BlockSpec((1, tt, width), lambda i, j: (i, j, 0)),
            pv_spec,
            pl.BlockSpec((1, POOL_MAX, width), lambda i, j: (i, 0, 0)),
            pl.BlockSpec(pool_w.shape, lambda i, j: (0, 0, 0)),
            pl.BlockSpec((1, width), lambda i, j: (0, 0)),
        ],
        out_specs=pl.BlockSpec((1, tt, width), lambda i, j: (i, j, 0)),
        compiler_params=_cparams(("parallel", "parallel")),
        name="pool_mix",
    )(ucp, pv, p0, pool_w.astype(BF16), pool_scale.reshape(1, width))
    return z[:, :t], ext[:, -(POOL_MAX - 1):]


def _mix_out_kernel(x_ref, ya_ref, yr_ref, yc_ref, wa_ref, wr_ref, wc_ref, o_ref):
    acc = jnp.dot(ya_ref[...].astype(BF16), wa_ref[...], preferred_element_type=F32)
    acc += jnp.dot(yr_ref[...].astype(BF16), wr_ref[...], preferred_element_type=F32)
    acc += jnp.dot(yc_ref[...].astype(BF16), wc_ref[...], preferred_element_type=F32)
    o_ref[...] = x_ref[...] + acc


def mix_out(x, ya, yr, yc, w_out, layer):
    n, d = x.shape
    wa_, wr_, wc_ = ya.shape[1], yr.shape[1], yc.shape[1]
    assert wa_ % wr_ == 0 and wr_ == wc_
    tm = _row_tile(n, 512)
    tn = d
    act = lambda wd: pl.BlockSpec((tm, wd), lambda i, j: (i, 0))
    wsp = lambda wd, blk: pl.BlockSpec((None, wd, tn), lambda i, j: (layer, blk, j))
    return pl.pallas_call(
        _mix_out_kernel,
        out_shape=jax.ShapeDtypeStruct((n, d), F32),
        grid=(n // tm, d // tn),
        in_specs=[pl.BlockSpec((tm, tn), lambda i, j: (i, j)), act(wa_), act(wr_), act(wc_),
                  wsp(wa_, 0), wsp(wr_, wa_ // wr_), wsp(wc_, (wa_ + wr_) // wc_)],
        out_specs=pl.BlockSpec((tm, tn), lambda i, j: (i, j)),
        compiler_params=_cparams(("parallel", "arbitrary")),
        name="mix_out",
    )(x, ya, yr, yc, w_out, w_out, w_out)


def _cross_kernel(x_ref, g_ref, wq_ref, mk_ref, mv_ref, wo_ref, o_ref, *, nh):
    x = x_ref[0]
    h = _rms(x, g_ref[...]).astype(BF16)
    q = jnp.dot(h, wq_ref[...], preferred_element_type=F32)
    scale = M_HEAD_DIM ** -0.5
    outs = []
    for hh in range(nh):
        hs = slice(hh * M_HEAD_DIM, (hh + 1) * M_HEAD_DIM)
        s = lax.dot_general(q[:, hs].astype(BF16), mk_ref[0, :, hs].astype(BF16), NT_DIMS,
                            preferred_element_type=F32) * scale
        p = jnp.exp(s - s.max(axis=-1, keepdims=True))
        l = jnp.sum(p, axis=-1, keepdims=True)
        outs.append(jnp.dot(p.astype(BF16), mv_ref[0, :, hs].astype(BF16), preferred_element_type=F32) / l)
    o = jnp.concatenate(outs, axis=-1).astype(BF16)
    o_ref[0] = x + jnp.dot(o, wo_ref[...], preferred_element_type=F32)


def cross_attend(x, g, wq, mk, mv, wo, mem_layer):
    b, t0, d = x.shape
    t = -(-t0 // SUBLANES) * SUBLANES
    x = jnp.pad(x, ((0, 0), (0, t - t0), (0, 0)))
    mw = wq.shape[1]
    nm = mk.shape[2]
    tm = _row_tile(t, 512)
    out = pl.pallas_call(
        functools.partial(_cross_kernel, nh=mw // M_HEAD_DIM),
        out_shape=jax.ShapeDtypeStruct((b, t, d), F32),
        grid=(b, t // tm),
        in_specs=[
            pl.BlockSpec((1, tm, d), lambda i, j: (i, j, 0)),
            pl.BlockSpec((1, d), lambda i, j: (0, 0)),
            pl.BlockSpec((d, mw), lambda i, j: (0, 0)),
            pl.BlockSpec((None, 1, nm, mw), lambda i, j: (mem_layer, i, 0, 0)),
            pl.BlockSpec((None, 1, nm, mw), lambda i, j: (mem_layer, i, 0, 0)),
            pl.BlockSpec((mw, d), lambda i, j: (0, 0)),
        ],
        out_specs=pl.BlockSpec((1, tm, d), lambda i, j: (i, j, 0)),
        compiler_params=_cparams(("parallel", "parallel")),
        name="cross_attend",
    )(x, g.reshape(1, d), wq, mk, mv, wo)
    return out[:, :t0]


def _layer(x, prm, moba_fn, pos0, shift0, wkv0, pool0, mem_k, mem_v, mem_layer, final_gain):
    b, t, d = x.shape
    n = b * t
    lyr, stk = prm['layer'], prm['stacked']
    x2 = ffn_half_step(x.reshape(n, d), prm['norm_ffn1'], stk['ffn1_gate'], stk['ffn1_up'], stk['ffn1_down'], lyr)
    aw, rc, cw = prm['a_width'], prm['r_cols'], prm['c_width']
    qa, ka, va, ur, uc = (z.reshape(b, t, -1) for z in
                          in_proj(x2, prm['norm_mix'], stk['w_in'], lyr, (aw, aw, aw, rc, cw)))
    ya = moba_fn(qa, ka, va)
    yr, wkv, shift = rwkv_mix(ur, shift0, wkv0, prm)
    yc, pool_buf = pool_mix(uc, pool0, pos0, prm['pool_w'], prm['pool_scale'])
    x3 = mix_out(x2, ya.reshape(n, aw), yr.reshape(n, -1), yc.reshape(n, cw), stk['w_out'], lyr)
    x4 = cross_attend(x3.reshape(b, t, d), prm['norm_cross'], prm['mem_wq'], mem_k, mem_v, prm['mem_wo'],
                      mem_layer)
    x5 = ffn_half_step(x4.reshape(n, d), prm['norm_ffn2'], stk['ffn2_gate'], stk['ffn2_up'], stk['ffn2_down'], lyr,
                       final_gain)
    return x5.reshape(b, t, d), ka, va, wkv, shift, pool_buf


def kernel(x_prompt, x_sample, cache_k, cache_v, cache_mem_k, cache_mem_v, state_wkv, state_shift, state_pool, page_table, mem_prompt, norm_ffn1, ffn1_gate, ffn1_up, ffn1_down, norm_mix, w_in, w_out, rw_mu, rw_w0, rw_w2, rw_a0, rw_a2, rw_g2, rw_kk, rw_ka, rw_rk, rw_lnx_w, rw_lnx_b, pool_w, pool_scale, norm_cross, norm_mem, mem_wq, mem_wk, mem_wv, mem_wo, norm_ffn2, ffn2_gate, ffn2_up, ffn2_down, norm_final):
    depth = w_in.shape[0]
    bp, tp, d = x_prompt.shape
    db, ts, _ = x_sample.shape
    n_heads, hd = cache_k.shape[3], cache_k.shape[4]
    aw = n_heads * hd
    r_heads, rn = rw_rk.shape[1], rw_rk.shape[2]
    r_cols = rw_mu.shape[1]
    c_width = pool_scale.shape[1]
    nm, m_heads, mhd = cache_mem_k.shape[2:]
    mw = m_heads * mhd
    assert hd == A_HEAD_DIM and rn == R_HEAD_DIM and mhd == M_HEAD_DIM
    past = page_table.shape[1] * PAGE_SIZE
    slopes = jnp.exp2(-8.0 * jnp.arange(1, n_heads + 1, dtype=F32) / n_heads)
    ck = cache_k.reshape(depth, cache_k.shape[1], PAGE_SIZE * n_heads, hd)
    cv = cache_v.reshape(depth, cache_v.shape[1], PAGE_SIZE * n_heads, hd)
    bf = lambda z: z.astype(BF16)
    bounds = (0, aw, 2 * aw, 3 * aw, 3 * aw + r_cols, w_in.shape[2])
    groups = [bf(w_in[:, :, lo:hi]) for lo, hi in zip(bounds[:-1], bounds[1:])]
    w_in_p = jnp.concatenate([jnp.pad(gw, ((0, 0), (0, 0), (0, -gw.shape[2] % IN_TILE))) for gw in groups], axis=2)
    stacked = dict(ffn1_gate=bf(ffn1_gate), ffn1_up=bf(ffn1_up), ffn1_down=bf(ffn1_down), w_in=w_in_p,
                   w_out=bf(w_out), ffn2_gate=bf(ffn2_gate), ffn2_up=bf(ffn2_up), ffn2_down=bf(ffn2_down))
    big = dict(mem_wq=bf(mem_wq), mem_wk=bf(mem_wk), mem_wv=bf(mem_wv), mem_wo=bf(mem_wo))
    small = dict(norm_ffn1=norm_ffn1, norm_mix=norm_mix, rw_mu=rw_mu, rw_w0=rw_w0, rw_w2=rw_w2, rw_a0=rw_a0,
                 rw_a2=rw_a2, rw_g2=rw_g2, rw_kk=rw_kk, rw_ka=rw_ka, rw_rk=rw_rk.reshape(depth, -1),
                 rw_lnx_w=rw_lnx_w, rw_lnx_b=rw_lnx_b, pool_w=pool_w, pool_scale=pool_scale,
                 norm_cross=norm_cross, norm_ffn2=norm_ffn2)

    shift0 = jnp.zeros((bp, r_cols), F32)
    wkv0 = jnp.zeros((bp, r_heads, rn, rn), F32)
    pool0 = jnp.zeros((bp, POOL_MAX - 1, c_width), F32)
    xp, xs = x_prompt, x_sample
    cmk = cache_mem_k.reshape(depth, db, nm, mw)
    cmv = cache_mem_v.reshape(depth, db, nm, mw)
    outs = [[] for _ in range(12)]
    for l in range(depth):
        prm = {k: v[l] for k, v in big.items()}
        prm.update({k: v[l] for k, v in small.items()})
        prm.update(a_width=aw, r_cols=r_cols, c_width=c_width, layer=l, stacked=stacked)
        hm = rmsnorm(mem_prompt.reshape(bp * nm, d), norm_mem[l], BF16)
        mk = matmul(hm, prm['mem_wk'], mw).reshape(bp, nm, mw)
        mv = matmul(hm, prm['mem_wv'], mw).reshape(bp, nm, mw)
        moba_p = functools.partial(moba_prompt, slopes=slopes)
        final_gain = norm_final if l == depth - 1 else None
        xp, k_, v_, w_, sh_, pl_ = _layer(xp, prm, moba_p, 0, shift0, wkv0, pool0, mk[None], mv[None], 0,
                                           final_gain)
        res_p = (k_, v_, w_, sh_, pl_, mk, mv)
        moba_s = functools.partial(moba_sample, cache_k=ck, cache_v=cv, layer=l, page_table=page_table,
                                   slopes=slopes)
        xs, k_, v_, w_, sh_, pl_ = _layer(xs, prm, moba_s, past, state_shift[l], state_wkv[l], state_pool[l],
                                           cmk, cmv, l, final_gain)
        res_s = (k_, v_, w_, sh_, pl_)
        for i, z in enumerate(res_p + res_s):
            outs[i].append(z)
    y_prompt, y_sample = xp, xs
    stk = [jnp.stack(o) for o in outs]
    heads = lambda z, nh_, hd_: z.reshape(z.shape[:-1] + (nh_, hd_))
    for i in (0, 1, 7, 8):
        stk[i] = heads(stk[i], n_heads, hd)
    for i in (5, 6):
        stk[i] = heads(stk[i], m_heads, mhd)
    return (y_prompt, y_sample) + tuple(stk)
```
